```python
import jax, jax.numpy as jnp
from jax import lax
import numpy as np

D_MODEL = 2048
BATCH = 1
SEQ = 8192
DEPTH = 4

HGRN_HEADS = 8
HGRN_KEY_DIM = 128
HGRN_VAL_DIM = 128
HGRN_WIDTH = HGRN_HEADS * HGRN_KEY_DIM
HGRN_CHUNK = 64
MIN_FORGET = 1e-30
ATTN_GROUPS = ((128, 1), (512, 4), (2048, 16))
HEADS_PER_GROUP = 4
ATTN_HEAD_DIM = 128
N_ATTN_HEADS = HEADS_PER_GROUP * len(ATTN_GROUPS)
ATTN_WIDTH = N_ATTN_HEADS * ATTN_HEAD_DIM
ATTN_OUT_WIDTH = HEADS_PER_GROUP * ATTN_HEAD_DIM
ATTN_BLOCK = 128
REL_BUCKETS = 32
REL_MAX_DISTANCE = 1024
D_FF = 4 * D_MODEL
NORM_EPS = 1e-6
NEG_INF = -1e30
IN_SPLITS = (HGRN_WIDTH,) * 5 + (ATTN_WIDTH,) * 3 + (D_MODEL,) * 2
N_IN = sum(IN_SPLITS)

kernel_name = "hybrid_hgrn2_dilated_attn_encoder"


def rms_norm(x, w):
    xf = x.astype(jnp.float32)
    y = xf * lax.rsqrt(jnp.mean(xf * xf, axis=-1, keepdims=True) + NORM_EPS)
    return (y * w.astype(jnp.float32)).astype(x.dtype)


def layer_lower_bounds(logits):
    p = jax.nn.softmax(logits.astype(jnp.float32), axis=0)
    return jnp.cumsum(p, axis=0) - p[0:1]


def hgrn2_chunk_scan(q, k, log_f, v):
    B, S, H, K = q.shape
    V = v.shape[-1]
    C = HGRN_CHUNK
    n = S // C

    def to_chunks(a):
        return a.reshape(B, n, C, H, a.shape[-1]).transpose(1, 0, 3, 2, 4)

    causal_in_chunk = jnp.asarray(np.tril(np.ones((C, C), dtype=bool)))[:, :, None]

    def step(state, inp):
        qb, kb, gb, vb = inp
        b = jnp.cumsum(gb, axis=2)
        diff = b[:, :, :, None, :] - b[:, :, None, :, :]
        decay = jnp.where(causal_in_chunk, jnp.exp(jnp.where(causal_in_chunk, diff, 0.0)), 0.0)
        scores = jnp.einsum('bhtk,bhsk,bhtsk->bhts', qb, kb, decay)
        o = (jnp.einsum('bhts,bhsv->bhtv', scores, vb)
             + jnp.einsum('bhtk,bhkv->bhtv', qb * jnp.exp(b), state))
        b_last = b[:, :, -1:, :]
        state = (state * jnp.exp(b_last)[:, :, 0, :, None]
                 + jnp.einsum('bhsk,bhsv->bhkv', kb * jnp.exp(b_last - b), vb))
        return state, o

    state0 = jnp.zeros((B, H, K, V), jnp.float32)
    _, o = lax.scan(step, state0, (to_chunks(q), to_chunks(k), to_chunks(log_f), to_chunks(v)))
    return o.transpose(1, 0, 3, 2, 4).reshape(B, S, H, V)


def hgrn2_branch(q, z_fwd, z_bwd, i, g, lb_fwd, lb_bwd, norm_w):
    B, S, _ = q.shape

    def heads(a, d):
        return a.reshape(B, S, HGRN_HEADS, d).astype(jnp.float32)

    qh = heads(q, HGRN_KEY_DIM)
    vh = heads(i, HGRN_VAL_DIM)

    def gates(z, lb):
        z = heads(z, HGRN_KEY_DIM)
        lb = lb.reshape(HGRN_HEADS, HGRN_KEY_DIM)
        f = lb + (1.0 - lb) * jax.nn.sigmoid(z)
        log_f = jnp.log(jnp.maximum(f, MIN_FORGET))
        key = (1.0 - lb) * jax.nn.sigmoid(-z)
        return log_f, key

    lf_f, k_f = gates(z_fwd, lb_fwd)
    lf_b, k_b = gates(z_bwd, lb_bwd)
    rev = lambda a: jnp.flip(a, axis=1)
    o_f = hgrn2_chunk_scan(qh, k_f, lf_f, vh)
    o_b = rev(hgrn2_chunk_scan(rev(qh), rev(k_b), rev(lf_b), rev(vh)))
    o = o_f + o_b
    o = o * lax.rsqrt(jnp.mean(o * o, axis=-1, keepdims=True) + NORM_EPS)
    o = o.reshape(B, S, HGRN_WIDTH) * norm_w.astype(jnp.float32) * jax.nn.silu(g.astype(jnp.float32))
    return o.astype(q.dtype)


def t5_bucket(rel):
    half = REL_BUCKETS // 2
    ret = (rel > 0).astype(np.int32) * half
    n = np.abs(rel)
    max_exact = half // 2
    large = max_exact + (np.log(np.maximum(n, 1) / max_exact)
                         / np.log(REL_MAX_DISTANCE / max_exact)
                         * (half - max_exact)).astype(np.int32)
    large = np.minimum(large, half - 1)
    return (ret + np.where(n < max_exact, n, large)).astype(np.int32)


def dilated_group_attention(q, k, v, bias_table, n_side, dil):
    B, S, G, Dh = q.shape
    L = S // dil

    def sub(a):
        return a.reshape(B, L, dil, G, Dh).transpose(0, 2, 3, 1, 4)

    qs, ks, vs = sub(q), sub(k), sub(v)
    qb_len = min(ATTN_BLOCK, L)
    nb = -(-L // qb_len)
    lp = nb * qb_len
    kw = qb_len + 2 * n_side
    pad_q = ((0, 0), (0, 0), (0, 0), (0, lp - L), (0, 0))
    pad_k = ((0, 0), (0, 0), (0, 0), (n_side, lp - L + n_side), (0, 0))
    qs = jnp.pad(qs, pad_q).reshape(B, dil, G, nb, qb_len, Dh)
    key_idx = np.arange(nb)[:, None] * qb_len + np.arange(kw)[None, :]
    kb = jnp.take(jnp.pad(ks, pad_k), key_idx, axis=3)
    vb = jnp.take(jnp.pad(vs, pad_k), key_idx, axis=3)
    rel = np.arange(kw)[None, :] - n_side - np.arange(qb_len)[:, None]
    key_pos = key_idx - n_side
    valid = ((np.abs(rel) <= n_side)[None]
             & ((key_pos >= 0) & (key_pos < L))[:, None, :])
    bias = bias_table[t5_bucket(rel * dil)]
    bias = jnp.transpose(bias, (2, 0, 1))[:, None].astype(jnp.float32)
    scale = ATTN_HEAD_DIM ** -0.5
    s = jnp.einsum('brgnqd,brgnkd->brgnqk', qs, kb).astype(jnp.float32) * scale + bias
    s = jnp.where(jnp.asarray(valid), s, NEG_INF)
    lse = jax.nn.logsumexp(s, axis=-1)
    p = jnp.exp(s - lse[..., None])
    o = jnp.einsum('brgnqk,brgnkd->brgnqd', p, vb.astype(jnp.float32))
    o = o.reshape(B, dil, G, lp, Dh)[:, :, :, :L].transpose(0, 3, 1, 2, 4).reshape(B, S, G, Dh)
    lse = lse.reshape(B, dil, G, lp)[:, :, :, :L].transpose(0, 3, 1, 2).reshape(B, S, G)
    return o, lse


def dilated_attention_branch(q, k, v, rel_bias_table):
    B, S, _ = q.shape
    shp = lambda a: a.reshape(B, S, N_ATTN_HEADS, ATTN_HEAD_DIM)
    q, k, v = shp(q), shp(k), shp(v)
    outs, lses = [], []
    for gi, (window, dil) in enumerate(ATTN_GROUPS):
        hs = slice(gi * HEADS_PER_GROUP, (gi + 1) * HEADS_PER_GROUP)
        o, lse = dilated_group_attention(q[:, :, hs], k[:, :, hs], v[:, :, hs],
                                         rel_bias_table[:, hs], window // (2 * dil), dil)
        outs.append(o)
        lses.append(lse)
    alpha = jax.nn.softmax(jnp.stack(lses), axis=0)
    out = jnp.einsum('nbsg,nbsgd->bsgd', alpha, jnp.stack(outs))
    return out.reshape(B, S, ATTN_OUT_WIDTH).astype(q.dtype)


def setup_inputs(seed: int = 0) -> dict:
    key = jax.random.key(seed)
    ks = jax.random.split(key, 16)
    nrm = lambda k, shape, scale: jax.random.normal(k, shape, jnp.float32) * scale
    return {
        "x": nrm(ks[0], (BATCH, SEQ, D_MODEL), 1.0),
        "w_in": nrm(ks[1], (DEPTH, D_MODEL, N_IN), D_MODEL ** -0.5),
        "hgrn_lb_fwd": nrm(ks[2], (DEPTH, HGRN_WIDTH), 0.5),
        "hgrn_lb_bwd": nrm(ks[3], (DEPTH, HGRN_WIDTH), 0.5),
        "hgrn_norm_w": 1.0 + nrm(ks[4], (DEPTH, HGRN_WIDTH), 0.02),
        "rel_bias_table": nrm(ks[5], (REL_BUCKETS, N_ATTN_HEADS), 0.5),
        "w_branch_hgrn": nrm(ks[6], (DEPTH, HGRN_WIDTH, D_MODEL), HGRN_WIDTH ** -0.5),
        "w_branch_attn": nrm(ks[7], (DEPTH, ATTN_OUT_WIDTH, D_MODEL), ATTN_OUT_WIDTH ** -0.5),
        "w_out": nrm(ks[8], (DEPTH, D_MODEL, D_MODEL), D_MODEL ** -0.5),
        "norm_mix_w": 1.0 + nrm(ks[9], (DEPTH, D_MODEL), 0.02),
        "norm_mlp_w": 1.0 + nrm(ks[10], (DEPTH, D_MODEL), 0.02),
        "w_up": nrm(ks[11], (DEPTH, D_MODEL, D_FF), D_MODEL ** -0.5),
        "w_down": nrm(ks[12], (DEPTH, D_FF, D_MODEL), D_FF ** -0.5),
        "final_norm_w": 1.0 + nrm(ks[13], (D_MODEL,), 0.02),
    }


def reference(x, w_in, hgrn_lb_fwd, hgrn_lb_bwd, hgrn_norm_w, rel_bias_table,
              w_branch_hgrn, w_branch_attn, w_out, norm_mix_w, norm_mlp_w,
              w_up, w_down, final_norm_w):
    lb_fwd_all = layer_lower_bounds(hgrn_lb_fwd)
    lb_bwd_all = layer_lower_bounds(hgrn_lb_bwd)
    split_points = list(np.cumsum(IN_SPLITS)[:-1])
    for l in range(DEPTH):
        h = rms_norm(x, norm_mix_w[l])
        proj = jnp.einsum('bsd,dn->bsn', h, w_in[l])
        (hq, hzf, hzb, hi, hg, aq, ak, av, gate_a, gate_b) = jnp.split(proj, split_points, axis=-1)
        o_hgrn = hgrn2_branch(hq, hzf, hzb, hi, hg, lb_fwd_all[l], lb_bwd_all[l], hgrn_norm_w[l])
        o_attn = dilated_attention_branch(aq, ak, av, rel_bias_table)
        merged = (jax.nn.sigmoid(gate_a) * jnp.einsum('bsc,cd->bsd', o_hgrn, w_branch_hgrn[l])
                  + jax.nn.sigmoid(gate_b) * jnp.einsum('bsc,cd->bsd', o_attn, w_branch_attn[l]))
        x = x + jnp.einsum('bsd,de->bse', merged, w_out[l])
        h2 = rms_norm(x, norm_mlp_w[l])
        u = jnp.square(jax.nn.relu(jnp.einsum('bsd,df->bsf', h2, w_up[l])))
        x = x + jnp.einsum('bsf,fd->bsd', u, w_down[l])
    return rms_norm(x, final_norm_w)
```

```python
import functools

import jax
import jax.numpy as jnp
import numpy as np
from jax import lax
from jax.experimental import pallas as pl
from jax.experimental.pallas import tpu as pltpu

F32 = jnp.float32
BF16 = jnp.bfloat16

LANES = 128
HGRN_HEADS = 8
HGRN_WIDTH = HGRN_HEADS * LANES
CHUNK = 128
N_LEVELS = 7
MIN_FORGET = 1e-30
ATTN_GROUPS = ((128, 1), (512, 4), (2048, 16))
HEADS_PER_GROUP = 4
N_ATTN_HEADS = HEADS_PER_GROUP * len(ATTN_GROUPS)
ATTN_WIDTH = N_ATTN_HEADS * LANES
ATTN_OUT_WIDTH = HEADS_PER_GROUP * LANES
ATTN_QBLK = 128
ATTN_SIDE = 64
REL_BUCKETS = 32
REL_MAX_DISTANCE = 1024
NORM_EPS = 1e-6
NEG_INF = -1e30
VMEM_LIMIT = 56 * 1024 * 1024

_NT = (((1,), (1,)), ((), ()))
_TN = (((0,), (0,)), ((), ()))


def _params(*sem):
    return pltpu.CompilerParams(dimension_semantics=sem, vmem_limit_bytes=VMEM_LIMIT)


def _rms(x, w):
    return x * lax.rsqrt(jnp.mean(x * x, axis=-1, keepdims=True) + NORM_EPS) * w


def _sigmoid(x):
    e = jnp.exp(-jnp.abs(x))
    r = 1.0 / (1.0 + e)
    return jnp.where(x >= 0, r, e * r)


def _norm_kernel(x_ref, w_ref, o_ref):
    o_ref[...] = _rms(x_ref[...], w_ref[...]).astype(o_ref.dtype)


def _norm(x, w, tm=512):
    s, d = x.shape
    return pl.pallas_call(
        _norm_kernel,
        grid=(s // tm,),
        in_specs=[pl.BlockSpec((tm, d), lambda i: (i, 0)),
                  pl.BlockSpec((1, d), lambda i: (0, 0))],
        out_specs=pl.BlockSpec((tm, d), lambda i: (i, 0)),
        out_shape=jax.ShapeDtypeStruct((s, d), BF16),
        compiler_params=_params("parallel"),
        name="rmsnorm_in",
    )(x, w.reshape(1, d))


def _proj_kernel(x_ref, w_ref, o_ref, *, slabs):
    acc = jnp.dot(x_ref[...], w_ref[...], preferred_element_type=F32)
    if slabs:
        for j in range(slabs):
            o_ref[j] = acc[:, j * LANES:(j + 1) * LANES].astype(o_ref.dtype)
    else:
        o_ref[...] = acc.astype(o_ref.dtype)


def _proj(h, w, layer, col0, n, out_dtype, slab_major, name, tm=1024, tn=512):
    s, d = h.shape
    off = col0 // tn
    if slab_major:
        slabs = tn // LANES
        out_shape = jax.ShapeDtypeStruct((n // LANES, s, LANES), out_dtype)
        out_spec = pl.BlockSpec((slabs, tm, LANES), lambda i, j: (j, i, 0))
    else:
        slabs = 0
        out_shape = jax.ShapeDtypeStruct((s, n), out_dtype)
        out_spec = pl.BlockSpec((tm, tn), lambda i, j: (i, j))
    return pl.pallas_call(
        functools.partial(_proj_kernel, slabs=slabs),
        grid=(s // tm, n // tn),
        in_specs=[pl.BlockSpec((tm, d), lambda i, j: (i, 0)),
                  pl.BlockSpec((None, d, tn), lambda i, j: (layer, 0, j + off))],
        out_specs=out_spec,
        out_shape=out_shape,
        compiler_params=_params("parallel", "arbitrary"),
        name=name,
    )(h, w)


def _hgrn_constants():
    c = CHUNK
    t = np.arange(c)[:, None]
    s = np.arange(c)[None, :]
    mats = [s <= t, s > t]
    masks = []
    for lvl in range(N_LEVELS):
        m = c >> (lvl + 1)
        p = (t // (2 * m)) * (2 * m) + m - 1
        mats.append(((t > p) & (s > p) & (s <= t)) | ((t <= p) & (s > t) & (s <= p)))
        masks.append((t // (2 * m) == s // (2 * m)) & (t % (2 * m) >= m) & (s % (2 * m) < m))
    masks.append(s == t)
    w = np.stack(mats).astype(np.float32)
    k = np.stack(masks).astype(np.float32)
    w = np.stack([w, w[:, ::-1, ::-1]]).reshape(2, (2 + N_LEVELS) * c, c)
    k = np.stack([k, k[:, ::-1, ::-1]])
    return jnp.asarray(w, BF16), jnp.asarray(k, F32)


def _hgrn_kernel(q_ref, z_ref, v_ref, lb_ref, w_ref, m_ref, o_ref, st_ref):
    d = pl.program_id(0)
    c = pl.program_id(1)

    @pl.when(c == 0)
    def _():
        st_ref[...] = jnp.zeros_like(st_ref)

    def head(h, carry):
        q = q_ref[h]
        z = z_ref[h]
        v = v_ref[h].astype(BF16)
        lb = lb_ref[0, h]
        e = jnp.exp(-jnp.abs(z))
        r = 1.0 / (1.0 + e)
        pos = z >= 0
        sig = jnp.where(pos, r, e * r)
        sig_neg = jnp.where(pos, e * r, r)
        logf = jnp.log(jnp.maximum(lb + (1.0 - lb) * sig, MIN_FORGET))
        kk = (1.0 - lb) * sig_neg
        hi = logf.astype(BF16)
        lo = (logf - hi.astype(F32)).astype(BF16)
        dall = jnp.dot(w_ref[0], jnp.concatenate([hi, lo], axis=1), preferred_element_type=F32)
        dec = dall[:, :LANES] + dall[:, LANES:]
        b = dec[0:CHUNK]
        total = jnp.where(d == 0, b[CHUNK - 1:CHUNK], b[0:1])
        st = st_ref[h]
        q_in = (q * jnp.exp(b)).astype(BF16)
        k_out = (kk * jnp.exp(dec[CHUNK:2 * CHUNK])).astype(BF16)
        o = lax.dot_general(q_in, st.astype(BF16), _NT, preferred_element_type=F32)
        a = lax.dot_general(q.astype(BF16), kk.astype(BF16), _NT,
                            preferred_element_type=F32) * m_ref[0, N_LEVELS]
        for lvl in range(N_LEVELS):
            el = jnp.exp(dec[(2 + lvl) * CHUNK:(3 + lvl) * CHUNK])
            a = a + lax.dot_general((q * el).astype(BF16), (kk * el).astype(BF16), _NT,
                                    preferred_element_type=F32) * m_ref[0, lvl]
        o = o + jnp.dot(a.astype(BF16), v, preferred_element_type=F32)
        o_ref[0, h] = o
        st_ref[h] = st * jnp.exp(total) + lax.dot_general(v, k_out, _TN, preferred_element_type=F32)
        return carry

    lax.fori_loop(0, HGRN_HEADS, head, 0)


def _hgrn(proj_h, lb):
    s = proj_h.shape[1]
    n = s // CHUNK
    wmat, masks = _hgrn_constants()
    hb = (HGRN_HEADS, CHUNK, LANES)

    def rows(d, c):
        return c + d * (n - 1 - 2 * c)

    return pl.pallas_call(
        _hgrn_kernel,
        grid=(2, n),
        in_specs=[pl.BlockSpec(hb, lambda d, c: (0, rows(d, c), 0)),
                  pl.BlockSpec(hb, lambda d, c: (1 + d, rows(d, c), 0)),
                  pl.BlockSpec(hb, lambda d, c: (3, rows(d, c), 0)),
                  pl.BlockSpec((1, HGRN_HEADS, 1, LANES), lambda d, c: (d, 0, 0, 0)),
                  pl.BlockSpec((1,) + wmat.shape[1:], lambda d, c: (d, 0, 0)),
                  pl.BlockSpec((1,) + masks.shape[1:], lambda d, c: (d, 0, 0, 0))],
        out_specs=pl.BlockSpec((1,) + hb, lambda d, c: (d, 0, rows(d, c), 0)),
        out_shape=jax.ShapeDtypeStruct((2, HGRN_HEADS, s, LANES), F32),
        scratch_shapes=[pltpu.VMEM((HGRN_HEADS, LANES, LANES), F32)],
        compiler_params=_params("arbitrary", "arbitrary"),
        name="hgrn2_scan",
    )(proj_h, proj_h, proj_h, lb, wmat, masks)


def _t5_bucket(rel):
    half = REL_BUCKETS // 2
    ret = (rel > 0).astype(np.int32) * half
    n = np.abs(rel)
    max_exact = half // 2
    large = max_exact + (np.log(np.maximum(n, 1) / max_exact)
                         / np.log(REL_MAX_DISTANCE / max_exact)
                         * (half - max_exact)).astype(np.int32)
    large = np.minimum(large, half - 1)
    return (ret + np.where(n < max_exact, n, large)).astype(np.int32)


def _attn_bias(rel_bias_table):
    rel = np.arange(2 * ATTN_QBLK)[None, :] - ATTN_SIDE - np.arange(ATTN_QBLK)[:, None]
    band = jnp.asarray(np.abs(rel) <= ATTN_SIDE)
    out = []
    for gi, (_, dil) in enumerate(ATTN_GROUPS):
        tab = rel_bias_table[:, gi * HEADS_PER_GROUP:(gi + 1) * HEADS_PER_GROUP].astype(F32)
        bias = jnp.transpose(tab[_t5_bucket(rel * dil)], (2, 0, 1))
        out.append(jnp.where(band[None], bias, NEG_INF))
    return jnp.stack(out)


def _attn_kernel(q_ref, kp_ref, km_ref, kn_ref, vp_ref, vm_ref, vn_ref, bias_ref,
                 o_ref, lse_ref, k_s, v_s, *, rows, length):
    i = pl.program_id(2)
    k_s[0:ATTN_SIDE] = kp_ref[...]
    k_s[ATTN_SIDE:ATTN_SIDE + rows] = km_ref[...]
    k_s[ATTN_SIDE + rows:] = kn_ref[...]
    v_s[0:ATTN_SIDE] = vp_ref[...]
    v_s[ATTN_SIDE:ATTN_SIDE + rows] = vm_ref[...]
    v_s[ATTN_SIDE + rows:] = vn_ref[...]
    bias = bias_ref[...]
    col = lax.broadcasted_iota(jnp.int32, (ATTN_QBLK, 2 * ATTN_QBLK), 1)
    scale = LANES ** -0.5
    for j in range(rows // ATTN_QBLK):
        r0 = j * ATTN_QBLK
        key_pos = col + (i * rows + r0 - ATTN_SIDE)
        valid = (key_pos >= 0) & (key_pos < length)
        s = lax.dot_general(q_ref[r0:r0 + ATTN_QBLK], k_s[r0:r0 + 2 * ATTN_QBLK], _NT,
                            preferred_element_type=F32) * scale + bias
        s = jnp.where(valid, s, NEG_INF)
        m = jnp.max(s, axis=-1, keepdims=True)
        p = jnp.exp(s - m)
        l = jnp.sum(p, axis=-1, keepdims=True)
        o = jnp.dot(p.astype(BF16), v_s[r0:r0 + 2 * ATTN_QBLK], preferred_element_type=F32)
        o_ref[r0:r0 + ATTN_QBLK] = (o / l).astype(o_ref.dtype)
        lse_ref[r0:r0 + ATTN_QBLK] = jnp.broadcast_to(m + jnp.log(l), (ATTN_QBLK, LANES))


def _attn_group(proj_a, bias, gi):
    s = proj_a.shape[1]
    dil = ATTN_GROUPS[gi][1]
    length = s // dil
    rows = min(512, length)
    nblk = length // rows
    sub = rows // ATTN_SIDE
    last = length // ATTN_SIDE - 1
    view = proj_a.reshape(3 * N_ATTN_HEADS, length, dil * LANES)
    q0, k0, v0 = gi * HEADS_PER_GROUP, N_ATTN_HEADS + gi * HEADS_PER_GROUP, 2 * N_ATTN_HEADS + gi * HEADS_PER_GROUP

    def main(base):
        return pl.BlockSpec((None, rows, LANES), lambda g, c, i: (base + g, i, c))

    def prev(base):
        return pl.BlockSpec((None, ATTN_SIDE, LANES),
                            lambda g, c, i: (base + g, jnp.maximum(i * sub - 1, 0), c))

    def nxt(base):
        return pl.BlockSpec((None, ATTN_SIDE, LANES),
                            lambda g, c, i: (base + g, jnp.minimum((i + 1) * sub, last), c))

    out_spec = pl.BlockSpec((None, rows, LANES), lambda g, c, i: (g, i, c))
    out_shape = jax.ShapeDtypeStruct((HEADS_PER_GROUP, length, dil * LANES), F32)
    o, lse = pl.pallas_call(
        functools.partial(_attn_kernel, rows=rows, length=length),
        grid=(HEADS_PER_GROUP, dil, nblk),
        in_specs=[main(q0), prev(k0), main(k0), nxt(k0), prev(v0), main(v0), nxt(v0),
                  pl.BlockSpec((None, None, ATTN_QBLK, 2 * ATTN_QBLK), lambda g, c, i: (gi, g, 0, 0))],
        out_specs=[out_spec, out_spec],
        out_shape=[out_shape, out_shape],
        scratch_shapes=[pltpu.VMEM((rows + 2 * ATTN_SIDE, LANES), BF16),
                        pltpu.VMEM((rows + 2 * ATTN_SIDE, LANES), BF16)],
        compiler_params=_params("parallel", "parallel", "arbitrary"),
        name=f"dilated_attn_g{gi}",
    )(view, view, view, view, view, view, view, bias)
    return o.reshape(HEADS_PER_GROUP, s, LANES), lse.reshape(HEADS_PER_GROUP, s, LANES)


def _merge_kernel(oh_ref, g_ref, a0_ref, a1_ref, a2_ref, l0_ref, l1_ref, l2_ref, gate_ref,
                  x_ref, pa_ref, pb_ref, wo_ref, hw_ref, nw_ref, xo_ref, ho_ref, hs, as_):
    for h in range(HGRN_HEADS):
        o = oh_ref[0, h] + oh_ref[1, h]
        o = o * lax.rsqrt(jnp.mean(o * o, axis=-1, keepdims=True) + NORM_EPS)
        g = g_ref[h]
        hs[:, h * LANES:(h + 1) * LANES] = (o * hw_ref[h] * (g * _sigmoid(g))).astype(BF16)
    for h in range(HEADS_PER_GROUP):
        l0, l1, l2 = l0_ref[h], l1_ref[h], l2_ref[h]
        m = jnp.maximum(jnp.maximum(l0, l1), l2)
        e0, e1, e2 = jnp.exp(l0 - m), jnp.exp(l1 - m), jnp.exp(l2 - m)
        o = (e0 * a0_ref[h] + e1 * a1_ref[h] + e2 * a2_ref[h]) / (e0 + e1 + e2)
        as_[:, h * LANES:(h + 1) * LANES] = o.astype(BF16)
    d = x_ref.shape[1]
    a = jnp.dot(hs[...], pa_ref[...], preferred_element_type=F32)
    b = jnp.dot(as_[...], pb_ref[...], preferred_element_type=F32)
    merged = (_sigmoid(gate_ref[:, :d].astype(F32)) * a
              + _sigmoid(gate_ref[:, d:].astype(F32)) * b)
    xn = x_ref[...] + jnp.dot(merged.astype(BF16), wo_ref[...], preferred_element_type=F32)
    xo_ref[...] = xn
    ho_ref[...] = _rms(xn, nw_ref[...]).astype(ho_ref.dtype)


def _merge(o_hgrn, proj_h, attn, gates, x, pa, pb, wo, hgrn_norm_w, norm_w, layer, tm=256):
    s, d = x.shape
    (a0, l0), (a1, l1), (a2, l2) = attn
    aspec = pl.BlockSpec((HEADS_PER_GROUP, tm, LANES), lambda i: (0, i, 0))

    def wspec(rows):
        return pl.BlockSpec((None, rows, d), lambda i: (layer, 0, 0))

    return pl.pallas_call(
        _merge_kernel,
        grid=(s // tm,),
        in_specs=[pl.BlockSpec((2, HGRN_HEADS, tm, LANES), lambda i: (0, 0, i, 0)),
                  pl.BlockSpec((HGRN_HEADS, tm, LANES), lambda i: (4, i, 0)),
                  aspec, aspec, aspec, aspec, aspec, aspec,
                  pl.BlockSpec((tm, 2 * d), lambda i: (i, 0)),
                  pl.BlockSpec((tm, d), lambda i: (i, 0)),
                  wspec(HGRN_WIDTH), wspec(ATTN_OUT_WIDTH), wspec(d),
                  pl.BlockSpec((HGRN_HEADS, 1, LANES), lambda i: (0, 0, 0)),
                  pl.BlockSpec((1, d), lambda i: (0, 0))],
        out_specs=[pl.BlockSpec((tm, d), lambda i: (i, 0)),
                   pl.BlockSpec((tm, d), lambda i: (i, 0))],
        out_shape=[jax.ShapeDtypeStruct((s, d), F32), jax.ShapeDtypeStruct((s, d), BF16)],
        scratch_shapes=[pltpu.VMEM((tm, HGRN_WIDTH), BF16), pltpu.VMEM((tm, ATTN_OUT_WIDTH), BF16)],
        compiler_params=_params("parallel"),
        name="merge_out_proj",
    )(o_hgrn, proj_h, a0, a1, a2, l0, l1, l2, gates, x, pa, pb, wo,
      hgrn_norm_w.reshape(HGRN_HEADS, 1, LANES), norm_w.reshape(1, d))


def _mlp_kernel(h_ref, x_ref, wu_ref, wd_ref, nw_ref, xo_ref, ho_ref, acc):
    f = pl.program_id(1)

    @pl.when(f == 0)
    def _():
        acc[...] = x_ref[...]

    u = jnp.maximum(jnp.dot(h_ref[...], wu_ref[...], preferred_element_type=F32), 0.0)
    acc[...] += jnp.dot((u * u).astype(BF16), wd_ref[...], preferred_element_type=F32)

    @pl.when(f == pl.num_programs(1) - 1)
    def _():
        xn = acc[...]
        xo_ref[...] = xn
        ho_ref[...] = _rms(xn, nw_ref[...]).astype(ho_ref.dtype)


def _mlp(h, x, wu, wd, norm_w, layer, h_dtype, tm=512, tf=1024):
    s, d = x.shape
    ff = wu.shape[2]
    return pl.pallas_call(
        _mlp_kernel,
        grid=(s // tm, ff // tf),
        in_specs=[pl.BlockSpec((tm, d), lambda i, f: (i, 0)),
                  pl.BlockSpec((tm, d), lambda i, f: (i, 0)),
                  pl.BlockSpec((None, d, tf), lambda i, f: (layer, 0, f)),
                  pl.BlockSpec((None, tf, d), lambda i, f: (layer, f, 0)),
                  pl.BlockSpec((1, d), lambda i, f: (0, 0))],
        out_specs=[pl.BlockSpec((tm, d), lambda i, f: (i, 0)),
                   pl.BlockSpec((tm, d), lambda i, f: (i, 0))],
        out_shape=[jax.ShapeDtypeStruct((s, d), F32), jax.ShapeDtypeStruct((s, d), h_dtype)],
        scratch_shapes=[pltpu.VMEM((tm, d), F32)],
        compiler_params=_params("parallel", "arbitrary"),
        name="mlp_relu2",
    )(h, x, wu, wd, norm_w.reshape(1, d))


def _lower_bounds(logits):
    p = jax.nn.softmax(logits.astype(F32), axis=0)
    return jnp.cumsum(p, axis=0) - p[0:1]


def kernel(x, w_in, hgrn_lb_fwd, hgrn_lb_bwd, hgrn_norm_w, rel_bias_table, w_branch_hgrn,
           w_branch_attn, w_out, norm_mix_w, norm_mlp_w, w_up, w_down, final_norm_w):
    batch, s, d = x.shape
    depth = w_in.shape[0]
    assert batch == 1
    x = x.reshape(s, d)
    lb = jnp.stack([_lower_bounds(hgrn_lb_fwd), _lower_bounds(hgrn_lb_bwd)], axis=1)
    lb = lb.reshape(-1, 2, HGRN_HEADS, 1, LANES)
    bias = _attn_bias(rel_bias_table)
    w_in_b, pa_b, pb_b = w_in.astype(BF16), w_branch_hgrn.astype(BF16), w_branch_attn.astype(BF16)
    wo_b, wu_b, wd_b = w_out.astype(BF16), w_up.astype(BF16), w_down.astype(BF16)
    n_h = 5 * HGRN_WIDTH
    n_a = 3 * ATTN_WIDTH
    h = _norm(x, norm_mix_w[0])
    for l in range(depth):
        proj_h = _proj(h, w_in_b, l, 0, n_h, F32, True, "proj_hgrn")
        proj_a = _proj(h, w_in_b, l, n_h, n_a, BF16, True, "proj_attn")
        gates = _proj(h, w_in_b, l, n_h + n_a, 2 * d, BF16, False, "proj_gates")
        o_hgrn = _hgrn(proj_h, lb[l])
        attn = [_attn_group(proj_a, bias, gi) for gi in range(len(ATTN_GROUPS))]
        x, h2 = _merge(o_hgrn, proj_h, attn, gates, x, pa_b, pb_b, wo_b, hgrn_norm_w[l],
                       norm_mlp_w[l], l)
        last = l == depth - 1
        x, h = _mlp(h2, x, wu_b, wd_b, final_norm_w if last else norm_mix_w[l + 1], l,
                    F32 if last else BF16)
    return h.reshape(batch, s, d)
```

```python
import functools

import jax
import jax.numpy as jnp
import numpy as np
from jax import lax
from jax.experimental import pallas as pl
from jax.experimental.pallas import tpu as pltpu

F32 = jnp.float32
BF16 = jnp.bfloat16

LANES = 128
HGRN_HEADS = 8
HGRN_WIDTH = HGRN_HEADS * LANES
CHUNK = 128
N_LEVELS = 7
MXU_LEVEL_BELOW = 8
N_MXU_LEVELS = 3
MIN_FORGET = 1e-30
ATTN_GROUPS = ((128, 1), (512, 4), (2048, 16))
HEADS_PER_GROUP = 4
N_ATTN_HEADS = HEADS_PER_GROUP * len(ATTN_GROUPS)
ATTN_WIDTH = N_ATTN_HEADS * LANES
ATTN_OUT_WIDTH = HEADS_PER_GROUP * LANES
ATTN_QBLK = 128
ATTN_SIDE = 64
REL_BUCKETS = 32
REL_MAX_DISTANCE = 1024
NORM_EPS = 1e-6
NEG_INF = -1e30
VMEM_LIMIT = 56 * 1024 * 1024

_NT = (((1,), (1,)), ((), ()))
_TN = (((0,), (0,)), ((), ()))


def _params(*sem):
    return pltpu.CompilerParams(dimension_semantics=sem, vmem_limit_bytes=VMEM_LIMIT)


def _rms(x, w):
    return x * lax.rsqrt(jnp.mean(x * x, axis=-1, keepdims=True) + NORM_EPS) * w


def _sigmoid(x):
    e = jnp.exp(-jnp.abs(x))
    r = 1.0 / (1.0 + e)
    return jnp.where(x >= 0, r, e * r)


def _norm_kernel(x_ref, w_ref, o_ref):
    o_ref[...] = _rms(x_ref[...], w_ref[...]).astype(o_ref.dtype)


def _norm(x, w, tm=512):
    s, d = x.shape
    return pl.pallas_call(
        _norm_kernel,
        grid=(s // tm,),
        in_specs=[pl.BlockSpec((tm, d), lambda i: (i, 0)),
                  pl.BlockSpec((1, d), lambda i: (0, 0))],
        out_specs=pl.BlockSpec((tm, d), lambda i: (i, 0)),
        out_shape=jax.ShapeDtypeStruct((s, d), BF16),
        compiler_params=_params("parallel"),
        name="rmsnorm_in",
    )(x, w.reshape(1, d))


def _proj_kernel(x_ref, w_ref, o_ref, *, slabs):
    acc = jnp.dot(x_ref[...], w_ref[...], preferred_element_type=F32)
    if slabs:
        width = o_ref.shape[-1]
        for j in range(slabs):
            o_ref[j] = acc[:, j * width:(j + 1) * width].astype(o_ref.dtype)
    else:
        o_ref[...] = acc.astype(o_ref.dtype)


def _proj(h, w, layer, col0, n, out_dtype, slab_width, name, tm=1024, tn=512):
    s, d = h.shape
    off = col0 // tn
    if slab_width:
        slabs = tn // slab_width
        out_shape = jax.ShapeDtypeStruct((n // slab_width, s, slab_width), out_dtype)
        out_spec = pl.BlockSpec((slabs, tm, slab_width), lambda i, j: (j, i, 0))
    else:
        slabs = 0
        out_shape = jax.ShapeDtypeStruct((s, n), out_dtype)
        out_spec = pl.BlockSpec((tm, tn), lambda i, j: (i, j))
    return pl.pallas_call(
        functools.partial(_proj_kernel, slabs=slabs),
        grid=(s // tm, n // tn),
        in_specs=[pl.BlockSpec((tm, d), lambda i, j: (i, 0)),
                  pl.BlockSpec((None, d, tn), lambda i, j: (layer, 0, j + off))],
        out_specs=out_spec,
        out_shape=out_shape,
        compiler_params=_params("parallel", "arbitrary"),
        name=name,
    )(h, w)


def _proj_attn_kernel(x_ref, w_ref, o_ref, scr, *, dil):
    acc = jnp.dot(x_ref[...], w_ref[...], preferred_element_type=F32)
    if dil == 1:
        for hh in range(HEADS_PER_GROUP):
            o_ref[hh, 0] = acc[:, hh * LANES:(hh + 1) * LANES].astype(o_ref.dtype)
    else:
        rows = scr.shape[1] // dil
        for hh in range(HEADS_PER_GROUP):
            scr[hh] = acc[:, hh * LANES:(hh + 1) * LANES]
        for hh in range(HEADS_PER_GROUP):
            for c in range(dil):
                o_ref[hh, c] = scr[hh, pl.ds(c, rows, stride=dil), :].astype(o_ref.dtype)


def _proj_attn(h, w, layer, col0, gi, tm=1024):
    s, d = h.shape
    dil = ATTN_GROUPS[gi][1]
    tn = ATTN_OUT_WIDTH
    off = col0 // tn + gi
    n_groups = len(ATTN_GROUPS)
    return pl.pallas_call(
        functools.partial(_proj_attn_kernel, dil=dil),
        grid=(s // tm, 3),
        in_specs=[pl.BlockSpec((tm, d), lambda i, j: (i, 0)),
                  pl.BlockSpec((None, d, tn), lambda i, j: (layer, 0, off + n_groups * j))],
        out_specs=pl.BlockSpec((HEADS_PER_GROUP, dil, tm // dil, LANES), lambda i, j: (j, 0, i, 0)),
        out_shape=jax.ShapeDtypeStruct((3 * HEADS_PER_GROUP, dil, s // dil, LANES), BF16),
        scratch_shapes=[pltpu.VMEM((HEADS_PER_GROUP, tm, LANES), F32)],
        compiler_params=_params("parallel", "arbitrary"),
        name=f"proj_attn_g{gi}",
    )(h, w)


def _hgrn_constants():
    c = CHUNK
    t = np.arange(c)[:, None]
    s = np.arange(c)[None, :]
    mats = [s <= t]
    masks = []
    for lvl in range(N_LEVELS):
        m = c >> (lvl + 1)
        p = (t // (2 * m)) * (2 * m) + m - 1
        if m < MXU_LEVEL_BELOW:
            mats.append(((t > p) & (s > p) & (s <= t)) | ((t <= p) & (s > t) & (s <= p)))
        masks.append((t // (2 * m) == s // (2 * m)) & (t % (2 * m) >= m) & (s % (2 * m) < m))
    w = np.stack(mats).astype(np.float32)
    k = np.stack(masks).astype(np.float32)
    w = np.stack([w, w[:, ::-1, ::-1]]).reshape(2, len(mats) * c, c)
    w = np.concatenate([w, w], axis=2)
    k = np.stack([k, k[:, ::-1, ::-1]])
    k = np.concatenate([k, k], axis=3)
    return jnp.asarray(w, BF16), jnp.asarray(k, BF16)


def _block_diag(x):
    zero = jnp.zeros((x.shape[0], LANES), x.dtype)
    return jnp.concatenate([jnp.concatenate([x[:, :LANES], zero], axis=1),
                            jnp.concatenate([zero, x[:, LANES:]], axis=1)], axis=0)


def _hgrn_chunk(q, z, v, lb, wmat, m_ref, st, backward):
    e = jnp.exp(-jnp.abs(z))
    r = 1.0 / (1.0 + e)
    pos = z >= 0
    sig = jnp.where(pos, r, e * r)
    sig_neg = jnp.where(pos, e * r, r)
    logf = jnp.log2(jnp.maximum(lb + (1.0 - lb) * sig, MIN_FORGET))
    kk = (1.0 - lb) * sig_neg
    hi = logf.astype(BF16)
    lo = (logf - hi.astype(F32)).astype(BF16)
    dec = jnp.dot(wmat, jnp.concatenate([hi, lo], axis=0), preferred_element_type=F32)
    b = dec[0:CHUNK]
    total = b[0:1] if backward else b[CHUNK - 1:CHUNK]
    q_in = (q * jnp.exp2(b)).astype(BF16)
    k_out = (kk * jnp.exp2(total - b)).astype(BF16)
    vb = v.astype(BF16)
    qb = q.astype(BF16)
    kb = kk.astype(BF16)
    o = lax.dot_general(q_in, _block_diag(st.astype(BF16)), _NT, preferred_element_type=F32)
    qk = q * kk
    diag = jnp.concatenate(
        [jnp.broadcast_to(jnp.sum(qk[:, :LANES], axis=-1, keepdims=True), (CHUNK, LANES)),
         jnp.broadcast_to(jnp.sum(qk[:, LANES:], axis=-1, keepdims=True), (CHUNK, LANES))], axis=1)
    o = o + diag * v
    a = None
    for lvl in range(N_LEVELS):
        m = CHUNK >> (lvl + 1)
        if m >= MXU_LEVEL_BELOW:
            parts = []
            for j in range(CHUNK // (2 * m)):
                lo_rows = b[2 * m * j:2 * m * j + m]
                hi_rows = b[2 * m * j + m:2 * m * (j + 1)]
                if backward:
                    piv = b[2 * m * j + m:2 * m * j + m + 1]
                    parts += [lo_rows - piv, piv - hi_rows]
                else:
                    piv = b[2 * m * j + m - 1:2 * m * j + m]
                    parts += [piv - lo_rows, hi_rows - piv]
            nl = jnp.concatenate(parts, axis=0)
        else:
            i = 1 + lvl - (N_LEVELS - N_MXU_LEVELS)
            nl = dec[i * CHUNK:(i + 1) * CHUNK]
        el = jnp.exp2(nl).astype(BF16)
        sc = lax.dot_general(qb * el, _block_diag(kb * el), _NT, preferred_element_type=F32)
        sc = sc.astype(BF16) * m_ref[lvl]
        a = sc if a is None else a + sc
    o = o + jnp.dot(a, _block_diag(vb), preferred_element_type=F32)
    v_rows = jnp.concatenate([vb[:, :LANES], vb[:, LANES:]], axis=0)
    st_new = st * jnp.exp2(total) + lax.dot_general(v_rows, _block_diag(k_out), _TN,
                                                    preferred_element_type=F32)
    return o, st_new


def _hgrn_kernel(qf_ref, zf_ref, vf_ref, qb_ref, zb_ref, vb_ref, lb_ref, w_ref, m_ref,
                 of_ref, ob_ref, stf_ref, stb_ref):
    @pl.when(pl.program_id(0) == 0)
    def _():
        stf_ref[...] = jnp.zeros_like(stf_ref)
        stb_ref[...] = jnp.zeros_like(stb_ref)

    for p in range(HGRN_HEADS // 2):
        o, st = _hgrn_chunk(qf_ref[p], zf_ref[p], vf_ref[p], lb_ref[0, p], w_ref[0], m_ref.at[0],
                            stf_ref[p], False)
        of_ref[p] = o
        stf_ref[p] = st
        o, st = _hgrn_chunk(qb_ref[p], zb_ref[p], vb_ref[p], lb_ref[1, p], w_ref[1], m_ref.at[1],
                            stb_ref[p], True)
        ob_ref[p] = o
        stb_ref[p] = st


def _hgrn(proj_h, lb):
    s = proj_h.shape[1]
    n = s // CHUNK
    pairs = HGRN_HEADS // 2
    wmat, masks = _hgrn_constants()
    hb = (pairs, CHUNK, 2 * LANES)
    fwd = lambda slab: pl.BlockSpec(hb, lambda c: (slab, c, 0))
    bwd = lambda slab: pl.BlockSpec(hb, lambda c: (slab, n - 1 - c, 0))
    const = lambda a: pl.BlockSpec(a.shape, lambda c: (0,) * a.ndim)
    out_shape = jax.ShapeDtypeStruct((pairs, s, 2 * LANES), F32)
    state = pltpu.VMEM((pairs, LANES, 2 * LANES), F32)
    return pl.pallas_call(
        _hgrn_kernel,
        grid=(n,),
        in_specs=[fwd(0), fwd(1), fwd(3), bwd(0), bwd(2), bwd(3), const(lb), const(wmat), const(masks)],
        out_specs=[fwd(0), bwd(0)],
        out_shape=[out_shape, out_shape],
        scratch_shapes=[state, state],
        compiler_params=_params("arbitrary"),
        name="hgrn2_scan",
    )(proj_h, proj_h, proj_h, proj_h, proj_h, proj_h, lb, wmat, masks)


def _t5_bucket(rel):
    half = REL_BUCKETS // 2
    ret = (rel > 0).astype(np.int32) * half
    n = np.abs(rel)
    max_exact = half // 2
    large = max_exact + (np.log(np.maximum(n, 1) / max_exact)
                         / np.log(REL_MAX_DISTANCE / max_exact)
                         * (half - max_exact)).astype(np.int32)
    large = np.minimum(large, half - 1)
    return (ret + np.where(n < max_exact, n, large)).astype(np.int32)


def _attn_bias(rel_bias_table):
    span = 3 * ATTN_QBLK - 1
    rel = np.arange(span) - (ATTN_QBLK - 1) - ATTN_SIDE
    band = jnp.asarray(np.abs(np.arange(2 * ATTN_QBLK)[None, :] - ATTN_SIDE
                              - np.arange(ATTN_QBLK)[:, None]) <= ATTN_SIDE)
    out = []
    for gi, (_, dil) in enumerate(ATTN_GROUPS):
        tab = rel_bias_table[:, gi * HEADS_PER_GROUP:(gi + 1) * HEADS_PER_GROUP].astype(F32)
        onehot = jnp.asarray(_t5_bucket(rel * dil)[:, None] == np.arange(REL_BUCKETS)[None, :], F32)
        vec = jnp.einsum("rb,bh->hr", onehot, tab, precision=lax.Precision.HIGHEST)
        period = span + 2
        vec = jnp.pad(vec, ((0, 0), (0, period - span)))
        flat = jnp.tile(vec, (1, ATTN_QBLK))[:, :ATTN_QBLK * (span + 1)]
        toep = flat.reshape(HEADS_PER_GROUP, ATTN_QBLK, span + 1)[:, :, ATTN_QBLK - 1:3 * ATTN_QBLK - 1]
        out.append(jnp.where(band[None], toep, NEG_INF))
    return jnp.stack(out)


def _attn_kernel(q_ref, kp_ref, km_ref, kn_ref, vp_ref, vm_ref, vn_ref, bias_ref,
                 o_ref, lse_ref, k_s, v_s, *, rows, length, dil):
    c = pl.program_id(2)
    i = pl.program_id(1)
    k_s[0:ATTN_SIDE] = kp_ref[...]
    k_s[ATTN_SIDE:ATTN_SIDE + rows] = km_ref[...]
    k_s[ATTN_SIDE + rows:] = kn_ref[...]
    v_s[0:ATTN_SIDE] = vp_ref[...]
    v_s[ATTN_SIDE:ATTN_SIDE + rows] = vm_ref[...]
    v_s[ATTN_SIDE + rows:] = vn_ref[...]
    bias = bias_ref[...]
    col = lax.broadcasted_iota(jnp.int32, (ATTN_QBLK, 2 * ATTN_QBLK), 1)
    scale = LANES ** -0.5
    for j in range(rows // ATTN_QBLK):
        r0 = j * ATTN_QBLK
        key_pos = col + (i * rows + r0 - ATTN_SIDE)
        valid = (key_pos >= 0) & (key_pos < length)
        s = lax.dot_general(q_ref[r0:r0 + ATTN_QBLK], k_s[r0:r0 + 2 * ATTN_QBLK], _NT,
                            preferred_element_type=F32) * scale + bias
        s = jnp.where(valid, s, NEG_INF)
        m = jnp.max(s, axis=-1, keepdims=True)
        p = jnp.exp(s - m)
        l = jnp.sum(p, axis=-1, keepdims=True)
        o = jnp.dot(p.astype(BF16), v_s[r0:r0 + 2 * ATTN_QBLK], preferred_element_type=F32) / l
        lse = jnp.broadcast_to(m + jnp.log(l), (ATTN_QBLK, LANES))
        if dil == 1:
            o_ref[r0:r0 + ATTN_QBLK] = o
            lse_ref[r0:r0 + ATTN_QBLK] = lse
        else:
            o_ref[pl.ds(r0 * dil + c, ATTN_QBLK, stride=dil), :] = o
            lse_ref[pl.ds(r0 * dil + c, ATTN_QBLK, stride=dil), :] = lse


def _attn_group(qkv, bias, gi):
    dil, length = qkv.shape[1], qkv.shape[2]
    s = dil * length
    rows = min(512, length)
    nblk = length // rows
    sub = rows // ATTN_SIDE
    last = length // ATTN_SIDE - 1
    hp = HEADS_PER_GROUP

    def main(kind):
        return pl.BlockSpec((None, None, rows, LANES), lambda g, i, c: (kind * hp + g, c, i, 0))

    def prev(kind):
        return pl.BlockSpec((None, None, ATTN_SIDE, LANES),
                            lambda g, i, c: (kind * hp + g, c, jnp.maximum(i * sub - 1, 0), 0))

    def nxt(kind):
        return pl.BlockSpec((None, None, ATTN_SIDE, LANES),
                            lambda g, i, c: (kind * hp + g, c, jnp.minimum((i + 1) * sub, last), 0))

    out_spec = pl.BlockSpec((None, rows * dil, LANES), lambda g, i, c: (g, i, 0))
    out_shape = jax.ShapeDtypeStruct((hp, s, LANES), F32)
    return pl.pallas_call(
        functools.partial(_attn_kernel, rows=rows, length=length, dil=dil),
        grid=(hp, nblk, dil),
        in_specs=[main(0), prev(1), main(1), nxt(1), prev(2), main(2), nxt(2),
                  pl.BlockSpec((None, None, ATTN_QBLK, 2 * ATTN_QBLK), lambda g, i, c: (gi, g, 0, 0))],
        out_specs=[out_spec, out_spec],
        out_shape=[out_shape, out_shape],
        scratch_shapes=[pltpu.VMEM((rows + 2 * ATTN_SIDE, LANES), BF16),
                        pltpu.VMEM((rows + 2 * ATTN_SIDE, LANES), BF16)],
        compiler_params=_params("parallel", "arbitrary", "arbitrary"),
        name=f"dilated_attn_g{gi}",
    )(qkv, qkv, qkv, qkv, qkv, qkv, qkv, bias)


def _merge_kernel(of_ref, ob_ref, g_ref, a0_ref, a1_ref, a2_ref, l0_ref, l1_ref, l2_ref, gate_ref,
                  x_ref, pa_ref, pb_ref, wo_ref, hw_ref, nw_ref, xo_ref, ho_ref, hs, as_):
    for h in range(HGRN_HEADS):
        lanes = slice((h % 2) * LANES, (h % 2 + 1) * LANES)
        o = of_ref[h // 2, :, lanes] + ob_ref[h // 2, :, lanes]
        o = o * lax.rsqrt(jnp.mean(o * o, axis=-1, keepdims=True) + NORM_EPS)
        g = g_ref[h // 2, :, lanes]
        hs[:, h * LANES:(h + 1) * LANES] = (o * hw_ref[h] * (g * _sigmoid(g))).astype(BF16)
    for h in range(HEADS_PER_GROUP):
        l0, l1, l2 = l0_ref[h], l1_ref[h], l2_ref[h]
        m = jnp.maximum(jnp.maximum(l0, l1), l2)
        e0, e1, e2 = jnp.exp(l0 - m), jnp.exp(l1 - m), jnp.exp(l2 - m)
        o = (e0 * a0_ref[h] + e1 * a1_ref[h] + e2 * a2_ref[h]) / (e0 + e1 + e2)
        as_[:, h * LANES:(h + 1) * LANES] = o.astype(BF16)
    d = x_ref.shape[1]
    a = jnp.dot(hs[...], pa_ref[...], preferred_element_type=F32)
    b = jnp.dot(as_[...], pb_ref[...], preferred_element_type=F32)
    merged = (_sigmoid(gate_ref[:, :d].astype(F32)) * a
              + _sigmoid(gate_ref[:, d:].astype(F32)) * b)
    xn = x_ref[...] + jnp.dot(merged.astype(BF16), wo_ref[...], preferred_element_type=F32)
    xo_ref[...] = xn
    ho_ref[...] = _rms(xn, nw_ref[...]).astype(ho_ref.dtype)


def _merge(o_hgrn, proj_h, attn, gates, x, pa, pb, wo, hgrn_norm_w, norm_w, layer, tm=256):
    s, d = x.shape
    (a0, l0), (a1, l1), (a2, l2) = attn
    aspec = pl.BlockSpec((HEADS_PER_GROUP, tm, LANES), lambda i: (0, i, 0))
    pairs = HGRN_HEADS // 2
    hspec = pl.BlockSpec((pairs, tm, 2 * LANES), lambda i: (0, i, 0))

    def wspec(rows):
        return pl.BlockSpec((None, rows, d), lambda i: (layer, 0, 0))

    return pl.pallas_call(
        _merge_kernel,
        grid=(s // tm,),
        in_specs=[hspec, hspec,
                  pl.BlockSpec((pairs, tm, 2 * LANES), lambda i: (4, i, 0)),
                  aspec, aspec, aspec, aspec, aspec, aspec,
                  pl.BlockSpec((tm, 2 * d), lambda i: (i, 0)),
                  pl.BlockSpec((tm, d), lambda i: (i, 0)),
                  wspec(HGRN_WIDTH), wspec(ATTN_OUT_WIDTH), wspec(d),
                  pl.BlockSpec((HGRN_HEADS, 1, LANES), lambda i: (0, 0, 0)),
                  pl.BlockSpec((1, d), lambda i: (0, 0))],
        out_specs=[pl.BlockSpec((tm, d), lambda i: (i, 0)),
                   pl.BlockSpec((tm, d), lambda i: (i, 0))],
        out_shape=[jax.ShapeDtypeStruct((s, d), F32), jax.ShapeDtypeStruct((s, d), BF16)],
        scratch_shapes=[pltpu.VMEM((tm, HGRN_WIDTH), BF16), pltpu.VMEM((tm, ATTN_OUT_WIDTH), BF16)],
        compiler_params=_params("parallel"),
        name="merge_out_proj",
    )(o_hgrn[0], o_hgrn[1], proj_h, a0, a1, a2, l0, l1, l2, gates, x, pa, pb, wo,
      hgrn_norm_w.reshape(HGRN_HEADS, 1, LANES), norm_w.reshape(1, d))


def _mlp_kernel(h_ref, x_ref, wu_ref, wd_ref, nw_ref, xo_ref, ho_ref, acc):
    f = pl.program_id(1)

    @pl.when(f == 0)
    def _():
        acc[...] = x_ref[...]

    u = jnp.maximum(jnp.dot(h_ref[...], wu_ref[...], preferred_element_type=F32), 0.0)
    acc[...] += jnp.dot((u * u).astype(BF16), wd_ref[...], preferred_element_type=F32)

    @pl.when(f == pl.num_programs(1) - 1)
    def _():
        xn = acc[...]
        xo_ref[...] = xn
        ho_ref[...] = _rms(xn, nw_ref[...]).astype(ho_ref.dtype)


def _mlp(h, x, wu, wd, norm_w, layer, h_dtype, tm=512, tf=1024):
    s, d = x.shape
    ff = wu.shape[2]
    return pl.pallas_call(
        _mlp_kernel,
        grid=(s // tm, ff // tf),
        in_specs=[pl.BlockSpec((tm, d), lambda i, f: (i, 0)),
                  pl.BlockSpec((tm, d), lambda i, f: (i, 0)),
                  pl.BlockSpec((None, d, tf), lambda i, f: (layer, 0, f)),
                  pl.BlockSpec((None, tf, d), lambda i, f: (layer, f, 0)),
                  pl.BlockSpec((1, d), lambda i, f: (0, 0))],
        out_specs=[pl.BlockSpec((tm, d), lambda i, f: (i, 0)),
                   pl.BlockSpec((tm, d), lambda i, f: (i, 0))],
        out_shape=[jax.ShapeDtypeStruct((s, d), F32), jax.ShapeDtypeStruct((s, d), h_dtype)],
        scratch_shapes=[pltpu.VMEM((tm, d), F32)],
        compiler_params=_params("parallel", "arbitrary"),
        name="mlp_relu2",
    )(h, x, wu, wd, norm_w.reshape(1, d))


def _lower_bounds(logits):
    p = jax.nn.softmax(logits.astype(F32), axis=0)
    return jnp.cumsum(p, axis=0) - p[0:1]


def kernel(x, w_in, hgrn_lb_fwd, hgrn_lb_bwd, hgrn_norm_w, rel_bias_table, w_branch_hgrn,
           w_branch_attn, w_out, norm_mix_w, norm_mlp_w, w_up, w_down, final_norm_w):
    batch, s, d = x.shape
    depth = w_in.shape[0]
    assert batch == 1
    x = x.reshape(s, d)
    lb = jnp.stack([_lower_bounds(hgrn_lb_fwd), _lower_bounds(hgrn_lb_bwd)], axis=1)
    lb = lb.reshape(-1, 2, HGRN_HEADS // 2, 1, 2 * LANES)
    bias = _attn_bias(rel_bias_table)
    w_in_b, pa_b, pb_b = w_in.astype(BF16), w_branch_hgrn.astype(BF16), w_branch_attn.astype(BF16)
    wo_b, wu_b, wd_b = w_out.astype(BF16), w_up.astype(BF16), w_down.astype(BF16)
    n_h = 5 * HGRN_WIDTH
    n_a = 3 * ATTN_WIDTH
    h = _norm(x, norm_mix_w[0])
    for l in range(depth):
        proj_h = _proj(h, w_in_b, l, 0, n_h, F32, 2 * LANES, "proj_hgrn")
        gates = _proj(h, w_in_b, l, n_h + n_a, 2 * d, BF16, 0, "proj_gates")
        o_hgrn = _hgrn(proj_h, lb[l])
        attn = [_attn_group(_proj_attn(h, w_in_b, l, n_h, gi), bias, gi)
                for gi in range(len(ATTN_GROUPS))]
        x, h2 = _merge(o_hgrn, proj_h, attn, gates, x, pa_b, pb_b, wo_b, hgrn_norm_w[l],
                       norm_mlp_w[l], l)
        last = l == depth - 1
        x, h = _mlp(h2, x, wu_b, wd_b, final_norm_w if last else norm_mix_w[l + 1], l,
                    F32 if last else BF16)
    return h.reshape(batch, s, d)
```

```python
import functools

import jax
import jax.numpy as jnp
import numpy as np
from jax import lax
from jax.experimental import pallas as pl
from jax.experimental.pallas import tpu as pltpu

F32 = jnp.float32
BF16 = jnp.bfloat16

LANES = 128
HGRN_HEADS = 8
HGRN_WIDTH = HGRN_HEADS * LANES
CHUNK = 128
N_LEVELS = 7
MXU_LEVEL_BELOW = 8
N_MXU_LEVELS = 3
MIN_FORGET = 1e-30
ATTN_GROUPS = ((128, 1), (512, 4), (2048, 16))
HEADS_PER_GROUP = 4
N_ATTN_HEADS = HEADS_PER_GROUP * len(ATTN_GROUPS)
ATTN_WIDTH = N_ATTN_HEADS * LANES
ATTN_OUT_WIDTH = HEADS_PER_GROUP * LANES
ATTN_QBLK = 128
ATTN_SIDE = 64
REL_BUCKETS = 32
REL_MAX_DISTANCE = 1024
NORM_EPS = 1e-6
NEG_INF = -1e30
VMEM_LIMIT = 56 * 1024 * 1024

_NT = (((1,), (1,)), ((), ()))
_TN = (((0,), (0,)), ((), ()))


def _params(*sem):
    return pltpu.CompilerParams(dimension_semantics=sem, vmem_limit_bytes=VMEM_LIMIT)


def _rms(x, w):
    return x * lax.rsqrt(jnp.mean(x * x, axis=-1, keepdims=True) + NORM_EPS) * w


def _sigmoid(x):
    return 0.5 * jnp.tanh(0.5 * x) + 0.5


def _norm_kernel(x_ref, w_ref, o_ref):
    o_ref[...] = _rms(x_ref[...], w_ref[...]).astype(o_ref.dtype)


def _norm(x, w, tm=512):
    s, d = x.shape
    return pl.pallas_call(
        _norm_kernel,
        grid=(s // tm,),
        in_specs=[pl.BlockSpec((tm, d), lambda i: (i, 0)),
                  pl.BlockSpec((1, d), lambda i: (0, 0))],
        out_specs=pl.BlockSpec((tm, d), lambda i: (i, 0)),
        out_shape=jax.ShapeDtypeStruct((s, d), BF16),
        compiler_params=_params("parallel"),
        name="rmsnorm_in",
    )(x, w.reshape(1, d))


def _proj_kernel(x_ref, w_ref, o_ref, *, slabs):
    acc = jnp.dot(x_ref[...], w_ref[...].astype(BF16), preferred_element_type=F32)
    if slabs:
        width = o_ref.shape[-1]
        for j in range(slabs):
            o_ref[j] = acc[:, j * width:(j + 1) * width].astype(o_ref.dtype)
    else:
        o_ref[...] = acc.astype(o_ref.dtype)


def _proj(h, w, layer, col0, n, out_dtype, slab_width, name, tm=2048, tn=512):
    s, d = h.shape
    off = col0 // tn
    if slab_width:
        slabs = tn // slab_width
        out_shape = jax.ShapeDtypeStruct((n // slab_width, s, slab_width), out_dtype)
        out_spec = pl.BlockSpec((slabs, tm, slab_width), lambda i, j: (j, i, 0))
    else:
        slabs = 0
        out_shape = jax.ShapeDtypeStruct((s, n), out_dtype)
        out_spec = pl.BlockSpec((tm, tn), lambda i, j: (i, j))
    return pl.pallas_call(
        functools.partial(_proj_kernel, slabs=slabs),
        grid=(s // tm, n // tn),
        in_specs=[pl.BlockSpec((tm, d), lambda i, j: (i, 0)),
                  pl.BlockSpec((None, d, tn), lambda i, j: (layer, 0, j + off))],
        out_specs=out_spec,
        out_shape=out_shape,
        compiler_params=_params("parallel", "arbitrary"),
        name=name,
    )(h, w)


def _proj_attn_kernel(x_ref, w_ref, o_ref, scr, *, dil):
    acc = jnp.dot(x_ref[...], w_ref[...].astype(BF16), preferred_element_type=F32)
    if dil == 1:
        for hh in range(HEADS_PER_GROUP):
            o_ref[hh, 0] = acc[:, hh * LANES:(hh + 1) * LANES].astype(o_ref.dtype)
    else:
        rows = scr.shape[1] // dil
        for hh in range(HEADS_PER_GROUP):
            scr[hh] = acc[:, hh * LANES:(hh + 1) * LANES]
        for hh in range(HEADS_PER_GROUP):
            for c in range(dil):
                o_ref[hh, c] = scr[hh, pl.ds(c, rows, stride=dil), :].astype(o_ref.dtype)


def _proj_attn(h, w, layer, col0, gi, tm=2048):
    s, d = h.shape
    dil = ATTN_GROUPS[gi][1]
    tn = ATTN_OUT_WIDTH
    off = col0 // tn + gi
    n_groups = len(ATTN_GROUPS)
    return pl.pallas_call(
        functools.partial(_proj_attn_kernel, dil=dil),
        grid=(s // tm, 3),
        in_specs=[pl.BlockSpec((tm, d), lambda i, j: (i, 0)),
                  pl.BlockSpec((None, d, tn), lambda i, j: (layer, 0, off + n_groups * j))],
        out_specs=pl.BlockSpec((HEADS_PER_GROUP, dil, tm // dil, LANES), lambda i, j: (j, 0, i, 0)),
        out_shape=jax.ShapeDtypeStruct((3 * HEADS_PER_GROUP, dil, s // dil, LANES), BF16),
        scratch_shapes=[pltpu.VMEM((HEADS_PER_GROUP, tm, LANES), F32)],
        compiler_params=_params("parallel", "arbitrary"),
        name=f"proj_attn_g{gi}",
    )(h, w)


def _hgrn_constants():
    c = CHUNK
    t = np.arange(c)[:, None]
    s = np.arange(c)[None, :]
    mats = [s <= t]
    masks = []
    for lvl in range(N_LEVELS):
        m = c >> (lvl + 1)
        p = (t // (2 * m)) * (2 * m) + m - 1
        if m < MXU_LEVEL_BELOW:
            mats.append(((t > p) & (s > p) & (s <= t)) | ((t <= p) & (s > t) & (s <= p)))
        masks.append((t // (2 * m) == s // (2 * m)) & (t % (2 * m) >= m) & (s % (2 * m) < m))
    w = np.stack(mats).astype(np.float32)
    k = np.stack(masks).astype(np.float32)
    w = np.stack([w, w[:, ::-1, ::-1]]).reshape(2, len(mats) * c, c)
    w = np.concatenate([w, w], axis=2)
    k = np.stack([k, k[:, ::-1, ::-1]])
    k = np.concatenate([k, k], axis=3)
    return jnp.asarray(w, BF16), jnp.asarray(k, BF16)


def _block_diag(x):
    zero = jnp.zeros((x.shape[0], LANES), x.dtype)
    return jnp.concatenate([jnp.concatenate([x[:, :LANES], zero], axis=1),
                            jnp.concatenate([zero, x[:, LANES:]], axis=1)], axis=0)


def _hgrn_chunk(q, z, v, lb, wmat, m_ref, st, backward):
    e = jnp.exp(-jnp.abs(z))
    r = 1.0 / (1.0 + e)
    pos = z >= 0
    sig = jnp.where(pos, r, e * r)
    sig_neg = jnp.where(pos, e * r, r)
    logf = jnp.log2(jnp.maximum(lb + (1.0 - lb) * sig, MIN_FORGET))
    kk = (1.0 - lb) * sig_neg
    hi = logf.astype(BF16)
    lo = (logf - hi.astype(F32)).astype(BF16)
    dec = jnp.dot(wmat, jnp.concatenate([hi, lo], axis=0), preferred_element_type=F32)
    b = dec[0:CHUNK]
    total = b[0:1] if backward else b[CHUNK - 1:CHUNK]
    q_in = (q * jnp.exp2(b)).astype(BF16)
    k_out = (kk * jnp.exp2(total - b)).astype(BF16)
    vb = v.astype(BF16)
    qb = q.astype(BF16)
    kb = kk.astype(BF16)
    o = lax.dot_general(q_in, _block_diag(st.astype(BF16)), _NT, preferred_element_type=F32)
    qk = q * kk
    diag = jnp.concatenate(
        [jnp.broadcast_to(jnp.sum(qk[:, :LANES], axis=-1, keepdims=True), (CHUNK, LANES)),
         jnp.broadcast_to(jnp.sum(qk[:, LANES:], axis=-1, keepdims=True), (CHUNK, LANES))], axis=1)
    o = o + diag * v
    a = None
    for lvl in range(N_LEVELS):
        m = CHUNK >> (lvl + 1)
        if m >= MXU_LEVEL_BELOW:
            parts = []
            for j in range(CHUNK // (2 * m)):
                lo_rows = b[2 * m * j:2 * m * j + m]
                hi_rows = b[2 * m * j + m:2 * m * (j + 1)]
                if backward:
                    piv = b[2 * m * j + m:2 * m * j + m + 1]
                    parts += [lo_rows - piv, piv - hi_rows]
                else:
                    piv = b[2 * m * j + m - 1:2 * m * j + m]
                    parts += [piv - lo_rows, hi_rows - piv]
            nl = jnp.concatenate(parts, axis=0)
        else:
            i = 1 + lvl - (N_LEVELS - N_MXU_LEVELS)
            nl = dec[i * CHUNK:(i + 1) * CHUNK]
        el = jnp.exp2(nl).astype(BF16)
        sc = lax.dot_general(qb * el, _block_diag(kb * el), _NT, preferred_element_type=F32)
        sc = sc.astype(BF16) * m_ref[lvl]
        a = sc if a is None else a + sc
    o = o + jnp.dot(a, _block_diag(vb), preferred_element_type=F32)
    v_rows = jnp.concatenate([vb[:, :LANES], vb[:, LANES:]], axis=0)
    st_new = st * jnp.exp2(total) + lax.dot_general(v_rows, _block_diag(k_out), _TN,
                                                    preferred_element_type=F32)
    return o, st_new


def _hgrn_kernel(qf_ref, zf_ref, vf_ref, qb_ref, zb_ref, vb_ref, lb_ref, w_ref, m_ref,
                 of_ref, ob_ref, stf_ref, stb_ref):
    @pl.when(pl.program_id(0) == 0)
    def _():
        stf_ref[...] = jnp.zeros_like(stf_ref)
        stb_ref[...] = jnp.zeros_like(stb_ref)

    for p in range(HGRN_HEADS // 2):
        o, st = _hgrn_chunk(qf_ref[p], zf_ref[p], vf_ref[p], lb_ref[0, p], w_ref[0], m_ref.at[0],
                            stf_ref[p], False)
        of_ref[p] = o
        stf_ref[p] = st
        o, st = _hgrn_chunk(qb_ref[p], zb_ref[p], vb_ref[p], lb_ref[1, p], w_ref[1], m_ref.at[1],
                            stb_ref[p], True)
        ob_ref[p] = o
        stb_ref[p] = st


def _hgrn(proj_h, lb):
    s = proj_h.shape[1]
    n = s // CHUNK
    pairs = HGRN_HEADS // 2
    wmat, masks = _hgrn_constants()
    hb = (pairs, CHUNK, 2 * LANES)
    fwd = lambda slab: pl.BlockSpec(hb, lambda c: (slab, c, 0))
    bwd = lambda slab: pl.BlockSpec(hb, lambda c: (slab, n - 1 - c, 0))
    const = lambda a: pl.BlockSpec(a.shape, lambda c: (0,) * a.ndim)
    out_shape = jax.ShapeDtypeStruct((pairs, s, 2 * LANES), F32)
    state = pltpu.VMEM((pairs, LANES, 2 * LANES), F32)
    return pl.pallas_call(
        _hgrn_kernel,
        grid=(n,),
        in_specs=[fwd(0), fwd(1), fwd(3), bwd(0), bwd(2), bwd(3), const(lb), const(wmat), const(masks)],
        out_specs=[fwd(0), bwd(0)],
        out_shape=[out_shape, out_shape],
        scratch_shapes=[state, state],
        compiler_params=_params("arbitrary"),
        name="hgrn2_scan",
    )(proj_h, proj_h, proj_h, proj_h, proj_h, proj_h, lb, wmat, masks)


def _t5_bucket(rel):
    half = REL_BUCKETS // 2
    ret = (rel > 0).astype(np.int32) * half
    n = np.abs(rel)
    max_exact = half // 2
    large = max_exact + (np.log(np.maximum(n, 1) / max_exact)
                         / np.log(REL_MAX_DISTANCE / max_exact)
                         * (half - max_exact)).astype(np.int32)
    large = np.minimum(large, half - 1)
    return (ret + np.where(n < max_exact, n, large)).astype(np.int32)


def _attn_bias(rel_bias_table):
    span = 3 * ATTN_QBLK - 1
    rel = np.arange(span) - (ATTN_QBLK - 1) - ATTN_SIDE
    band = jnp.asarray(np.abs(np.arange(2 * ATTN_QBLK)[None, :] - ATTN_SIDE
                              - np.arange(ATTN_QBLK)[:, None]) <= ATTN_SIDE)
    out = []
    for gi, (_, dil) in enumerate(ATTN_GROUPS):
        tab = rel_bias_table[:, gi * HEADS_PER_GROUP:(gi + 1) * HEADS_PER_GROUP].astype(F32)
        onehot = jnp.asarray(_t5_bucket(rel * dil)[:, None] == np.arange(REL_BUCKETS)[None, :], F32)
        vec = jnp.einsum("rb,bh->hr", onehot, tab, precision=lax.Precision.HIGHEST)
        period = span + 2
        vec = jnp.pad(vec, ((0, 0), (0, period - span)))
        flat = jnp.tile(vec, (1, ATTN_QBLK))[:, :ATTN_QBLK * (span + 1)]
        toep = flat.reshape(HEADS_PER_GROUP, ATTN_QBLK, span + 1)[:, :, ATTN_QBLK - 1:3 * ATTN_QBLK - 1]
        out.append(jnp.where(band[None], toep, NEG_INF))
    return jnp.stack(out)


def _attn_kernel(q_ref, kp_ref, km_ref, kn_ref, vp_ref, vm_ref, vn_ref, bias_ref,
                 o_ref, lse_ref, k_s, v_s, *, rows, length, dil):
    c = pl.program_id(2)
    i = pl.program_id(1)
    k_s[0:ATTN_SIDE] = kp_ref[...]
    k_s[ATTN_SIDE:ATTN_SIDE + rows] = km_ref[...]
    k_s[ATTN_SIDE + rows:] = kn_ref[...]
    v_s[0:ATTN_SIDE] = vp_ref[...]
    v_s[ATTN_SIDE:ATTN_SIDE + rows] = vm_ref[...]
    v_s[ATTN_SIDE + rows:] = vn_ref[...]
    bias = bias_ref[...]
    col = lax.broadcasted_iota(jnp.int32, (ATTN_QBLK, 2 * ATTN_QBLK), 1)
    scale = LANES ** -0.5
    for j in range(rows // ATTN_QBLK):
        r0 = j * ATTN_QBLK
        key_pos = col + (i * rows + r0 - ATTN_SIDE)
        valid = (key_pos >= 0) & (key_pos < length)
        s = lax.dot_general(q_ref[r0:r0 + ATTN_QBLK], k_s[r0:r0 + 2 * ATTN_QBLK], _NT,
                            preferred_element_type=F32) * scale + bias
        s = jnp.where(valid, s, NEG_INF)
        m = jnp.max(s, axis=-1, keepdims=True)
        p = jnp.exp(s - m)
        l = jnp.sum(p, axis=-1, keepdims=True)
        o = jnp.dot(p.astype(BF16), v_s[r0:r0 + 2 * ATTN_QBLK], preferred_element_type=F32) / l
        lse = jnp.broadcast_to(m + jnp.log(l), (ATTN_QBLK, LANES))
        if dil == 1:
            o_ref[r0:r0 + ATTN_QBLK] = o
            lse_ref[r0:r0 + ATTN_QBLK] = lse
        else:
            o_ref[pl.ds(r0 * dil + c, ATTN_QBLK, stride=dil), :] = o
            lse_ref[pl.ds(r0 * dil + c, ATTN_QBLK, stride=dil), :] = lse


def _attn_group(qkv, bias, gi):
    dil, length = qkv.shape[1], qkv.shape[2]
    s = dil * length
    rows = min(512, length)
    nblk = length // rows
    sub = rows // ATTN_SIDE
    last = length // ATTN_SIDE - 1
    hp = HEADS_PER_GROUP

    def main(kind):
        return pl.BlockSpec((None, None, rows, LANES), lambda g, i, c: (kind * hp + g, c, i, 0))

    def prev(kind):
        return pl.BlockSpec((None, None, ATTN_SIDE, LANES),
                            lambda g, i, c: (kind * hp + g, c, jnp.maximum(i * sub - 1, 0), 0))

    def nxt(kind):
        return pl.BlockSpec((None, None, ATTN_SIDE, LANES),
                            lambda g, i, c: (kind * hp + g, c, jnp.minimum((i + 1) * sub, last), 0))

    out_spec = pl.BlockSpec((None, rows * dil, LANES), lambda g, i, c: (g, i, 0))
    out_shape = jax.ShapeDtypeStruct((hp, s, LANES), F32)
    return pl.pallas_call(
        functools.partial(_attn_kernel, rows=rows, length=length, dil=dil),
        grid=(hp, nblk, dil),
        in_specs=[main(0), prev(1), main(1), nxt(1), prev(2), main(2), nxt(2),
                  pl.BlockSpec((None, None, ATTN_QBLK, 2 * ATTN_QBLK), lambda g, i, c: (gi, g, 0, 0))],
        out_specs=[out_spec, out_spec],
        out_shape=[out_shape, out_shape],
        scratch_shapes=[pltpu.VMEM((rows + 2 * ATTN_SIDE, LANES), BF16),
                        pltpu.VMEM((rows + 2 * ATTN_SIDE, LANES), BF16)],
        compiler_params=_params("parallel", "arbitrary", "arbitrary"),
        name=f"dilated_attn_g{gi}",
    )(qkv, qkv, qkv, qkv, qkv, qkv, qkv, bias)


def _merge_kernel(of_ref, ob_ref, g_ref, a0_ref, a1_ref, a2_ref, l0_ref, l1_ref, l2_ref, gate_ref,
                  x_ref, pa_ref, pb_ref, wo_ref, hw_ref, nw_ref, xo_ref, ho_ref, hs, as_):
    for h in range(HGRN_HEADS):
        lanes = slice((h % 2) * LANES, (h % 2 + 1) * LANES)
        o = of_ref[h // 2, :, lanes] + ob_ref[h // 2, :, lanes]
        o = o * lax.rsqrt(jnp.mean(o * o, axis=-1, keepdims=True) + NORM_EPS)
        g = g_ref[h // 2, :, lanes]
        hs[:, h * LANES:(h + 1) * LANES] = (o * hw_ref[h] * (g * _sigmoid(g))).astype(BF16)
    for h in range(HEADS_PER_GROUP):
        l0, l1, l2 = l0_ref[h], l1_ref[h], l2_ref[h]
        m = jnp.maximum(jnp.maximum(l0, l1), l2)
        e0, e1, e2 = jnp.exp(l0 - m), jnp.exp(l1 - m), jnp.exp(l2 - m)
        o = (e0 * a0_ref[h] + e1 * a1_ref[h] + e2 * a2_ref[h]) / (e0 + e1 + e2)
        as_[:, h * LANES:(h + 1) * LANES] = o.astype(BF16)
    d = x_ref.shape[1]
    a = jnp.dot(hs[...], pa_ref[...], preferred_element_type=F32)
    b = jnp.dot(as_[...], pb_ref[...], preferred_element_type=F32)
    merged = _sigmoid(gate_ref[:, :d]) * a.astype(BF16) + _sigmoid(gate_ref[:, d:]) * b.astype(BF16)
    xn = x_ref[...] + jnp.dot(merged, wo_ref[...], preferred_element_type=F32)
    xo_ref[...] = xn
    ho_ref[...] = _rms(xn, nw_ref[...]).astype(ho_ref.dtype)


def _merge(o_hgrn, proj_h, attn, gates, x, pa, pb, wo, hgrn_norm_w, norm_w, layer, tm=256):
    s, d = x.shape
    (a0, l0), (a1, l1), (a2, l2) = attn
    aspec = pl.BlockSpec((HEADS_PER_GROUP, tm, LANES), lambda i: (0, i, 0))
    pairs = HGRN_HEADS // 2
    hspec = pl.BlockSpec((pairs, tm, 2 * LANES), lambda i: (0, i, 0))

    def wspec(rows):
        return pl.BlockSpec((None, rows, d), lambda i: (layer, 0, 0))

    return pl.pallas_call(
        _merge_kernel,
        grid=(s // tm,),
        in_specs=[hspec, hspec,
                  pl.BlockSpec((pairs, tm, 2 * LANES), lambda i: (4, i, 0)),
                  aspec, aspec, aspec, aspec, aspec, aspec,
                  pl.BlockSpec((tm, 2 * d), lambda i: (i, 0)),
                  pl.BlockSpec((tm, d), lambda i: (i, 0)),
                  wspec(HGRN_WIDTH), wspec(ATTN_OUT_WIDTH), wspec(d),
                  pl.BlockSpec((HGRN_HEADS, 1, LANES), lambda i: (0, 0, 0)),
                  pl.BlockSpec((1, d), lambda i: (0, 0))],
        out_specs=[pl.BlockSpec((tm, d), lambda i: (i, 0)),
                   pl.BlockSpec((tm, d), lambda i: (i, 0))],
        out_shape=[jax.ShapeDtypeStruct((s, d), F32), jax.ShapeDtypeStruct((s, d), BF16)],
        scratch_shapes=[pltpu.VMEM((tm, HGRN_WIDTH), BF16), pltpu.VMEM((tm, ATTN_OUT_WIDTH), BF16)],
        compiler_params=_params("parallel"),
        name="merge_out_proj",
    )(o_hgrn[0], o_hgrn[1], proj_h, a0, a1, a2, l0, l1, l2, gates, x, pa, pb, wo,
      hgrn_norm_w.reshape(HGRN_HEADS, 1, LANES), norm_w.reshape(1, d))


def _mlp_kernel(h_ref, x_ref, wu_ref, wd_ref, nw_ref, xo_ref, ho_ref, acc):
    f = pl.program_id(1)

    @pl.when(f == 0)
    def _():
        acc[...] = x_ref[...]

    u = jnp.maximum(jnp.dot(h_ref[...], wu_ref[...], preferred_element_type=F32), 0.0)
    acc[...] += jnp.dot((u * u).astype(BF16), wd_ref[...], preferred_element_type=F32)

    @pl.when(f == pl.num_programs(1) - 1)
    def _():
        xn = acc[...]
        xo_ref[...] = xn
        ho_ref[...] = _rms(xn, nw_ref[...]).astype(ho_ref.dtype)


def _mlp(h, x, wu, wd, norm_w, layer, h_dtype, tm=512, tf=1024):
    s, d = x.shape
    ff = wu.shape[2]
    return pl.pallas_call(
        _mlp_kernel,
        grid=(s // tm, ff // tf),
        in_specs=[pl.BlockSpec((tm, d), lambda i, f: (i, 0)),
                  pl.BlockSpec((tm, d), lambda i, f: (i, 0)),
                  pl.BlockSpec((None, d, tf), lambda i, f: (layer, 0, f)),
                  pl.BlockSpec((None, tf, d), lambda i, f: (layer, f, 0)),
                  pl.BlockSpec((1, d), lambda i, f: (0, 0))],
        out_specs=[pl.BlockSpec((tm, d), lambda i, f: (i, 0)),
                   pl.BlockSpec((tm, d), lambda i, f: (i, 0))],
        out_shape=[jax.ShapeDtypeStruct((s, d), F32), jax.ShapeDtypeStruct((s, d), h_dtype)],
        scratch_shapes=[pltpu.VMEM((tm, d), F32)],
        compiler_params=_params("parallel", "arbitrary"),
        name="mlp_relu2",
    )(h, x, wu, wd, norm_w.reshape(1, d))


def _lower_bounds(logits):
    p = jax.nn.softmax(logits.astype(F32), axis=0)
    return jnp.cumsum(p, axis=0) - p[0:1]


def kernel(x, w_in, hgrn_lb_fwd, hgrn_lb_bwd, hgrn_norm_w, rel_bias_table, w_branch_hgrn,
           w_branch_attn, w_out, norm_mix_w, norm_mlp_w, w_up, w_down, final_norm_w):
    batch, s, d = x.shape
    depth = w_in.shape[0]
    assert batch == 1
    x = x.reshape(s, d)
    lb = jnp.stack([_lower_bounds(hgrn_lb_fwd), _lower_bounds(hgrn_lb_bwd)], axis=1)
    lb = lb.reshape(-1, 2, HGRN_HEADS // 2, 1, 2 * LANES)
    bias = _attn_bias(rel_bias_table)
    pa_b, pb_b = w_branch_hgrn.astype(BF16), w_branch_attn.astype(BF16)
    wo_b, wu_b, wd_b = w_out.astype(BF16), w_up.astype(BF16), w_down.astype(BF16)
    n_h = 5 * HGRN_WIDTH
    n_a = 3 * ATTN_WIDTH
    h = _norm(x, norm_mix_w[0])
    for l in range(depth):
        proj_h = _proj(h, w_in, l, 0, n_h, F32, 2 * LANES, "proj_hgrn")
        gates = _proj(h, w_in, l, n_h + n_a, 2 * d, BF16, 0, "proj_gates")
        o_hgrn = _hgrn(proj_h, lb[l])
        attn = [_attn_group(_proj_attn(h, w_in, l, n_h, gi), bias, gi)
                for gi in range(len(ATTN_GROUPS))]
        x, h2 = _merge(o_hgrn, proj_h, attn, gates, x, pa_b, pb_b, wo_b, hgrn_norm_w[l],
                       norm_mlp_w[l], l)
        last = l == depth - 1
        x, h = _mlp(h2, x, wu_b, wd_b, final_norm_w if last else norm_mix_w[l + 1], l,
                    F32 if last else BF16)
    return h.reshape(batch, s, d)
```

```python
import functools

import jax
import jax.numpy as jnp
import numpy as np
from jax import lax
from jax.experimental import pallas as pl
from jax.experimental.pallas import tpu as pltpu

F32 = jnp.float32
BF16 = jnp.bfloat16

LANES = 128
HGRN_HEADS = 8
HGRN_WIDTH = HGRN_HEADS * LANES
CHUNK = 128
N_LEVELS = 7
MXU_LEVEL_BELOW = 8
N_MXU_LEVELS = 3
MIN_FORGET = 1e-30
ATTN_GROUPS = ((128, 1), (512, 4), (2048, 16))
HEADS_PER_GROUP = 4
N_ATTN_HEADS = HEADS_PER_GROUP * len(ATTN_GROUPS)
ATTN_WIDTH = N_ATTN_HEADS * LANES
ATTN_OUT_WIDTH = HEADS_PER_GROUP * LANES
ATTN_QBLK = 128
ATTN_SIDE = 64
ATTN_ROWS = 1024
ATTN_FOLD_ROWS = 256
REL_BUCKETS = 32
REL_MAX_DISTANCE = 1024
NORM_EPS = 1e-6
NEG_INF = -1e30
VMEM_LIMIT = 56 * 1024 * 1024

_NT = (((1,), (1,)), ((), ()))
_TN = (((0,), (0,)), ((), ()))


def _params(*sem):
    return pltpu.CompilerParams(dimension_semantics=sem, vmem_limit_bytes=VMEM_LIMIT)


def _rms(x, w):
    return x * lax.rsqrt(jnp.mean(x * x, axis=-1, keepdims=True) + NORM_EPS) * w


def _sigmoid(x):
    return 0.5 * jnp.tanh(0.5 * x) + 0.5


def _norm_kernel(x_ref, w_ref, o_ref):
    o_ref[...] = _rms(x_ref[...], w_ref[...]).astype(o_ref.dtype)


def _norm(x, w, tm=512):
    s, d = x.shape
    return pl.pallas_call(
        _norm_kernel,
        grid=(s // tm,),
        in_specs=[pl.BlockSpec((tm, d), lambda i: (i, 0)),
                  pl.BlockSpec((1, d), lambda i: (0, 0))],
        out_specs=pl.BlockSpec((tm, d), lambda i: (i, 0)),
        out_shape=jax.ShapeDtypeStruct((s, d), BF16),
        compiler_params=_params("parallel"),
        name="rmsnorm_in",
    )(x, w.reshape(1, d))


def _proj_kernel(x_ref, w_ref, o_ref, *, slabs):
    acc = jnp.dot(x_ref[...], w_ref[...].astype(BF16), preferred_element_type=F32)
    if slabs:
        width = o_ref.shape[-1]
        for j in range(slabs):
            o_ref[j] = acc[:, j * width:(j + 1) * width].astype(o_ref.dtype)
    else:
        o_ref[...] = acc.astype(o_ref.dtype)


def _proj(h, w, layer, col0, n, out_dtype, slab_width, name, tm=2048, tn=512):
    s, d = h.shape
    off = col0 // tn
    if slab_width:
        slabs = tn // slab_width
        out_shape = jax.ShapeDtypeStruct((n // slab_width, s, slab_width), out_dtype)
        out_spec = pl.BlockSpec((slabs, tm, slab_width), lambda i, j: (j, i, 0))
    else:
        slabs = 0
        out_shape = jax.ShapeDtypeStruct((s, n), out_dtype)
        out_spec = pl.BlockSpec((tm, tn), lambda i, j: (i, j))
    return pl.pallas_call(
        functools.partial(_proj_kernel, slabs=slabs),
        grid=(s // tm, n // tn),
        in_specs=[pl.BlockSpec((tm, d), lambda i, j: (i, 0)),
                  pl.BlockSpec((None, d, tn), lambda i, j: (layer, 0, j + off))],
        out_specs=out_spec,
        out_shape=out_shape,
        compiler_params=_params("parallel", "arbitrary"),
        name=name,
    )(h, w)


def _proj_attn_kernel(x_ref, w_ref, o_ref, scr, *, dil):
    acc = jnp.dot(x_ref[...], w_ref[...].astype(BF16), preferred_element_type=F32)
    if dil == 1:
        for hh in range(HEADS_PER_GROUP):
            o_ref[hh, 0] = acc[:, hh * LANES:(hh + 1) * LANES].astype(o_ref.dtype)
    else:
        rows = scr.shape[1] // dil
        for hh in range(HEADS_PER_GROUP):
            scr[hh] = acc[:, hh * LANES:(hh + 1) * LANES]
        for hh in range(HEADS_PER_GROUP):
            for c in range(dil):
                o_ref[hh, c] = scr[hh, pl.ds(c, rows, stride=dil), :].astype(o_ref.dtype)


def _proj_attn(h, w, layer, col0, gi, tm=2048):
    s, d = h.shape
    dil = ATTN_GROUPS[gi][1]
    tn = ATTN_OUT_WIDTH
    off = col0 // tn + gi
    n_groups = len(ATTN_GROUPS)
    return pl.pallas_call(
        functools.partial(_proj_attn_kernel, dil=dil),
        grid=(s // tm, 3),
        in_specs=[pl.BlockSpec((tm, d), lambda i, j: (i, 0)),
                  pl.BlockSpec((None, d, tn), lambda i, j: (layer, 0, off + n_groups * j))],
        out_specs=pl.BlockSpec((HEADS_PER_GROUP, dil, tm // dil, LANES), lambda i, j: (j, 0, i, 0)),
        out_shape=jax.ShapeDtypeStruct((3 * HEADS_PER_GROUP, dil, s // dil, LANES), BF16),
        scratch_shapes=[pltpu.VMEM((HEADS_PER_GROUP, tm, LANES), F32)],
        compiler_params=_params("parallel", "arbitrary"),
        name=f"proj_attn_g{gi}",
    )(h, w)


def _hgrn_constants():
    c = CHUNK
    t = np.arange(c)[:, None]
    s = np.arange(c)[None, :]
    mats = [s <= t]
    masks = []
    for lvl in range(N_LEVELS):
        m = c >> (lvl + 1)
        p = (t // (2 * m)) * (2 * m) + m - 1
        if m < MXU_LEVEL_BELOW:
            mats.append(((t > p) & (s > p) & (s <= t)) | ((t <= p) & (s > t) & (s <= p)))
        masks.append((t // (2 * m) == s // (2 * m)) & (t % (2 * m) >= m) & (s % (2 * m) < m))
    w = np.stack(mats).astype(np.float32)
    k = np.stack(masks).astype(np.float32)
    w = np.stack([w, w[:, ::-1, ::-1]]).reshape(2, len(mats) * c, c)
    w = np.concatenate([w, w], axis=2)
    k = np.stack([k, k[:, ::-1, ::-1]])
    k = np.concatenate([k, k], axis=3)
    return jnp.asarray(w, BF16), jnp.asarray(k, BF16)


def _block_diag(x):
    zero = jnp.zeros((x.shape[0], LANES), x.dtype)
    return jnp.concatenate([jnp.concatenate([x[:, :LANES], zero], axis=1),
                            jnp.concatenate([zero, x[:, LANES:]], axis=1)], axis=0)


def _hgrn_chunk(q, z, v, lb, wmat, m_ref, st, backward):
    half = 0.5 * (1.0 - lb)
    ct = half * jnp.tanh(0.5 * z)
    kk = half - ct
    logf = jnp.log2(jnp.maximum((lb + half) + ct, MIN_FORGET))
    hi = logf.astype(BF16)
    lo = (logf - hi.astype(F32)).astype(BF16)
    dec = jnp.dot(wmat, jnp.concatenate([hi, lo], axis=0), preferred_element_type=F32)
    b = dec[0:CHUNK]
    total = b[0:1] if backward else b[CHUNK - 1:CHUNK]
    q_in = (q * jnp.exp2(b)).astype(BF16)
    k_out = (kk * jnp.exp2(total - b)).astype(BF16)
    vb = v.astype(BF16)
    qb = q.astype(BF16)
    kb = kk.astype(BF16)
    o = lax.dot_general(q_in, _block_diag(st.astype(BF16)), _NT, preferred_element_type=F32)
    qk = q * kk
    diag = jnp.concatenate(
        [jnp.broadcast_to(jnp.sum(qk[:, :LANES], axis=-1, keepdims=True), (CHUNK, LANES)),
         jnp.broadcast_to(jnp.sum(qk[:, LANES:], axis=-1, keepdims=True), (CHUNK, LANES))], axis=1)
    o = o + diag * v
    a = None
    for lvl in range(N_LEVELS):
        m = CHUNK >> (lvl + 1)
        if m >= MXU_LEVEL_BELOW:
            parts = []
            for j in range(CHUNK // (2 * m)):
                lo_rows = b[2 * m * j:2 * m * j + m]
                hi_rows = b[2 * m * j + m:2 * m * (j + 1)]
                if backward:
                    piv = b[2 * m * j + m:2 * m * j + m + 1]
                    parts += [lo_rows - piv, piv - hi_rows]
                else:
                    piv = b[2 * m * j + m - 1:2 * m * j + m]
                    parts += [piv - lo_rows, hi_rows - piv]
            nl = jnp.concatenate(parts, axis=0)
        else:
            i = 1 + lvl - (N_LEVELS - N_MXU_LEVELS)
            nl = dec[i * CHUNK:(i + 1) * CHUNK]
        el = jnp.exp2(nl).astype(BF16)
        sc = lax.dot_general(qb * el, _block_diag(kb * el), _NT, preferred_element_type=F32)
        sc = sc.astype(BF16) * m_ref[lvl]
        a = sc if a is None else a + sc
    o = o + jnp.dot(a, _block_diag(vb), preferred_element_type=F32)
    v_rows = jnp.concatenate([vb[:, :LANES], vb[:, LANES:]], axis=0)
    st_new = st * jnp.exp2(total) + lax.dot_general(v_rows, _block_diag(k_out), _TN,
                                                    preferred_element_type=F32)
    return o, st_new


def _hgrn_kernel(qf_ref, zf_ref, vf_ref, qb_ref, zb_ref, vb_ref, lb_ref, w_ref, m_ref,
                 of_ref, ob_ref, stf_ref, stb_ref):
    @pl.when(pl.program_id(0) == 0)
    def _():
        stf_ref[...] = jnp.zeros_like(stf_ref)
        stb_ref[...] = jnp.zeros_like(stb_ref)

    for p in range(HGRN_HEADS // 2):
        o, st = _hgrn_chunk(qf_ref[p], zf_ref[p], vf_ref[p], lb_ref[0, p], w_ref[0], m_ref.at[0],
                            stf_ref[p], False)
        of_ref[p] = o
        stf_ref[p] = st
        o, st = _hgrn_chunk(qb_ref[p], zb_ref[p], vb_ref[p], lb_ref[1, p], w_ref[1], m_ref.at[1],
                            stb_ref[p], True)
        ob_ref[p] = o
        stb_ref[p] = st


def _hgrn(proj_h, lb):
    s = proj_h.shape[1]
    n = s // CHUNK
    pairs = HGRN_HEADS // 2
    wmat, masks = _hgrn_constants()
    hb = (pairs, CHUNK, 2 * LANES)
    fwd = lambda slab: pl.BlockSpec(hb, lambda c: (slab, c, 0))
    bwd = lambda slab: pl.BlockSpec(hb, lambda c: (slab, n - 1 - c, 0))
    const = lambda a: pl.BlockSpec(a.shape, lambda c: (0,) * a.ndim)
    out_shape = jax.ShapeDtypeStruct((pairs, s, 2 * LANES), F32)
    state = pltpu.VMEM((pairs, LANES, 2 * LANES), F32)
    return pl.pallas_call(
        _hgrn_kernel,
        grid=(n,),
        in_specs=[fwd(0), fwd(1), fwd(3), bwd(0), bwd(2), bwd(3), const(lb), const(wmat), const(masks)],
        out_specs=[fwd(0), bwd(0)],
        out_shape=[out_shape, out_shape],
        scratch_shapes=[state, state],
        compiler_params=_params("arbitrary"),
        name="hgrn2_scan",
    )(proj_h, proj_h, proj_h, proj_h, proj_h, proj_h, lb, wmat, masks)


def _t5_bucket(rel):
    half = REL_BUCKETS // 2
    ret = (rel > 0).astype(np.int32) * half
    n = np.abs(rel)
    max_exact = half // 2
    large = max_exact + (np.log(np.maximum(n, 1) / max_exact)
                         / np.log(REL_MAX_DISTANCE / max_exact)
                         * (half - max_exact)).astype(np.int32)
    large = np.minimum(large, half - 1)
    return (ret + np.where(n < max_exact, n, large)).astype(np.int32)


def _attn_bias(rel_bias_table):
    span = 3 * ATTN_QBLK - 1
    rel = np.arange(span) - (ATTN_QBLK - 1) - ATTN_SIDE
    band = jnp.asarray(np.abs(np.arange(2 * ATTN_QBLK)[None, :] - ATTN_SIDE
                              - np.arange(ATTN_QBLK)[:, None]) <= ATTN_SIDE)
    out = []
    for gi, (_, dil) in enumerate(ATTN_GROUPS):
        tab = rel_bias_table[:, gi * HEADS_PER_GROUP:(gi + 1) * HEADS_PER_GROUP].astype(F32)
        onehot = jnp.asarray(_t5_bucket(rel * dil)[:, None] == np.arange(REL_BUCKETS)[None, :], F32)
        vec = jnp.einsum("rb,bh->hr", onehot, tab, precision=lax.Precision.HIGHEST)
        period = span + 2
        vec = jnp.pad(vec, ((0, 0), (0, period - span)))
        flat = jnp.tile(vec, (1, ATTN_QBLK))[:, :ATTN_QBLK * (span + 1)]
        toep = flat.reshape(HEADS_PER_GROUP, ATTN_QBLK, span + 1)[:, :, ATTN_QBLK - 1:3 * ATTN_QBLK - 1]
        out.append(jnp.where(band[None], toep, NEG_INF))
    return jnp.stack(out)


def _attn_kernel(q_ref, kp_ref, km_ref, kn_ref, vp_ref, vm_ref, vn_ref, bias_ref, *rest,
                 rows, length, dil, has_prev, final):
    rest = list(rest)
    op_ref, lp_ref = (rest.pop(0), rest.pop(0)) if has_prev else (None, None)
    o_ref = rest.pop(0)
    lse_ref = None if final else rest.pop(0)
    k_s, v_s = rest.pop(0), rest.pop(0)
    o_s, l_s = rest if rest else (o_ref, lse_ref)
    c = pl.program_id(2)
    i = pl.program_id(1)
    k_s[0:ATTN_SIDE] = kp_ref[...]
    k_s[ATTN_SIDE:ATTN_SIDE + rows] = km_ref[...]
    k_s[ATTN_SIDE + rows:] = kn_ref[...]
    v_s[0:ATTN_SIDE] = vp_ref[...]
    v_s[ATTN_SIDE:ATTN_SIDE + rows] = vm_ref[...]
    v_s[ATTN_SIDE + rows:] = vn_ref[...]
    bias = bias_ref[...]
    col = lax.broadcasted_iota(jnp.int32, (ATTN_QBLK, 2 * ATTN_QBLK), 1)
    scale = LANES ** -0.5
    for j in range(rows // ATTN_QBLK):
        r0 = j * ATTN_QBLK
        key_pos = col + (i * rows + r0 - ATTN_SIDE)
        valid = (key_pos >= 0) & (key_pos < length)
        s = lax.dot_general(q_ref[r0:r0 + ATTN_QBLK], k_s[r0:r0 + 2 * ATTN_QBLK], _NT,
                            preferred_element_type=F32) * scale + bias
        s = jnp.where(valid, s, NEG_INF)
        m = jnp.max(s, axis=-1, keepdims=True)
        p = jnp.exp(s - m)
        l = jnp.sum(p, axis=-1, keepdims=True)
        o = jnp.dot(p.astype(BF16), v_s[r0:r0 + 2 * ATTN_QBLK], preferred_element_type=F32) / l
        lse = jnp.broadcast_to(m + jnp.log(l), (ATTN_QBLK, LANES))
        if dil == 1:
            o_s[r0:r0 + ATTN_QBLK] = o
            l_s[r0:r0 + ATTN_QBLK] = lse
        else:
            o_s[pl.ds(r0 * dil + c, ATTN_QBLK, stride=dil), :] = o
            l_s[pl.ds(r0 * dil + c, ATTN_QBLK, stride=dil), :] = lse

    if has_prev:
        @pl.when(c == dil - 1)
        def _():
            def fold(t, carry):
                r = pl.ds(pl.multiple_of(t * ATTN_FOLD_ROWS, ATTN_FOLD_ROWS), ATTN_FOLD_ROWS)
                lp, lc = lp_ref[r, :], l_s[r, :]
                m = jnp.maximum(lp, lc)
                wp, wc = jnp.exp(lp - m), jnp.exp(lc - m)
                den = wp + wc
                o = (wp * op_ref[r, :] + wc * o_s[r, :]) / den
                o_ref[r, :] = o.astype(o_ref.dtype)
                if not final:
                    lse_ref[r, :] = m + jnp.log(den)
                return carry

            lax.fori_loop(0, rows * dil // ATTN_FOLD_ROWS, fold, 0)


def _attn_group(qkv, bias, gi, prev):
    dil, length = qkv.shape[1], qkv.shape[2]
    s = dil * length
    rows = min(ATTN_ROWS, length)
    nblk = length // rows
    sub = rows // ATTN_SIDE
    last = length // ATTN_SIDE - 1
    hp = HEADS_PER_GROUP
    final = gi == len(ATTN_GROUPS) - 1
    has_prev = prev is not None
    assert has_prev or dil == 1

    def main(kind):
        return pl.BlockSpec((None, None, rows, LANES), lambda g, i, c: (kind * hp + g, c, i, 0))

    def prev_blk(kind):
        return pl.BlockSpec((None, None, ATTN_SIDE, LANES),
                            lambda g, i, c: (kind * hp + g, c, jnp.maximum(i * sub - 1, 0), 0))

    def next_blk(kind):
        return pl.BlockSpec((None, None, ATTN_SIDE, LANES),
                            lambda g, i, c: (kind * hp + g, c, jnp.minimum((i + 1) * sub, last), 0))

    run_spec = pl.BlockSpec((None, rows * dil, LANES), lambda g, i, c: (g, i, 0))
    run_shape = jax.ShapeDtypeStruct((hp, s, LANES), F32)
    in_specs = [main(0), prev_blk(1), main(1), next_blk(1), prev_blk(2), main(2), next_blk(2),
                pl.BlockSpec((None, None, ATTN_QBLK, 2 * ATTN_QBLK), lambda g, i, c: (gi, g, 0, 0))]
    args = [qkv] * 7 + [bias]
    if has_prev:
        in_specs += [run_spec, run_spec]
        args += list(prev)
    if final:
        out_specs = pl.BlockSpec((rows * dil, LANES), lambda g, i, c: (i, g))
        out_shape = jax.ShapeDtypeStruct((s, hp * LANES), BF16)
    else:
        out_specs = [run_spec, run_spec]
        out_shape = [run_shape, run_shape]
    scratch = [pltpu.VMEM((rows + 2 * ATTN_SIDE, LANES), BF16),
               pltpu.VMEM((rows + 2 * ATTN_SIDE, LANES), BF16)]
    if has_prev:
        scratch += [pltpu.VMEM((rows * dil, LANES), F32), pltpu.VMEM((rows * dil, LANES), F32)]
    return pl.pallas_call(
        functools.partial(_attn_kernel, rows=rows, length=length, dil=dil, has_prev=has_prev,
                          final=final),
        grid=(hp, nblk, dil),
        in_specs=in_specs,
        out_specs=out_specs,
        out_shape=out_shape,
        scratch_shapes=scratch,
        compiler_params=_params("parallel", "arbitrary", "arbitrary"),
        name=f"dilated_attn_g{gi}",
    )(*args)


def _merge_kernel(of_ref, ob_ref, g_ref, oa_ref, gate_ref, x_ref, pa_ref, pb_ref, wo_ref, hw_ref,
                  nw_ref, xo_ref, ho_ref, hs):
    for h in range(HGRN_HEADS):
        lanes = slice((h % 2) * LANES, (h % 2 + 1) * LANES)
        o = of_ref[h // 2, :, lanes] + ob_ref[h // 2, :, lanes]
        o = o * lax.rsqrt(jnp.mean(o * o, axis=-1, keepdims=True) + NORM_EPS)
        g = g_ref[h // 2, :, lanes]
        hs[:, h * LANES:(h + 1) * LANES] = (o * hw_ref[h] * (g * _sigmoid(g))).astype(BF16)
    d = x_ref.shape[1]
    a = jnp.dot(hs[...], pa_ref[...], preferred_element_type=F32)
    b = jnp.dot(oa_ref[...], pb_ref[...], preferred_element_type=F32)
    merged = _sigmoid(gate_ref[:, :d]) * a.astype(BF16) + _sigmoid(gate_ref[:, d:]) * b.astype(BF16)
    xn = x_ref[...] + jnp.dot(merged, wo_ref[...], preferred_element_type=F32)
    xo_ref[...] = xn
    ho_ref[...] = _rms(xn, nw_ref[...]).astype(ho_ref.dtype)


def _merge(o_hgrn, proj_h, o_attn, gates, x, pa, pb, wo, hgrn_norm_w, norm_w, layer, tm=256):
    s, d = x.shape
    pairs = HGRN_HEADS // 2
    hspec = pl.BlockSpec((pairs, tm, 2 * LANES), lambda i: (0, i, 0))

    def wspec(rows):
        return pl.BlockSpec((None, rows, d), lambda i: (layer, 0, 0))

    return pl.pallas_call(
        _merge_kernel,
        grid=(s // tm,),
        in_specs=[hspec, hspec,
                  pl.BlockSpec((pairs, tm, 2 * LANES), lambda i: (4, i, 0)),
                  pl.BlockSpec((tm, ATTN_OUT_WIDTH), lambda i: (i, 0)),
                  pl.BlockSpec((tm, 2 * d), lambda i: (i, 0)),
                  pl.BlockSpec((tm, d), lambda i: (i, 0)),
                  wspec(HGRN_WIDTH), wspec(ATTN_OUT_WIDTH), wspec(d),
                  pl.BlockSpec((HGRN_HEADS, 1, LANES), lambda i: (0, 0, 0)),
                  pl.BlockSpec((1, d), lambda i: (0, 0))],
        out_specs=[pl.BlockSpec((tm, d), lambda i: (i, 0)),
                   pl.BlockSpec((tm, d), lambda i: (i, 0))],
        out_shape=[jax.ShapeDtypeStruct((s, d), F32), jax.ShapeDtypeStruct((s, d), BF16)],
        scratch_shapes=[pltpu.VMEM((tm, HGRN_WIDTH), BF16)],
        compiler_params=_params("parallel"),
        name="merge_out_proj",
    )(o_hgrn[0], o_hgrn[1], proj_h, o_attn, gates, x, pa, pb, wo,
      hgrn_norm_w.reshape(HGRN_HEADS, 1, LANES), norm_w.reshape(1, d))


def _mlp_kernel(h_ref, x_ref, wu_ref, wd_ref, nw_ref, xo_ref, ho_ref, acc):
    f = pl.program_id(1)

    @pl.when(f == 0)
    def _():
        acc[...] = x_ref[...]

    u = jnp.maximum(jnp.dot(h_ref[...], wu_ref[...], preferred_element_type=F32), 0.0)
    acc[...] += jnp.dot((u * u).astype(BF16), wd_ref[...], preferred_element_type=F32)

    @pl.when(f == pl.num_programs(1) - 1)
    def _():
        xn = acc[...]
        xo_ref[...] = xn
        ho_ref[...] = _rms(xn, nw_ref[...]).astype(ho_ref.dtype)


def _mlp(h, x, wu, wd, norm_w, layer, h_dtype, tm=512, tf=1024):
    s, d = x.shape
    ff = wu.shape[2]
    return pl.pallas_call(
        _mlp_kernel,
        grid=(s // tm, ff // tf),
        in_specs=[pl.BlockSpec((tm, d), lambda i, f: (i, 0)),
                  pl.BlockSpec((tm, d), lambda i, f: (i, 0)),
                  pl.BlockSpec((None, d, tf), lambda i, f: (layer, 0, f)),
                  pl.BlockSpec((None, tf, d), lambda i, f: (layer, f, 0)),
                  pl.BlockSpec((1, d), lambda i, f: (0, 0))],
        out_specs=[pl.BlockSpec((tm, d), lambda i, f: (i, 0)),
                   pl.BlockSpec((tm, d), lambda i, f: (i, 0))],
        out_shape=[jax.ShapeDtypeStruct((s, d), F32), jax.ShapeDtypeStruct((s, d), h_dtype)],
        scratch_shapes=[pltpu.VMEM((tm, d), F32)],
        compiler_params=_params("parallel", "arbitrary"),
        name="mlp_relu2",
    )(h, x, wu, wd, norm_w.reshape(1, d))


def _lower_bounds(logits):
    p = jax.nn.softmax(logits.astype(F32), axis=0)
    return jnp.cumsum(p, axis=0) - p[0:1]


def kernel(x, w_in, hgrn_lb_fwd, hgrn_lb_bwd, hgrn_norm_w, rel_bias_table, w_branch_hgrn,
           w_branch_attn, w_out, norm_mix_w, norm_mlp_w, w_up, w_down, final_norm_w):
    batch, s, d = x.shape
    depth = w_in.shape[0]
    assert batch == 1
    x = x.reshape(s, d)
    lb = jnp.stack([_lower_bounds(hgrn_lb_fwd), _lower_bounds(hgrn_lb_bwd)], axis=1)
    lb = lb.reshape(-1, 2, HGRN_HEADS // 2, 1, 2 * LANES)
    bias = _attn_bias(rel_bias_table)
    pa_b, pb_b = w_branch_hgrn.astype(BF16), w_branch_attn.astype(BF16)
    wo_b, wu_b, wd_b = w_out.astype(BF16), w_up.astype(BF16), w_down.astype(BF16)
    n_h = 5 * HGRN_WIDTH
    n_a = 3 * ATTN_WIDTH
    h = _norm(x, norm_mix_w[0])
    for l in range(depth):
        proj_h = _proj(h, w_in, l, 0, n_h, F32, 2 * LANES, "proj_hgrn")
        gates = _proj(h, w_in, l, n_h + n_a, 2 * d, BF16, 0, "proj_gates")
        o_hgrn = _hgrn(proj_h, lb[l])
        attn = None
        for gi in range(len(ATTN_GROUPS)):
            attn = _attn_group(_proj_attn(h, w_in, l, n_h, gi), bias, gi, attn)
        x, h2 = _merge(o_hgrn, proj_h, attn, gates, x, pa_b, pb_b, wo_b, hgrn_norm_w[l],
                       norm_mlp_w[l], l)
        last = l == depth - 1
        x, h = _mlp(h2, x, wu_b, wd_b, final_norm_w if last else norm_mix_w[l + 1], l,
                    F32 if last else BF16)
    return h.reshape(batch, s, d)
```

```python
import functools

import jax
import jax.numpy as jnp
import numpy as np
from jax import lax
from jax.experimental import pallas as pl
from jax.experimental.pallas import tpu as pltpu

F32 = jnp.float32
BF16 = jnp.bfloat16

LANES = 128
HGRN_HEADS = 8
HGRN_WIDTH = HGRN_HEADS * LANES
CHUNK = 128
N_LEVELS = 7
MXU_LEVEL_BELOW = 8
N_MXU_LEVELS = 3
MIN_FORGET = 1e-30
ATTN_GROUPS = ((128, 1), (512, 4), (2048, 16))
HEADS_PER_GROUP = 4
N_ATTN_HEADS = HEADS_PER_GROUP * len(ATTN_GROUPS)
ATTN_WIDTH = N_ATTN_HEADS * LANES
ATTN_OUT_WIDTH = HEADS_PER_GROUP * LANES
ATTN_QBLK = 128
ATTN_SIDE = 64
ATTN_ROWS = 1024
ATTN_FOLD_ROWS = 256
REL_BUCKETS = 32
REL_MAX_DISTANCE = 1024
NORM_EPS = 1e-6
NEG_INF = -1e30
PROJ_ROW_TILES = 4
VMEM_LIMIT = 56 * 1024 * 1024

_NT = (((1,), (1,)), ((), ()))
_TN = (((0,), (0,)), ((), ()))


def _params(*sem):
    return pltpu.CompilerParams(dimension_semantics=sem, vmem_limit_bytes=VMEM_LIMIT)


def _rms(x, w):
    return x * lax.rsqrt(jnp.mean(x * x, axis=-1, keepdims=True) + NORM_EPS) * w


def _sigmoid(x):
    return 0.5 * jnp.tanh(0.5 * x) + 0.5


def _norm_kernel(x_ref, w_ref, o_ref):
    o_ref[...] = _rms(x_ref[...], w_ref[...]).astype(o_ref.dtype)


def _norm(x, w, tm=512):
    s, d = x.shape
    return pl.pallas_call(
        _norm_kernel,
        grid=(s // tm,),
        in_specs=[pl.BlockSpec((tm, d), lambda i: (i, 0)),
                  pl.BlockSpec((1, d), lambda i: (0, 0))],
        out_specs=pl.BlockSpec((tm, d), lambda i: (i, 0)),
        out_shape=jax.ShapeDtypeStruct((s, d), BF16),
        compiler_params=_params("parallel"),
        name="rmsnorm_in",
    )(x, w.reshape(1, d))


def _cast_rider(cast_in, cast_out, n_chunks):
    step = pl.program_id(0) * pl.num_programs(1) + pl.program_id(1)

    @pl.when(step < n_chunks)
    def _():
        cast_out[...] = cast_in[...].astype(cast_out.dtype)


def _rider_specs(rider, layer, n_j):
    w, axis, n_chunks = rider
    r, c = w.shape[1:]

    def chunk(i, j):
        return jnp.minimum(i * n_j + j, n_chunks - 1)

    if axis == 0:
        blk = (r // n_chunks, c)
        in_spec = pl.BlockSpec((None,) + blk, lambda i, j: (layer, chunk(i, j), 0))
        out_spec = pl.BlockSpec(blk, lambda i, j: (chunk(i, j), 0))
    else:
        blk = (r, c // n_chunks)
        in_spec = pl.BlockSpec((None,) + blk, lambda i, j: (layer, 0, chunk(i, j)))
        out_spec = pl.BlockSpec(blk, lambda i, j: (0, chunk(i, j)))
    return in_spec, out_spec, jax.ShapeDtypeStruct((r, c), BF16)


def _proj_kernel(x_ref, w_ref, cast_in, o_ref, cast_out, *, slabs, n_chunks):
    _cast_rider(cast_in, cast_out, n_chunks)
    acc = jnp.dot(x_ref[...], w_ref[...].astype(BF16), preferred_element_type=F32)
    if slabs:
        width = o_ref.shape[-1]
        for j in range(slabs):
            o_ref[j] = acc[:, j * width:(j + 1) * width].astype(o_ref.dtype)
    else:
        o_ref[...] = acc.astype(o_ref.dtype)


def _proj(h, w, layer, col0, n, out_dtype, slab_width, name, rider, tn=512):
    s, d = h.shape
    tm = s // PROJ_ROW_TILES
    off = col0 // tn
    n_j = n // tn
    assert rider[2] <= PROJ_ROW_TILES * n_j
    if slab_width:
        slabs = tn // slab_width
        out_shape = jax.ShapeDtypeStruct((n // slab_width, s, slab_width), out_dtype)
        out_spec = pl.BlockSpec((slabs, tm, slab_width), lambda i, j: (j, i, 0))
    else:
        slabs = 0
        out_shape = jax.ShapeDtypeStruct((s, n), out_dtype)
        out_spec = pl.BlockSpec((tm, tn), lambda i, j: (i, j))
    r_in, r_out, r_shape = _rider_specs(rider, layer, n_j)
    return pl.pallas_call(
        functools.partial(_proj_kernel, slabs=slabs, n_chunks=rider[2]),
        grid=(s // tm, n_j),
        in_specs=[pl.BlockSpec((tm, d), lambda i, j: (i, 0)),
                  pl.BlockSpec((None, d, tn), lambda i, j: (layer, 0, j + off)), r_in],
        out_specs=[out_spec, r_out],
        out_shape=[out_shape, r_shape],
        compiler_params=_params("arbitrary", "arbitrary"),
        name=name,
    )(h, w, rider[0])


def _proj_attn_kernel(x_ref, w_ref, cast_in, o_ref, cast_out, scr, *, dil, n_chunks):
    _cast_rider(cast_in, cast_out, n_chunks)
    acc = jnp.dot(x_ref[...], w_ref[...].astype(BF16), preferred_element_type=F32)
    if dil == 1:
        for hh in range(HEADS_PER_GROUP):
            o_ref[hh, 0] = acc[:, hh * LANES:(hh + 1) * LANES].astype(o_ref.dtype)
    else:
        rows = scr.shape[1] // dil
        for hh in range(HEADS_PER_GROUP):
            scr[hh] = acc[:, hh * LANES:(hh + 1) * LANES]
        for hh in range(HEADS_PER_GROUP):
            for c in range(dil):
                o_ref[hh, c] = scr[hh, pl.ds(c, rows, stride=dil), :].astype(o_ref.dtype)


def _proj_attn(h, w, layer, col0, gi, rider):
    s, d = h.shape
    tm = s // PROJ_ROW_TILES
    assert rider[2] <= PROJ_ROW_TILES * 3
    dil = ATTN_GROUPS[gi][1]
    tn = ATTN_OUT_WIDTH
    off = col0 // tn + gi
    n_groups = len(ATTN_GROUPS)
    r_in, r_out, r_shape = _rider_specs(rider, layer, 3)
    return pl.pallas_call(
        functools.partial(_proj_attn_kernel, dil=dil, n_chunks=rider[2]),
        grid=(s // tm, 3),
        in_specs=[pl.BlockSpec((tm, d), lambda i, j: (i, 0)),
                  pl.BlockSpec((None, d, tn), lambda i, j: (layer, 0, off + n_groups * j)), r_in],
        out_specs=[pl.BlockSpec((HEADS_PER_GROUP, dil, tm // dil, LANES), lambda i, j: (j, 0, i, 0)),
                   r_out],
        out_shape=[jax.ShapeDtypeStruct((3 * HEADS_PER_GROUP, dil, s // dil, LANES), BF16), r_shape],
        scratch_shapes=[pltpu.VMEM((HEADS_PER_GROUP, tm, LANES), F32)],
        compiler_params=_params("arbitrary", "arbitrary"),
        name=f"proj_attn_g{gi}",
    )(h, w, rider[0])


def _hgrn_constants():
    c = CHUNK
    t = np.arange(c)[:, None]
    s = np.arange(c)[None, :]
    mats = [s <= t]
    masks = []
    for lvl in range(N_LEVELS):
        m = c >> (lvl + 1)
        p = (t // (2 * m)) * (2 * m) + m - 1
        if m < MXU_LEVEL_BELOW:
            mats.append(((t > p) & (s > p) & (s <= t)) | ((t <= p) & (s > t) & (s <= p)))
        masks.append((t // (2 * m) == s // (2 * m)) & (t % (2 * m) >= m) & (s % (2 * m) < m))
    w = np.stack(mats).astype(np.float32)
    k = np.stack(masks).astype(np.float32)
    w = np.stack([w, w[:, ::-1, ::-1]]).reshape(2, len(mats) * c, c)
    w = np.concatenate([w, w], axis=2)
    k = np.stack([k, k[:, ::-1, ::-1]])
    k = np.concatenate([k, k], axis=3)
    return jnp.asarray(w, BF16), jnp.asarray(k, BF16)


def _block_diag(x):
    zero = jnp.zeros((x.shape[0], LANES), x.dtype)
    return jnp.concatenate([jnp.concatenate([x[:, :LANES], zero], axis=1),
                            jnp.concatenate([zero, x[:, LANES:]], axis=1)], axis=0)


def _hgrn_chunk(q, z, v, lb, wmat, m_ref, st, backward):
    half = 0.5 * (1.0 - lb)
    ct = half * jnp.tanh(0.5 * z)
    kk = half - ct
    logf = jnp.log2(jnp.maximum((lb + half) + ct, MIN_FORGET))
    hi = logf.astype(BF16)
    lo = (logf - hi.astype(F32)).astype(BF16)
    dec = jnp.dot(wmat, jnp.concatenate([hi, lo], axis=0), preferred_element_type=F32)
    b = dec[0:CHUNK]
    total = b[0:1] if backward else b[CHUNK - 1:CHUNK]
    q_in = (q * jnp.exp2(b)).astype(BF16)
    k_out = (kk * jnp.exp2(total - b)).astype(BF16)
    vb = v.astype(BF16)
    qb = q.astype(BF16)
    kb = kk.astype(BF16)
    o = lax.dot_general(q_in, _block_diag(st.astype(BF16)), _NT, preferred_element_type=F32)
    qk = q * kk
    diag = jnp.concatenate(
        [jnp.broadcast_to(jnp.sum(qk[:, :LANES], axis=-1, keepdims=True), (CHUNK, LANES)),
         jnp.broadcast_to(jnp.sum(qk[:, LANES:], axis=-1, keepdims=True), (CHUNK, LANES))], axis=1)
    o = o + diag * v
    a = None
    for lvl in range(N_LEVELS):
        m = CHUNK >> (lvl + 1)
        if m >= MXU_LEVEL_BELOW:
            parts = []
            for j in range(CHUNK // (2 * m)):
                lo_rows = b[2 * m * j:2 * m * j + m]
                hi_rows = b[2 * m * j + m:2 * m * (j + 1)]
                if backward:
                    piv = b[2 * m * j + m:2 * m * j + m + 1]
                    parts += [lo_rows - piv, piv - hi_rows]
                else:
                    piv = b[2 * m * j + m - 1:2 * m * j + m]
                    parts += [piv - lo_rows, hi_rows - piv]
            nl = jnp.concatenate(parts, axis=0)
        else:
            i = 1 + lvl - (N_LEVELS - N_MXU_LEVELS)
            nl = dec[i * CHUNK:(i + 1) * CHUNK]
        el = jnp.exp2(nl).astype(BF16)
        sc = lax.dot_general(qb * el, _block_diag(kb * el), _NT, preferred_element_type=F32)
        sc = sc.astype(BF16) * m_ref[lvl]
        a = sc if a is None else a + sc
    o = o + jnp.dot(a, _block_diag(vb), preferred_element_type=F32)
    v_rows = jnp.concatenate([vb[:, :LANES], vb[:, LANES:]], axis=0)
    st_new = st * jnp.exp2(total) + lax.dot_general(v_rows, _block_diag(k_out), _TN,
                                                    preferred_element_type=F32)
    return o, st_new


def _hgrn_kernel(qf_ref, zf_ref, vf_ref, qb_ref, zb_ref, vb_ref, lb_ref, w_ref, m_ref,
                 of_ref, ob_ref, stf_ref, stb_ref):
    @pl.when(pl.program_id(0) == 0)
    def _():
        stf_ref[...] = jnp.zeros_like(stf_ref)
        stb_ref[...] = jnp.zeros_like(stb_ref)

    for p in range(HGRN_HEADS // 2):
        o, st = _hgrn_chunk(qf_ref[p], zf_ref[p], vf_ref[p], lb_ref[0, p], w_ref[0], m_ref.at[0],
                            stf_ref[p], False)
        of_ref[p] = o
        stf_ref[p] = st
        o, st = _hgrn_chunk(qb_ref[p], zb_ref[p], vb_ref[p], lb_ref[1, p], w_ref[1], m_ref.at[1],
                            stb_ref[p], True)
        ob_ref[p] = o
        stb_ref[p] = st


def _hgrn(proj_h, lb):
    s = proj_h.shape[1]
    n = s // CHUNK
    pairs = HGRN_HEADS // 2
    wmat, masks = _hgrn_constants()
    hb = (pairs, CHUNK, 2 * LANES)
    fwd = lambda slab: pl.BlockSpec(hb, lambda c: (slab, c, 0))
    bwd = lambda slab: pl.BlockSpec(hb, lambda c: (slab, n - 1 - c, 0))
    const = lambda a: pl.BlockSpec(a.shape, lambda c: (0,) * a.ndim)
    out_shape = jax.ShapeDtypeStruct((pairs, s, 2 * LANES), F32)
    state = pltpu.VMEM((pairs, LANES, 2 * LANES), F32)
    return pl.pallas_call(
        _hgrn_kernel,
        grid=(n,),
        in_specs=[fwd(0), fwd(1), fwd(3), bwd(0), bwd(2), bwd(3), const(lb), const(wmat), const(masks)],
        out_specs=[fwd(0), bwd(0)],
        out_shape=[out_shape, out_shape],
        scratch_shapes=[state, state],
        compiler_params=_params("arbitrary"),
        name="hgrn2_scan",
    )(proj_h, proj_h, proj_h, proj_h, proj_h, proj_h, lb, wmat, masks)


def _t5_bucket(rel):
    half = REL_BUCKETS // 2
    ret = (rel > 0).astype(np.int32) * half
    n = np.abs(rel)
    max_exact = half // 2
    large = max_exact + (np.log(np.maximum(n, 1) / max_exact)
                         / np.log(REL_MAX_DISTANCE / max_exact)
                         * (half - max_exact)).astype(np.int32)
    large = np.minimum(large, half - 1)
    return (ret + np.where(n < max_exact, n, large)).astype(np.int32)


def _attn_bias(rel_bias_table):
    span = 3 * ATTN_QBLK - 1
    rel = np.arange(span) - (ATTN_QBLK - 1) - ATTN_SIDE
    col = np.arange(2 * ATTN_QBLK)[None, :]
    band = np.abs(col - ATTN_SIDE - np.arange(ATTN_QBLK)[:, None]) <= ATTN_SIDE
    after_start, before_end = col >= ATTN_SIDE, col < 2 * ATTN_QBLK - ATTN_SIDE
    keep = [band, band & after_start, band & before_end, band & after_start & before_end]
    out = []
    for gi, (_, dil) in enumerate(ATTN_GROUPS):
        tab = rel_bias_table[:, gi * HEADS_PER_GROUP:(gi + 1) * HEADS_PER_GROUP].astype(F32)
        onehot = jnp.asarray(_t5_bucket(rel * dil)[:, None] == np.arange(REL_BUCKETS)[None, :], F32)
        vec = jnp.einsum("rb,bh->hr", onehot, tab, precision=lax.Precision.HIGHEST)
        period = span + 2
        vec = jnp.pad(vec, ((0, 0), (0, period - span)))
        flat = jnp.tile(vec, (1, ATTN_QBLK))[:, :ATTN_QBLK * (span + 1)]
        toep = flat.reshape(HEADS_PER_GROUP, ATTN_QBLK, span + 1)[:, :, ATTN_QBLK - 1:3 * ATTN_QBLK - 1]
        out.append(jnp.stack([jnp.where(jnp.asarray(k)[None], toep, NEG_INF) for k in keep], axis=1))
    return jnp.stack(out)


def _attn_kernel(q_ref, kp_ref, km_ref, kn_ref, vp_ref, vm_ref, vn_ref, bias_ref, *rest,
                 rows, dil, has_prev, final):
    rest = list(rest)
    op_ref, lp_ref = (rest.pop(0), rest.pop(0)) if has_prev else (None, None)
    o_ref = rest.pop(0)
    lse_ref = None if final else rest.pop(0)
    o_s, l_s = rest if rest else (o_ref, lse_ref)
    c = pl.program_id(2)
    i = pl.program_id(1)
    scale = LANES ** -0.5
    n_sub = rows // ATTN_QBLK

    def window(prev_ref, main_ref, next_ref, j):
        lo, hi = j * ATTN_QBLK - ATTN_SIDE, (j + 1) * ATTN_QBLK + ATTN_SIDE
        parts = []
        if lo < 0:
            parts.append(prev_ref[...])
        parts.append(main_ref[max(lo, 0):min(hi, rows)])
        if hi > rows:
            parts.append(next_ref[...])
        return parts[0] if len(parts) == 1 else jnp.concatenate(parts, axis=0)

    scores = []
    for j in range(n_sub):
        r0 = j * ATTN_QBLK
        variant = 0
        if j == 0:
            variant = variant + (i == 0).astype(jnp.int32)
        if j == n_sub - 1:
            variant = variant + 2 * (i == pl.num_programs(1) - 1).astype(jnp.int32)
        scores.append(lax.dot_general(q_ref[r0:r0 + ATTN_QBLK], window(kp_ref, km_ref, kn_ref, j), _NT,
                                      preferred_element_type=F32) * scale + bias_ref[variant])
    for j, s in enumerate(scores):
        r0 = j * ATTN_QBLK
        m = jnp.max(s, axis=-1, keepdims=True)
        p = jnp.exp(s - m)
        l = jnp.sum(p, axis=-1, keepdims=True)
        o = jnp.dot(p.astype(BF16), window(vp_ref, vm_ref, vn_ref, j), preferred_element_type=F32) / l
        lse = jnp.broadcast_to(m + jnp.log(l), (ATTN_QBLK, LANES))
        if dil == 1:
            o_s[r0:r0 + ATTN_QBLK] = o
            l_s[r0:r0 + ATTN_QBLK] = lse
        else:
            o_s[pl.ds(r0 * dil + c, ATTN_QBLK, stride=dil), :] = o
            l_s[pl.ds(r0 * dil + c, ATTN_QBLK, stride=dil), :] = lse

    if has_prev:
        @pl.when(c == dil - 1)
        def _():
            def fold(t, carry):
                r = pl.ds(pl.multiple_of(t * ATTN_FOLD_ROWS, ATTN_FOLD_ROWS), ATTN_FOLD_ROWS)
                lp, lc = lp_ref[r, :], l_s[r, :]
                m = jnp.maximum(lp, lc)
                wp, wc = jnp.exp(lp - m), jnp.exp(lc - m)
                den = wp + wc
                o = (wp * op_ref[r, :] + wc * o_s[r, :]) / den
                o_ref[r, :] = o.astype(o_ref.dtype)
                if not final:
                    lse_ref[r, :] = m + jnp.log(den)
                return carry

            lax.fori_loop(0, rows * dil // ATTN_FOLD_ROWS, fold, 0)


def _attn_group(qkv, bias, gi, prev):
    dil, length = qkv.shape[1], qkv.shape[2]
    s = dil * length
    rows = min(ATTN_ROWS, length)
    nblk = length // rows
    sub = rows // ATTN_SIDE
    last = length // ATTN_SIDE - 1
    hp = HEADS_PER_GROUP
    final = gi == len(ATTN_GROUPS) - 1
    has_prev = prev is not None
    assert has_prev or dil == 1

    def main(kind):
        return pl.BlockSpec((None, None, rows, LANES), lambda g, i, c: (kind * hp + g, c, i, 0))

    def prev_blk(kind):
        return pl.BlockSpec((None, None, ATTN_SIDE, LANES),
                            lambda g, i, c: (kind * hp + g, c, jnp.maximum(i * sub - 1, 0), 0))

    def next_blk(kind):
        return pl.BlockSpec((None, None, ATTN_SIDE, LANES),
                            lambda g, i, c: (kind * hp + g, c, jnp.minimum((i + 1) * sub, last), 0))

    run_spec = pl.BlockSpec((None, rows * dil, LANES), lambda g, i, c: (g, i, 0))
    run_shape = jax.ShapeDtypeStruct((hp, s, LANES), F32)
    in_specs = [main(0), prev_blk(1), main(1), next_blk(1), prev_blk(2), main(2), next_blk(2),
                pl.BlockSpec((None, None, 4, ATTN_QBLK, 2 * ATTN_QBLK), lambda g, i, c: (gi, g, 0, 0, 0))]
    args = [qkv] * 7 + [bias]
    if has_prev:
        in_specs += [run_spec, run_spec]
        args += list(prev)
    if final:
        out_specs = pl.BlockSpec((rows * dil, LANES), lambda g, i, c: (i, g))
        out_shape = jax.ShapeDtypeStruct((s, hp * LANES), BF16)
    else:
        out_specs = [run_spec, run_spec]
        out_shape = [run_shape, run_shape]
    scratch = []
    if has_prev:
        scratch = [pltpu.VMEM((rows * dil, LANES), F32), pltpu.VMEM((rows * dil, LANES), F32)]
    return pl.pallas_call(
        functools.partial(_attn_kernel, rows=rows, dil=dil, has_prev=has_prev, final=final),
        grid=(hp, nblk, dil),
        in_specs=in_specs,
        out_specs=out_specs,
        out_shape=out_shape,
        scratch_shapes=scratch,
        compiler_params=_params("parallel", "arbitrary", "arbitrary"),
        name=f"dilated_attn_g{gi}",
    )(*args)


def _merge_kernel(of_ref, ob_ref, g_ref, oa_ref, gate_ref, x_ref, pa_ref, pb_ref, wo_ref, hw_ref,
                  nw_ref, xo_ref, ho_ref, hs):
    for h in range(HGRN_HEADS):
        lanes = slice((h % 2) * LANES, (h % 2 + 1) * LANES)
        o = of_ref[h // 2, :, lanes] + ob_ref[h // 2, :, lanes]
        o = o * lax.rsqrt(jnp.mean(o * o, axis=-1, keepdims=True) + NORM_EPS)
        g = g_ref[h // 2, :, lanes]
        hs[:, h * LANES:(h + 1) * LANES] = (o * hw_ref[h] * (g * _sigmoid(g))).astype(BF16)
    d = x_ref.shape[1]
    a = jnp.dot(hs[...], pa_ref[...], preferred_element_type=F32)
    b = jnp.dot(oa_ref[...], pb_ref[...], preferred_element_type=F32)
    merged = _sigmoid(gate_ref[:, :d]) * a.astype(BF16) + _sigmoid(gate_ref[:, d:]) * b.astype(BF16)
    xn = x_ref[...] + jnp.dot(merged, wo_ref[...], preferred_element_type=F32)
    xo_ref[...] = xn
    ho_ref[...] = _rms(xn, nw_ref[...]).astype(ho_ref.dtype)


def _merge(o_hgrn, proj_h, o_attn, gates, x, pa, pb, wo, hgrn_norm_w, norm_w, tm=256):
    s, d = x.shape
    pairs = HGRN_HEADS // 2
    hspec = pl.BlockSpec((pairs, tm, 2 * LANES), lambda i: (0, i, 0))

    def wspec(rows):
        return pl.BlockSpec((rows, d), lambda i: (0, 0))

    return pl.pallas_call(
        _merge_kernel,
        grid=(s // tm,),
        in_specs=[hspec, hspec,
                  pl.BlockSpec((pairs, tm, 2 * LANES), lambda i: (4, i, 0)),
                  pl.BlockSpec((tm, ATTN_OUT_WIDTH), lambda i: (i, 0)),
                  pl.BlockSpec((tm, 2 * d), lambda i: (i, 0)),
                  pl.BlockSpec((tm, d), lambda i: (i, 0)),
                  wspec(HGRN_WIDTH), wspec(ATTN_OUT_WIDTH), wspec(d),
                  pl.BlockSpec((HGRN_HEADS, 1, LANES), lambda i: (0, 0, 0)),
                  pl.BlockSpec((1, d), lambda i: (0, 0))],
        out_specs=[pl.BlockSpec((tm, d), lambda i: (i, 0)),
                   pl.BlockSpec((tm, d), lambda i: (i, 0))],
        out_shape=[jax.ShapeDtypeStruct((s, d), F32), jax.ShapeDtypeStruct((s, d), BF16)],
        scratch_shapes=[pltpu.VMEM((tm, HGRN_WIDTH), BF16)],
        compiler_params=_params("parallel"),
        name="merge_out_proj",
    )(o_hgrn[0], o_hgrn[1], proj_h, o_attn, gates, x, pa, pb, wo,
      hgrn_norm_w.reshape(HGRN_HEADS, 1, LANES), norm_w.reshape(1, d))


def _mlp_kernel(h_ref, x_ref, wu_ref, wd_ref, nw_ref, xo_ref, ho_ref, acc):
    f = pl.program_id(1)

    @pl.when(f == 0)
    def _():
        acc[...] = x_ref[...]

    u = jnp.maximum(jnp.dot(h_ref[...], wu_ref[...], preferred_element_type=F32), 0.0)
    acc[...] += jnp.dot((u * u).astype(BF16), wd_ref[...], preferred_element_type=F32)

    @pl.when(f == pl.num_programs(1) - 1)
    def _():
        xn = acc[...]
        xo_ref[...] = xn
        ho_ref[...] = _rms(xn, nw_ref[...]).astype(ho_ref.dtype)


def _mlp(h, x, wu, wd, norm_w, h_dtype, tm=512, tf=1024):
    s, d = x.shape
    ff = wu.shape[1]
    return pl.pallas_call(
        _mlp_kernel,
        grid=(s // tm, ff // tf),
        in_specs=[pl.BlockSpec((tm, d), lambda i, f: (i, 0)),
                  pl.BlockSpec((tm, d), lambda i, f: (i, 0)),
                  pl.BlockSpec((d, tf), lambda i, f: (0, f)),
                  pl.BlockSpec((tf, d), lambda i, f: (f, 0)),
                  pl.BlockSpec((1, d), lambda i, f: (0, 0))],
        out_specs=[pl.BlockSpec((tm, d), lambda i, f: (i, 0)),
                   pl.BlockSpec((tm, d), lambda i, f: (i, 0))],
        out_shape=[jax.ShapeDtypeStruct((s, d), F32), jax.ShapeDtypeStruct((s, d), h_dtype)],
        scratch_shapes=[pltpu.VMEM((tm, d), F32)],
        compiler_params=_params("parallel", "arbitrary"),
        name="mlp_relu2",
    )(h, x, wu, wd, norm_w.reshape(1, d))


def _lower_bounds(logits):
    p = jax.nn.softmax(logits.astype(F32), axis=0)
    return jnp.cumsum(p, axis=0) - p[0:1]


def kernel(x, w_in, hgrn_lb_fwd, hgrn_lb_bwd, hgrn_norm_w, rel_bias_table, w_branch_hgrn,
           w_branch_attn, w_out, norm_mix_w, norm_mlp_w, w_up, w_down, final_norm_w):
    batch, s, d = x.shape
    depth = w_in.shape[0]
    assert batch == 1
    x = x.reshape(s, d)
    lb = jnp.stack([_lower_bounds(hgrn_lb_fwd), _lower_bounds(hgrn_lb_bwd)], axis=1)
    lb = lb.reshape(-1, 2, HGRN_HEADS // 2, 1, 2 * LANES)
    bias = _attn_bias(rel_bias_table)
    n_h = 5 * HGRN_WIDTH
    n_a = 3 * ATTN_WIDTH
    cast_rows = 256
    h = _norm(x, norm_mix_w[0])
    for l in range(depth):
        proj_h, wd_b = _proj(h, w_in, l, 0, n_h, F32, 2 * LANES, "proj_hgrn",
                             (w_down, 0, w_down.shape[1] // cast_rows))
        gates, wu_b = _proj(h, w_in, l, n_h + n_a, 2 * d, BF16, 0, "proj_gates",
                            (w_up, 1, w_up.shape[2] // cast_rows))
        o_hgrn = _hgrn(proj_h, lb[l])
        attn = None
        cast = []
        for gi, wt in enumerate((w_out, w_branch_hgrn, w_branch_attn)):
            qkv, wt_b = _proj_attn(h, w_in, l, n_h, gi, (wt, 0, wt.shape[1] // cast_rows))
            attn = _attn_group(qkv, bias, gi, attn)
            cast.append(wt_b)
        wo_b, pa_b, pb_b = cast
        x, h2 = _merge(o_hgrn, proj_h, attn, gates, x, pa_b, pb_b, wo_b, hgrn_norm_w[l],
                       norm_mlp_w[l])
        last = l == depth - 1
        x, h = _mlp(h2, x, wu_b, wd_b, final_norm_w if last else norm_mix_w[l + 1],
                    F32 if last else BF16)
    return h.reshape(batch, s, d)
```

```python
import functools

import jax
import jax.numpy as jnp
import numpy as np
from jax import lax
from jax.experimental import pallas as pl
from jax.experimental.pallas import tpu as pltpu

F32 = jnp.float32
BF16 = jnp.bfloat16

LANES = 128
HGRN_HEADS = 8
HGRN_WIDTH = HGRN_HEADS * LANES
CHUNK = 128
HGRN_BLOCK = 2 * CHUNK
N_LEVELS = 7
MXU_LEVEL_BELOW = 8
N_MXU_LEVELS = 3
MIN_FORGET = 1e-30
ATTN_GROUPS = ((128, 1), (512, 4), (2048, 16))
HEADS_PER_GROUP = 4
N_ATTN_HEADS = HEADS_PER_GROUP * len(ATTN_GROUPS)
ATTN_WIDTH = N_ATTN_HEADS * LANES
ATTN_OUT_WIDTH = HEADS_PER_GROUP * LANES
ATTN_QBLK = 128
ATTN_SIDE = 64
ATTN_TILES = 16
ATTN_RESIDUES = 4
ATTN_FOLD_ROWS = 256
REL_BUCKETS = 32
REL_MAX_DISTANCE = 1024
NORM_EPS = 1e-6
NEG_INF = -1e30
PROJ_ROW_TILES = 4
VMEM_LIMIT = 56 * 1024 * 1024

_NT = (((1,), (1,)), ((), ()))
_TN = (((0,), (0,)), ((), ()))


def _params(*sem):
    return pltpu.CompilerParams(dimension_semantics=sem, vmem_limit_bytes=VMEM_LIMIT)


def _rms(x, w):
    return x * lax.rsqrt(jnp.mean(x * x, axis=-1, keepdims=True) + NORM_EPS) * w


def _sigmoid(x):
    return 0.5 * jnp.tanh(0.5 * x) + 0.5


def _norm_kernel(x_ref, w_ref, o_ref):
    o_ref[...] = _rms(x_ref[...], w_ref[...]).astype(o_ref.dtype)


def _norm(x, w, tm=512):
    s, d = x.shape
    return pl.pallas_call(
        _norm_kernel,
        grid=(s // tm,),
        in_specs=[pl.BlockSpec((tm, d), lambda i: (i, 0)),
                  pl.BlockSpec((1, d), lambda i: (0, 0))],
        out_specs=pl.BlockSpec((tm, d), lambda i: (i, 0)),
        out_shape=jax.ShapeDtypeStruct((s, d), BF16),
        compiler_params=_params("parallel"),
        name="rmsnorm_in",
    )(x, w.reshape(1, d))


def _cast_rider(cast_in, cast_out, n_chunks):
    step = pl.program_id(0) * pl.num_programs(1) + pl.program_id(1)

    @pl.when(step < n_chunks)
    def _():
        cast_out[...] = cast_in[...].astype(cast_out.dtype)


def _rider_specs(rider, layer, n_j):
    w, axis, n_chunks = rider
    r, c = w.shape[1:]

    def chunk(i, j):
        return jnp.minimum(i * n_j + j, n_chunks - 1)

    if axis == 0:
        blk = (r // n_chunks, c)
        in_spec = pl.BlockSpec((None,) + blk, lambda i, j: (layer, chunk(i, j), 0))
        out_spec = pl.BlockSpec(blk, lambda i, j: (chunk(i, j), 0))
    else:
        blk = (r, c // n_chunks)
        in_spec = pl.BlockSpec((None,) + blk, lambda i, j: (layer, 0, chunk(i, j)))
        out_spec = pl.BlockSpec(blk, lambda i, j: (0, chunk(i, j)))
    return in_spec, out_spec, jax.ShapeDtypeStruct((r, c), BF16)


def _proj_kernel(x_ref, w_ref, cast_in, o_ref, cast_out, *, slabs, n_chunks):
    _cast_rider(cast_in, cast_out, n_chunks)
    acc = jnp.dot(x_ref[...], w_ref[...].astype(BF16), preferred_element_type=F32)
    if slabs:
        width = o_ref.shape[-1]
        for j in range(slabs):
            o_ref[j] = acc[:, j * width:(j + 1) * width].astype(o_ref.dtype)
    else:
        o_ref[...] = acc.astype(o_ref.dtype)


def _proj(h, w, layer, col0, n, out_dtype, slab_width, name, rider, tn=512):
    s, d = h.shape
    tm = s // PROJ_ROW_TILES
    off = col0 // tn
    n_j = n // tn
    assert rider[2] <= PROJ_ROW_TILES * n_j
    if slab_width:
        slabs = tn // slab_width
        out_shape = jax.ShapeDtypeStruct((n // slab_width, s, slab_width), out_dtype)
        out_spec = pl.BlockSpec((slabs, tm, slab_width), lambda i, j: (j, i, 0))
    else:
        slabs = 0
        out_shape = jax.ShapeDtypeStruct((s, n), out_dtype)
        out_spec = pl.BlockSpec((tm, tn), lambda i, j: (i, j))
    r_in, r_out, r_shape = _rider_specs(rider, layer, n_j)
    return pl.pallas_call(
        functools.partial(_proj_kernel, slabs=slabs, n_chunks=rider[2]),
        grid=(s // tm, n_j),
        in_specs=[pl.BlockSpec((tm, d), lambda i, j: (i, 0)),
                  pl.BlockSpec((None, d, tn), lambda i, j: (layer, 0, j + off)), r_in],
        out_specs=[out_spec, r_out],
        out_shape=[out_shape, r_shape],
        compiler_params=_params("arbitrary", "arbitrary"),
        name=name,
    )(h, w, rider[0])


def _proj_attn_kernel(x_ref, w_ref, cast_in, o_ref, cast_out, scr, *, dil, n_chunks):
    _cast_rider(cast_in, cast_out, n_chunks)
    acc = jnp.dot(x_ref[...], w_ref[...].astype(BF16), preferred_element_type=F32)
    if dil == 1:
        for hh in range(HEADS_PER_GROUP):
            o_ref[hh, 0] = acc[:, hh * LANES:(hh + 1) * LANES].astype(o_ref.dtype)
    else:
        rows = scr.shape[1] // dil
        for hh in range(HEADS_PER_GROUP):
            scr[hh] = acc[:, hh * LANES:(hh + 1) * LANES]
        for hh in range(HEADS_PER_GROUP):
            for c in range(dil):
                o_ref[hh, c] = scr[hh, pl.ds(c, rows, stride=dil), :].astype(o_ref.dtype)


def _proj_attn(h, w, layer, col0, gi, rider):
    s, d = h.shape
    tm = s // PROJ_ROW_TILES
    assert rider[2] <= PROJ_ROW_TILES * 3
    dil = ATTN_GROUPS[gi][1]
    tn = ATTN_OUT_WIDTH
    off = col0 // tn + gi
    n_groups = len(ATTN_GROUPS)
    r_in, r_out, r_shape = _rider_specs(rider, layer, 3)
    return pl.pallas_call(
        functools.partial(_proj_attn_kernel, dil=dil, n_chunks=rider[2]),
        grid=(s // tm, 3),
        in_specs=[pl.BlockSpec((tm, d), lambda i, j: (i, 0)),
                  pl.BlockSpec((None, d, tn), lambda i, j: (layer, 0, off + n_groups * j)), r_in],
        out_specs=[pl.BlockSpec((HEADS_PER_GROUP, dil, tm // dil, LANES), lambda i, j: (j, 0, i, 0)),
                   r_out],
        out_shape=[jax.ShapeDtypeStruct((3 * HEADS_PER_GROUP, dil, s // dil, LANES), BF16), r_shape],
        scratch_shapes=[pltpu.VMEM((HEADS_PER_GROUP, tm, LANES), F32)],
        compiler_params=_params("arbitrary", "arbitrary"),
        name=f"proj_attn_g{gi}",
    )(h, w, rider[0])


def _hgrn_constants():
    c = CHUNK
    t = np.arange(c)[:, None]
    s = np.arange(c)[None, :]
    mats = [s <= t]
    masks = []
    for lvl in range(N_LEVELS):
        m = c >> (lvl + 1)
        p = (t // (2 * m)) * (2 * m) + m - 1
        if m < MXU_LEVEL_BELOW:
            mats.append(((t > p) & (s > p) & (s <= t)) | ((t <= p) & (s > t) & (s <= p)))
        masks.append((t // (2 * m) == s // (2 * m)) & (t % (2 * m) >= m) & (s % (2 * m) < m))
    w = np.stack(mats).astype(np.float32)
    k = np.stack(masks).astype(np.float32)
    w = np.stack([w, w[:, ::-1, ::-1]]).reshape(2, len(mats) * c, c)
    w = np.concatenate([w, w], axis=2)
    k = np.stack([k, k[:, ::-1, ::-1]])
    k = np.concatenate([k, k], axis=3)
    return jnp.asarray(w, BF16), jnp.asarray(k, BF16)


def _block_diag(x):
    zero = jnp.zeros((x.shape[0], LANES), x.dtype)
    return jnp.concatenate([jnp.concatenate([x[:, :LANES], zero], axis=1),
                            jnp.concatenate([zero, x[:, LANES:]], axis=1)], axis=0)


def _hgrn_chunk(q, z, v, lb, wmat, m_ref, st, backward):
    half = 0.5 * (1.0 - lb)
    ct = half * jnp.tanh(0.5 * z)
    kk = half - ct
    logf = jnp.log2(jnp.maximum((lb + half) + ct, MIN_FORGET))
    hi = logf.astype(BF16)
    lo = (logf - hi.astype(F32)).astype(BF16)
    dec = jnp.dot(wmat, jnp.concatenate([hi, lo], axis=0), preferred_element_type=F32)
    b = dec[0:CHUNK]
    total = b[0:1] if backward else b[CHUNK - 1:CHUNK]
    q_in = (q * jnp.exp2(b)).astype(BF16)
    k_out = (kk * jnp.exp2(total - b)).astype(BF16)
    vb = v.astype(BF16)
    qb = q.astype(BF16)
    kb = kk.astype(BF16)
    o = lax.dot_general(q_in, _block_diag(st.astype(BF16)), _NT, preferred_element_type=F32)
    qk = q * kk
    diag = jnp.concatenate(
        [jnp.broadcast_to(jnp.sum(qk[:, :LANES], axis=-1, keepdims=True), (CHUNK, LANES)),
         jnp.broadcast_to(jnp.sum(qk[:, LANES:], axis=-1, keepdims=True), (CHUNK, LANES))], axis=1)
    o = o + diag * v
    a = None
    for lvl in range(N_LEVELS):
        m = CHUNK >> (lvl + 1)
        if m >= MXU_LEVEL_BELOW:
            parts = []
            for j in range(CHUNK // (2 * m)):
                lo_rows = b[2 * m * j:2 * m * j + m]
                hi_rows = b[2 * m * j + m:2 * m * (j + 1)]
                if backward:
                    piv = b[2 * m * j + m:2 * m * j + m + 1]
                    parts += [lo_rows - piv, piv - hi_rows]
                else:
                    piv = b[2 * m * j + m - 1:2 * m * j + m]
                    parts += [piv - lo_rows, hi_rows - piv]
            nl = jnp.concatenate(parts, axis=0)
        else:
            i = 1 + lvl - (N_LEVELS - N_MXU_LEVELS)
            nl = dec[i * CHUNK:(i + 1) * CHUNK]
        el = jnp.exp2(nl).astype(BF16)
        sc = lax.dot_general(qb * el, _block_diag(kb * el), _NT, preferred_element_type=F32)
        sc = sc.astype(BF16) * m_ref[lvl]
        a = sc if a is None else a + sc
    o = o + jnp.dot(a, _block_diag(vb), preferred_element_type=F32)
    v_rows = jnp.concatenate([vb[:, :LANES], vb[:, LANES:]], axis=0)
    st_new = st * jnp.exp2(total) + lax.dot_general(v_rows, _block_diag(k_out), _TN,
                                                    preferred_element_type=F32)
    return o, st_new


def _hgrn_kernel(qf_ref, zf_ref, vf_ref, qb_ref, zb_ref, vb_ref, lb_ref, w_ref, m_ref,
                 of_ref, ob_ref, stf_ref, stb_ref):
    @pl.when(pl.program_id(0) == 0)
    def _():
        stf_ref[...] = jnp.zeros_like(stf_ref)
        stb_ref[...] = jnp.zeros_like(stb_ref)

    n_sub = qf_ref.shape[1] // CHUNK

    def sub_chunk(t, carry):
        rf = pl.ds(pl.multiple_of(t * CHUNK, CHUNK), CHUNK)
        rb = pl.ds(pl.multiple_of((n_sub - 1 - t) * CHUNK, CHUNK), CHUNK)
        for p in range(HGRN_HEADS // 2):
            o, st = _hgrn_chunk(qf_ref[p, rf, :], zf_ref[p, rf, :], vf_ref[p, rf, :], lb_ref[0, p],
                                w_ref[0], m_ref.at[0], stf_ref[p], False)
            of_ref[p, rf, :] = o
            stf_ref[p] = st
            o, st = _hgrn_chunk(qb_ref[p, rb, :], zb_ref[p, rb, :], vb_ref[p, rb, :], lb_ref[1, p],
                                w_ref[1], m_ref.at[1], stb_ref[p], True)
            ob_ref[p, rb, :] = o
            stb_ref[p] = st
        return carry

    lax.fori_loop(0, n_sub, sub_chunk, 0)


def _hgrn(proj_h, lb):
    s = proj_h.shape[1]
    n = s // HGRN_BLOCK
    pairs = HGRN_HEADS // 2
    wmat, masks = _hgrn_constants()
    hb = (pairs, HGRN_BLOCK, 2 * LANES)
    fwd = lambda slab: pl.BlockSpec(hb, lambda c: (slab, c, 0))
    bwd = lambda slab: pl.BlockSpec(hb, lambda c: (slab, n - 1 - c, 0))
    const = lambda a: pl.BlockSpec(a.shape, lambda c: (0,) * a.ndim)
    out_shape = jax.ShapeDtypeStruct((pairs, s, 2 * LANES), F32)
    state = pltpu.VMEM((pairs, LANES, 2 * LANES), F32)
    return pl.pallas_call(
        _hgrn_kernel,
        grid=(n,),
        in_specs=[fwd(0), fwd(1), fwd(3), bwd(0), bwd(2), bwd(3), const(lb), const(wmat), const(masks)],
        out_specs=[fwd(0), bwd(0)],
        out_shape=[out_shape, out_shape],
        scratch_shapes=[state, state],
        compiler_params=_params("arbitrary"),
        name="hgrn2_scan",
    )(proj_h, proj_h, proj_h, proj_h, proj_h, proj_h, lb, wmat, masks)


def _t5_bucket(rel):
    half = REL_BUCKETS // 2
    ret = (rel > 0).astype(np.int32) * half
    n = np.abs(rel)
    max_exact = half // 2
    large = max_exact + (np.log(np.maximum(n, 1) / max_exact)
                         / np.log(REL_MAX_DISTANCE / max_exact)
                         * (half - max_exact)).astype(np.int32)
    large = np.minimum(large, half - 1)
    return (ret + np.where(n < max_exact, n, large)).astype(np.int32)


def _attn_bias(rel_bias_table):
    span = 3 * ATTN_QBLK - 1
    rel = np.arange(span) - (ATTN_QBLK - 1) - ATTN_SIDE
    col = np.arange(2 * ATTN_QBLK)[None, :]
    band = np.abs(col - ATTN_SIDE - np.arange(ATTN_QBLK)[:, None]) <= ATTN_SIDE
    after_start, before_end = col >= ATTN_SIDE, col < 2 * ATTN_QBLK - ATTN_SIDE
    keep = [band, band & after_start, band & before_end, band & after_start & before_end]
    out = []
    for gi, (_, dil) in enumerate(ATTN_GROUPS):
        tab = rel_bias_table[:, gi * HEADS_PER_GROUP:(gi + 1) * HEADS_PER_GROUP].astype(F32)
        onehot = jnp.asarray(_t5_bucket(rel * dil)[:, None] == np.arange(REL_BUCKETS)[None, :], F32)
        vec = jnp.einsum("rb,bh->hr", onehot, tab, precision=lax.Precision.HIGHEST)
        period = span + 2
        vec = jnp.pad(vec, ((0, 0), (0, period - span)))
        flat = jnp.tile(vec, (1, ATTN_QBLK))[:, :ATTN_QBLK * (span + 1)]
        toep = flat.reshape(HEADS_PER_GROUP, ATTN_QBLK, span + 1)[:, :, ATTN_QBLK - 1:3 * ATTN_QBLK - 1]
        out.append(jnp.stack([jnp.where(jnp.asarray(k)[None], toep, NEG_INF) for k in keep], axis=1))
    return jnp.stack(out)


def _attn_kernel(q_ref, kp_ref, km_ref, kn_ref, vp_ref, vm_ref, vn_ref, bias_ref, *rest,
                 rows, dil, cpb, has_prev, final):
    rest = list(rest)
    op_ref, lp_ref = (rest.pop(0), rest.pop(0)) if has_prev else (None, None)
    o_ref = rest.pop(0)
    lse_ref = None if final else rest.pop(0)
    o_s, l_s = rest if rest else (o_ref, lse_ref)
    c = pl.program_id(2)
    i = pl.program_id(1)
    scale = LANES ** -0.5
    n_sub = rows // ATTN_QBLK

    def window(prev_ref, main_ref, next_ref, cc, j):
        lo, hi = j * ATTN_QBLK - ATTN_SIDE, (j + 1) * ATTN_QBLK + ATTN_SIDE
        parts = []
        if lo < 0:
            parts.append(prev_ref[cc])
        parts.append(main_ref[cc, max(lo, 0):min(hi, rows)])
        if hi > rows:
            parts.append(next_ref[cc])
        return parts[0] if len(parts) == 1 else jnp.concatenate(parts, axis=0)

    for cc in range(cpb):
        scores = []
        for j in range(n_sub):
            r0 = j * ATTN_QBLK
            variant = 0
            if j == 0:
                variant = variant + (i == 0).astype(jnp.int32)
            if j == n_sub - 1:
                variant = variant + 2 * (i == pl.num_programs(1) - 1).astype(jnp.int32)
            scores.append(lax.dot_general(q_ref[cc, r0:r0 + ATTN_QBLK],
                                          window(kp_ref, km_ref, kn_ref, cc, j), _NT,
                                          preferred_element_type=F32) * scale + bias_ref[variant])
        for j, s in enumerate(scores):
            r0 = j * ATTN_QBLK
            m = jnp.max(s, axis=-1, keepdims=True)
            p = jnp.exp(s - m)
            l = jnp.sum(p, axis=-1, keepdims=True)
            o = jnp.dot(p.astype(BF16), window(vp_ref, vm_ref, vn_ref, cc, j),
                        preferred_element_type=F32) / l
            lse = jnp.broadcast_to(m + jnp.log(l), (ATTN_QBLK, LANES))
            if dil == 1:
                o_s[r0:r0 + ATTN_QBLK] = o
                l_s[r0:r0 + ATTN_QBLK] = lse
            else:
                start = r0 * dil + c * cpb + cc
                o_s[pl.ds(start, ATTN_QBLK, stride=dil), :] = o
                l_s[pl.ds(start, ATTN_QBLK, stride=dil), :] = lse

    if has_prev:
        @pl.when(c == dil // cpb - 1)
        def _():
            def fold(t, carry):
                r = pl.ds(pl.multiple_of(t * ATTN_FOLD_ROWS, ATTN_FOLD_ROWS), ATTN_FOLD_ROWS)
                lp, lc = lp_ref[r, :], l_s[r, :]
                m = jnp.maximum(lp, lc)
                wp, wc = jnp.exp(lp - m), jnp.exp(lc - m)
                den = wp + wc
                o = (wp * op_ref[r, :] + wc * o_s[r, :]) / den
                o_ref[r, :] = o.astype(o_ref.dtype)
                if not final:
                    lse_ref[r, :] = m + jnp.log(den)
                return carry

            lax.fori_loop(0, rows * dil // ATTN_FOLD_ROWS, fold, 0)


def _attn_group(qkv, bias, gi, prev):
    dil, length = qkv.shape[1], qkv.shape[2]
    s = dil * length
    cpb = min(ATTN_RESIDUES, dil)
    rows = min(ATTN_TILES * ATTN_QBLK // cpb, length)
    nblk = length // rows
    sub = rows // ATTN_SIDE
    last = length // ATTN_SIDE - 1
    hp = HEADS_PER_GROUP
    final = gi == len(ATTN_GROUPS) - 1
    has_prev = prev is not None
    assert has_prev or dil == 1

    def main(kind):
        return pl.BlockSpec((None, cpb, rows, LANES), lambda g, i, c: (kind * hp + g, c, i, 0))

    def prev_blk(kind):
        return pl.BlockSpec((None, cpb, ATTN_SIDE, LANES),
                            lambda g, i, c: (kind * hp + g, c, jnp.maximum(i * sub - 1, 0), 0))

    def next_blk(kind):
        return pl.BlockSpec((None, cpb, ATTN_SIDE, LANES),
                            lambda g, i, c: (kind * hp + g, c, jnp.minimum((i + 1) * sub, last), 0))

    run_spec = pl.BlockSpec((None, rows * dil, LANES), lambda g, i, c: (g, i, 0))
    run_shape = jax.ShapeDtypeStruct((hp, s, LANES), F32)
    in_specs = [main(0), prev_blk(1), main(1), next_blk(1), prev_blk(2), main(2), next_blk(2),
                pl.BlockSpec((None, None, 4, ATTN_QBLK, 2 * ATTN_QBLK), lambda g, i, c: (gi, g, 0, 0, 0))]
    args = [qkv] * 7 + [bias]
    if has_prev:
        in_specs += [run_spec, run_spec]
        args += list(prev)
    if final:
        out_specs = pl.BlockSpec((rows * dil, LANES), lambda g, i, c: (i, g))
        out_shape = jax.ShapeDtypeStruct((s, hp * LANES), BF16)
    else:
        out_specs = [run_spec, run_spec]
        out_shape = [run_shape, run_shape]
    scratch = []
    if has_prev:
        scratch = [pltpu.VMEM((rows * dil, LANES), F32), pltpu.VMEM((rows * dil, LANES), F32)]
    return pl.pallas_call(
        functools.partial(_attn_kernel, rows=rows, dil=dil, cpb=cpb, has_prev=has_prev, final=final),
        grid=(hp, nblk, dil // cpb),
        in_specs=in_specs,
        out_specs=out_specs,
        out_shape=out_shape,
        scratch_shapes=scratch,
        compiler_params=_params("parallel", "arbitrary", "arbitrary"),
        name=f"dilated_attn_g{gi}",
    )(*args)


def _merge_kernel(of_ref, ob_ref, g_ref, oa_ref, gate_ref, x_ref, pa_ref, pb_ref, wo_ref, hw_ref,
                  nw_ref, xo_ref, ho_ref, hs):
    for h in range(HGRN_HEADS):
        lanes = slice((h % 2) * LANES, (h % 2 + 1) * LANES)
        o = of_ref[h // 2, :, lanes] + ob_ref[h // 2, :, lanes]
        o = o * lax.rsqrt(jnp.mean(o * o, axis=-1, keepdims=True) + NORM_EPS)
        g = g_ref[h // 2, :, lanes]
        hs[:, h * LANES:(h + 1) * LANES] = (o * hw_ref[h] * (g * _sigmoid(g))).astype(BF16)
    d = x_ref.shape[1]
    a = jnp.dot(hs[...], pa_ref[...], preferred_element_type=F32)
    b = jnp.dot(oa_ref[...], pb_ref[...], preferred_element_type=F32)
    merged = _sigmoid(gate_ref[:, :d]) * a.astype(BF16) + _sigmoid(gate_ref[:, d:]) * b.astype(BF16)
    xn = x_ref[...] + jnp.dot(merged, wo_ref[...], preferred_element_type=F32)
    xo_ref[...] = xn
    ho_ref[...] = _rms(xn, nw_ref[...]).astype(ho_ref.dtype)


def _merge(o_hgrn, proj_h, o_attn, gates, x, pa, pb, wo, hgrn_norm_w, norm_w, tm=256):
    s, d = x.shape
    pairs = HGRN_HEADS // 2
    hspec = pl.BlockSpec((pairs, tm, 2 * LANES), lambda i: (0, i, 0))

    def wspec(rows):
        return pl.BlockSpec((rows, d), lambda i: (0, 0))

    return pl.pallas_call(
        _merge_kernel,
        grid=(s // tm,),
        in_specs=[hspec, hspec,
                  pl.BlockSpec((pairs, tm, 2 * LANES), lambda i: (4, i, 0)),
                  pl.BlockSpec((tm, ATTN_OUT_WIDTH), lambda i: (i, 0)),
                  pl.BlockSpec((tm, 2 * d), lambda i: (i, 0)),
                  pl.BlockSpec((tm, d), lambda i: (i, 0)),
                  wspec(HGRN_WIDTH), wspec(ATTN_OUT_WIDTH), wspec(d),
                  pl.BlockSpec((HGRN_HEADS, 1, LANES), lambda i: (0, 0, 0)),
                  pl.BlockSpec((1, d), lambda i: (0, 0))],
        out_specs=[pl.BlockSpec((tm, d), lambda i: (i, 0)),
                   pl.BlockSpec((tm, d), lambda i: (i, 0))],
        out_shape=[jax.ShapeDtypeStruct((s, d), F32), jax.ShapeDtypeStruct((s, d), BF16)],
        scratch_shapes=[pltpu.VMEM((tm, HGRN_WIDTH), BF16)],
        compiler_params=_params("parallel"),
        name="merge_out_proj",
    )(o_hgrn[0], o_hgrn[1], proj_h, o_attn, gates, x, pa, pb, wo,
      hgrn_norm_w.reshape(HGRN_HEADS, 1, LANES), norm_w.reshape(1, d))


def _mlp_kernel(h_ref, x_ref, wu_ref, wd_ref, nw_ref, xo_ref, ho_ref, acc):
    f = pl.program_id(1)

    @pl.when(f == 0)
    def _():
        acc[...] = x_ref[...]

    u = jnp.maximum(jnp.dot(h_ref[...], wu_ref[...], preferred_element_type=F32), 0.0)
    acc[...] += jnp.dot((u * u).astype(BF16), wd_ref[...], preferred_element_type=F32)

    @pl.when(f == pl.num_programs(1) - 1)
    def _():
        xn = acc[...]
        xo_ref[...] = xn
        ho_ref[...] = _rms(xn, nw_ref[...]).astype(ho_ref.dtype)


def _mlp(h, x, wu, wd, norm_w, h_dtype, tm=512, tf=1024):
    s, d = x.shape
    ff = wu.shape[1]
    return pl.pallas_call(
        _mlp_kernel,
        grid=(s // tm, ff // tf),
        in_specs=[pl.BlockSpec((tm, d), lambda i, f: (i, 0)),
                  pl.BlockSpec((tm, d), lambda i, f: (i, 0)),
                  pl.BlockSpec((d, tf), lambda i, f: (0, f)),
                  pl.BlockSpec((tf, d), lambda i, f: (f, 0)),
                  pl.BlockSpec((1, d), lambda i, f: (0, 0))],
        out_specs=[pl.BlockSpec((tm, d), lambda i, f: (i, 0)),
                   pl.BlockSpec((tm, d), lambda i, f: (i, 0))],
        out_shape=[jax.ShapeDtypeStruct((s, d), F32), jax.ShapeDtypeStruct((s, d), h_dtype)],
        scratch_shapes=[pltpu.VMEM((tm, d), F32)],
        compiler_params=_params("parallel", "arbitrary"),
        name="mlp_relu2",
    )(h, x, wu, wd, norm_w.reshape(1, d))


def _lower_bounds(logits):
    p = jax.nn.softmax(logits.astype(F32), axis=0)
    return jnp.cumsum(p, axis=0) - p[0:1]


def kernel(x, w_in, hgrn_lb_fwd, hgrn_lb_bwd, hgrn_norm_w, rel_bias_table, w_branch_hgrn,
           w_branch_attn, w_out, norm_mix_w, norm_mlp_w, w_up, w_down, final_norm_w):
    batch, s, d = x.shape
    depth = w_in.shape[0]
    assert batch == 1
    x = x.reshape(s, d)
    lb = jnp.stack([_lower_bounds(hgrn_lb_fwd), _lower_bounds(hgrn_lb_bwd)], axis=1)
    lb = lb.reshape(-1, 2, HGRN_HEADS // 2, 1, 2 * LANES)
    bias = _attn_bias(rel_bias_table)
    n_h = 5 * HGRN_WIDTH
    n_a = 3 * ATTN_WIDTH
    cast_rows = 256
    h = _norm(x, norm_mix_w[0])
    for l in range(depth):
        proj_h, wd_b = _proj(h, w_in, l, 0, n_h, F32, 2 * LANES, "proj_hgrn",
                             (w_down, 0, w_down.shape[1] // cast_rows))
        gates, wu_b = _proj(h, w_in, l, n_h + n_a, 2 * d, BF16, 0, "proj_gates",
                            (w_up, 1, w_up.shape[2] // cast_rows))
        o_hgrn = _hgrn(proj_h, lb[l])
        attn = None
        cast = []
        for gi, wt in enumerate((w_out, w_branch_hgrn, w_branch_attn)):
            qkv, wt_b = _proj_attn(h, w_in, l, n_h, gi, (wt, 0, wt.shape[1] // cast_rows))
            attn = _attn_group(qkv, bias, gi, attn)
            cast.append(wt_b)
        wo_b, pa_b, pb_b = cast
        x, h2 = _merge(o_hgrn, proj_h, attn, gates, x, pa_b, pb_b, wo_b, hgrn_norm_w[l],
                       norm_mlp_w[l])
        last = l == depth - 1
        x, h = _mlp(h2, x, wu_b, wd_b, final_norm_w if last else norm_mix_w[l + 1],
                    F32 if last else BF16)
    return h.reshape(batch, s, d)
```

```python
import functools

import jax
import jax.numpy as jnp
import numpy as np
from jax import lax
from jax.experimental import pallas as pl
from jax.experimental.pallas import tpu as pltpu

F32 = jnp.float32
BF16 = jnp.bfloat16

LANES = 128
HGRN_HEADS = 8
HGRN_WIDTH = HGRN_HEADS * LANES
CHUNK = 128
HGRN_BLOCK = 2 * CHUNK
N_LEVELS = 7
MXU_LEVEL_BELOW = 8
N_MXU_LEVELS = 3
MIN_FORGET = 1e-30
ATTN_GROUPS = ((128, 1), (512, 4), (2048, 16))
HEADS_PER_GROUP = 4
N_ATTN_HEADS = HEADS_PER_GROUP * len(ATTN_GROUPS)
ATTN_WIDTH = N_ATTN_HEADS * LANES
ATTN_OUT_WIDTH = HEADS_PER_GROUP * LANES
ATTN_QBLK = 128
ATTN_SIDE = 64
ATTN_TILES = 16
ATTN_RESIDUES = 4
ATTN_FOLD_ROWS = 256
REL_BUCKETS = 32
REL_MAX_DISTANCE = 1024
NORM_EPS = 1e-6
NEG_INF = -1e30
PROJ_ROW_TILES = 4
VMEM_LIMIT = 56 * 1024 * 1024

_NT = (((1,), (1,)), ((), ()))
_TN = (((0,), (0,)), ((), ()))


def _params(*sem):
    return pltpu.CompilerParams(dimension_semantics=sem, vmem_limit_bytes=VMEM_LIMIT)


def _rms(x, w):
    return x * lax.rsqrt(jnp.mean(x * x, axis=-1, keepdims=True) + NORM_EPS) * w


def _sigmoid(x):
    return 0.5 * jnp.tanh(0.5 * x) + 0.5


def _norm_kernel(x_ref, w_ref, o_ref):
    o_ref[...] = _rms(x_ref[...], w_ref[...]).astype(o_ref.dtype)


def _norm(x, w, tm=512):
    s, d = x.shape
    return pl.pallas_call(
        _norm_kernel,
        grid=(s // tm,),
        in_specs=[pl.BlockSpec((tm, d), lambda i: (i, 0)),
                  pl.BlockSpec((1, d), lambda i: (0, 0))],
        out_specs=pl.BlockSpec((tm, d), lambda i: (i, 0)),
        out_shape=jax.ShapeDtypeStruct((s, d), BF16),
        compiler_params=_params("parallel"),
        name="rmsnorm_in",
    )(x, w.reshape(1, d))


def _cast_kernel(w_ref, o_ref):
    o_ref[...] = w_ref[...].astype(o_ref.dtype)


def _cast_layer(w, layer, rows=128):
    r, c = w.shape[1:]
    return pl.pallas_call(
        _cast_kernel,
        grid=(r // rows,),
        in_specs=[pl.BlockSpec((None, rows, c), lambda i: (layer, i, 0))],
        out_specs=pl.BlockSpec((rows, c), lambda i: (i, 0)),
        out_shape=jax.ShapeDtypeStruct((r, c), BF16),
        compiler_params=_params("parallel"),
        name="cast_w_in",
    )(w)


def _cast_rider(cast_in, cast_out, n_chunks):
    step = pl.program_id(0) * pl.num_programs(1) + pl.program_id(1)

    @pl.when(step < n_chunks)
    def _():
        cast_out[...] = cast_in[...].astype(cast_out.dtype)


def _rider_specs(rider, layer, n_j):
    w, axis, n_chunks = rider
    r, c = w.shape[1:]

    def chunk(i, j):
        return jnp.minimum(i * n_j + j, n_chunks - 1)

    if axis == 0:
        blk = (r // n_chunks, c)
        in_spec = pl.BlockSpec((None,) + blk, lambda i, j: (layer, chunk(i, j), 0))
        out_spec = pl.BlockSpec(blk, lambda i, j: (chunk(i, j), 0))
    else:
        blk = (r, c // n_chunks)
        in_spec = pl.BlockSpec((None,) + blk, lambda i, j: (layer, 0, chunk(i, j)))
        out_spec = pl.BlockSpec(blk, lambda i, j: (0, chunk(i, j)))
    return in_spec, out_spec, jax.ShapeDtypeStruct((r, c), BF16)


def _proj_kernel(x_ref, w_ref, cast_in, o_ref, cast_out, *, slabs, n_chunks):
    _cast_rider(cast_in, cast_out, n_chunks)
    acc = jnp.dot(x_ref[...], w_ref[...], preferred_element_type=F32)
    if slabs:
        width = o_ref.shape[-1]
        for j in range(slabs):
            o_ref[j] = acc[:, j * width:(j + 1) * width].astype(o_ref.dtype)
    else:
        o_ref[...] = acc.astype(o_ref.dtype)


def _proj(h, w, layer, col0, n, out_dtype, slab_width, name, rider, tn=512):
    s, d = h.shape
    tm = s // PROJ_ROW_TILES
    off = col0 // tn
    n_j = n // tn
    assert rider[2] <= PROJ_ROW_TILES * n_j
    if slab_width:
        slabs = tn // slab_width
        out_shape = jax.ShapeDtypeStruct((n // slab_width, s, slab_width), out_dtype)
        out_spec = pl.BlockSpec((slabs, tm, slab_width), lambda i, j: (j, i, 0))
    else:
        slabs = 0
        out_shape = jax.ShapeDtypeStruct((s, n), out_dtype)
        out_spec = pl.BlockSpec((tm, tn), lambda i, j: (i, j))
    r_in, r_out, r_shape = _rider_specs(rider, layer, n_j)
    return pl.pallas_call(
        functools.partial(_proj_kernel, slabs=slabs, n_chunks=rider[2]),
        grid=(s // tm, n_j),
        in_specs=[pl.BlockSpec((tm, d), lambda i, j: (i, 0)),
                  pl.BlockSpec((d, tn), lambda i, j: (0, j + off)), r_in],
        out_specs=[out_spec, r_out],
        out_shape=[out_shape, r_shape],
        compiler_params=_params("arbitrary", "arbitrary"),
        name=name,
    )(h, w, rider[0])


def _proj_attn_kernel(x_ref, w_ref, cast_in, o_ref, cast_out, scr, *, dil, n_chunks):
    _cast_rider(cast_in, cast_out, n_chunks)
    acc = jnp.dot(x_ref[...], w_ref[...], preferred_element_type=F32)
    if dil == 1:
        for hh in range(HEADS_PER_GROUP):
            o_ref[hh, 0] = acc[:, hh * LANES:(hh + 1) * LANES].astype(o_ref.dtype)
    else:
        rows = scr.shape[1] // dil
        for hh in range(HEADS_PER_GROUP):
            scr[hh] = acc[:, hh * LANES:(hh + 1) * LANES]
        for hh in range(HEADS_PER_GROUP):
            for c in range(dil):
                o_ref[hh, c] = scr[hh, pl.ds(c, rows, stride=dil), :].astype(o_ref.dtype)


def _proj_attn(h, w, layer, col0, gi, rider):
    s, d = h.shape
    tm = s // PROJ_ROW_TILES
    assert rider[2] <= PROJ_ROW_TILES * 3
    dil = ATTN_GROUPS[gi][1]
    tn = ATTN_OUT_WIDTH
    off = col0 // tn + gi
    n_groups = len(ATTN_GROUPS)
    r_in, r_out, r_shape = _rider_specs(rider, layer, 3)
    return pl.pallas_call(
        functools.partial(_proj_attn_kernel, dil=dil, n_chunks=rider[2]),
        grid=(s // tm, 3),
        in_specs=[pl.BlockSpec((tm, d), lambda i, j: (i, 0)),
                  pl.BlockSpec((d, tn), lambda i, j: (0, off + n_groups * j)), r_in],
        out_specs=[pl.BlockSpec((HEADS_PER_GROUP, dil, tm // dil, LANES), lambda i, j: (j, 0, i, 0)),
                   r_out],
        out_shape=[jax.ShapeDtypeStruct((3 * HEADS_PER_GROUP, dil, s // dil, LANES), BF16), r_shape],
        scratch_shapes=[pltpu.VMEM((HEADS_PER_GROUP, tm, LANES), F32)],
        compiler_params=_params("arbitrary", "arbitrary"),
        name=f"proj_attn_g{gi}",
    )(h, w, rider[0])


def _hgrn_constants():
    c = CHUNK
    t = np.arange(c)[:, None]
    s = np.arange(c)[None, :]
    mats = [s <= t]
    masks = []
    for lvl in range(N_LEVELS):
        m = c >> (lvl + 1)
        p = (t // (2 * m)) * (2 * m) + m - 1
        if m < MXU_LEVEL_BELOW:
            mats.append(((t > p) & (s > p) & (s <= t)) | ((t <= p) & (s > t) & (s <= p)))
        masks.append((t // (2 * m) == s // (2 * m)) & (t % (2 * m) >= m) & (s % (2 * m) < m))
    w = np.stack(mats).astype(np.float32)
    k = np.stack(masks).astype(np.float32)
    w = np.stack([w, w[:, ::-1, ::-1]]).reshape(2, len(mats) * c, c)
    w = np.concatenate([w, w], axis=2)
    k = np.stack([k, k[:, ::-1, ::-1]])
    k = np.concatenate([k, k], axis=3)
    return jnp.asarray(w, BF16), jnp.asarray(k, BF16)


def _block_diag(x):
    zero = jnp.zeros((x.shape[0], LANES), x.dtype)
    return jnp.concatenate([jnp.concatenate([x[:, :LANES], zero], axis=1),
                            jnp.concatenate([zero, x[:, LANES:]], axis=1)], axis=0)


def _hgrn_chunk(q, z, v, lb, wmat, m_ref, st, backward):
    half = 0.5 * (1.0 - lb)
    ct = half * jnp.tanh(0.5 * z)
    kk = half - ct
    logf = jnp.log2(jnp.maximum((lb + half) + ct, MIN_FORGET))
    hi = logf.astype(BF16)
    lo = (logf - hi.astype(F32)).astype(BF16)
    dec = jnp.dot(wmat, jnp.concatenate([hi, lo], axis=0), preferred_element_type=F32)
    b = dec[0:CHUNK]
    total = b[0:1] if backward else b[CHUNK - 1:CHUNK]
    q_in = (q * jnp.exp2(b)).astype(BF16)
    k_out = (kk * jnp.exp2(total - b)).astype(BF16)
    vb = v.astype(BF16)
    qb = q.astype(BF16)
    kb = kk.astype(BF16)
    o = lax.dot_general(q_in, _block_diag(st.astype(BF16)), _NT, preferred_element_type=F32)
    qk = q * kk
    diag = jnp.concatenate(
        [jnp.broadcast_to(jnp.sum(qk[:, :LANES], axis=-1, keepdims=True), (CHUNK, LANES)),
         jnp.broadcast_to(jnp.sum(qk[:, LANES:], axis=-1, keepdims=True), (CHUNK, LANES))], axis=1)
    o = o + diag * v
    a = None
    for lvl in range(N_LEVELS):
        m = CHUNK >> (lvl + 1)
        if m >= MXU_LEVEL_BELOW:
            parts = []
            for j in range(CHUNK // (2 * m)):
                lo_rows = b[2 * m * j:2 * m * j + m]
                hi_rows = b[2 * m * j + m:2 * m * (j + 1)]
                if backward:
                    piv = b[2 * m * j + m:2 * m * j + m + 1]
                    parts += [lo_rows - piv, piv - hi_rows]
                else:
                    piv = b[2 * m * j + m - 1:2 * m * j + m]
                    parts += [piv - lo_rows, hi_rows - piv]
            nl = jnp.concatenate(parts, axis=0)
        else:
            i = 1 + lvl - (N_LEVELS - N_MXU_LEVELS)
            nl = dec[i * CHUNK:(i + 1) * CHUNK]
        el = jnp.exp2(nl).astype(BF16)
        sc = lax.dot_general(qb * el, _block_diag(kb * el), _NT, preferred_element_type=F32)
        sc = sc.astype(BF16) * m_ref[lvl]
        a = sc if a is None else a + sc
    o = o + jnp.dot(a, _block_diag(vb), preferred_element_type=F32)
    v_rows = jnp.concatenate([vb[:, :LANES], vb[:, LANES:]], axis=0)
    st_new = st * jnp.exp2(total) + lax.dot_general(v_rows, _block_diag(k_out), _TN,
                                                    preferred_element_type=F32)
    return o, st_new


def _hgrn_kernel(qf_ref, zf_ref, vf_ref, qb_ref, zb_ref, vb_ref, lb_ref, w_ref, m_ref, *rest):
    if len(rest) == 6:
        cast_in, of_ref, ob_ref, cast_out, stf_ref, stb_ref = rest
        cast_out[...] = cast_in[...].astype(cast_out.dtype)
    else:
        of_ref, ob_ref, stf_ref, stb_ref = rest

    @pl.when(pl.program_id(0) == 0)
    def _():
        stf_ref[...] = jnp.zeros_like(stf_ref)
        stb_ref[...] = jnp.zeros_like(stb_ref)

    n_sub = qf_ref.shape[1] // CHUNK

    def sub_chunk(t, carry):
        rf = pl.ds(pl.multiple_of(t * CHUNK, CHUNK), CHUNK)
        rb = pl.ds(pl.multiple_of((n_sub - 1 - t) * CHUNK, CHUNK), CHUNK)
        for p in range(HGRN_HEADS // 2):
            o, st = _hgrn_chunk(qf_ref[p, rf, :], zf_ref[p, rf, :], vf_ref[p, rf, :], lb_ref[0, p],
                                w_ref[0], m_ref.at[0], stf_ref[p], False)
            of_ref[p, rf, :] = o
            stf_ref[p] = st
            o, st = _hgrn_chunk(qb_ref[p, rb, :], zb_ref[p, rb, :], vb_ref[p, rb, :], lb_ref[1, p],
                                w_ref[1], m_ref.at[1], stb_ref[p], True)
            ob_ref[p, rb, :] = o
            stb_ref[p] = st
        return carry

    lax.fori_loop(0, n_sub, sub_chunk, 0)


def _hgrn(proj_h, lb, w_next, layer_next):
    s = proj_h.shape[1]
    n = s // HGRN_BLOCK
    pairs = HGRN_HEADS // 2
    wmat, masks = _hgrn_constants()
    hb = (pairs, HGRN_BLOCK, 2 * LANES)
    fwd = lambda slab: pl.BlockSpec(hb, lambda c: (slab, c, 0))
    bwd = lambda slab: pl.BlockSpec(hb, lambda c: (slab, n - 1 - c, 0))
    const = lambda a: pl.BlockSpec(a.shape, lambda c: (0,) * a.ndim)
    out_shape = jax.ShapeDtypeStruct((pairs, s, 2 * LANES), F32)
    state = pltpu.VMEM((pairs, LANES, 2 * LANES), F32)
    in_specs = [fwd(0), fwd(1), fwd(3), bwd(0), bwd(2), bwd(3), const(lb), const(wmat), const(masks)]
    args = [proj_h] * 6 + [lb, wmat, masks]
    out_specs = [fwd(0), bwd(0)]
    out_shapes = [out_shape, out_shape]
    if layer_next is not None:
        r, cols = w_next.shape[1:]
        in_specs.append(pl.BlockSpec((None, r // n, cols), lambda c: (layer_next, c, 0)))
        args.append(w_next)
        out_specs.append(pl.BlockSpec((r // n, cols), lambda c: (c, 0)))
        out_shapes.append(jax.ShapeDtypeStruct((r, cols), BF16))
    return pl.pallas_call(
        _hgrn_kernel,
        grid=(n,),
        in_specs=in_specs,
        out_specs=out_specs,
        out_shape=out_shapes,
        scratch_shapes=[state, state],
        compiler_params=_params("arbitrary"),
        name="hgrn2_scan",
    )(*args)


def _t5_bucket(rel):
    half = REL_BUCKETS // 2
    ret = (rel > 0).astype(np.int32) * half
    n = np.abs(rel)
    max_exact = half // 2
    large = max_exact + (np.log(np.maximum(n, 1) / max_exact)
                         / np.log(REL_MAX_DISTANCE / max_exact)
                         * (half - max_exact)).astype(np.int32)
    large = np.minimum(large, half - 1)
    return (ret + np.where(n < max_exact, n, large)).astype(np.int32)


def _attn_bias(rel_bias_table):
    span = 3 * ATTN_QBLK - 1
    rel = np.arange(span) - (ATTN_QBLK - 1) - ATTN_SIDE
    col = np.arange(2 * ATTN_QBLK)[None, :]
    band = np.abs(col - ATTN_SIDE - np.arange(ATTN_QBLK)[:, None]) <= ATTN_SIDE
    after_start, before_end = col >= ATTN_SIDE, col < 2 * ATTN_QBLK - ATTN_SIDE
    keep = [band, band & after_start, band & before_end, band & after_start & before_end]
    out = []
    for gi, (_, dil) in enumerate(ATTN_GROUPS):
        tab = rel_bias_table[:, gi * HEADS_PER_GROUP:(gi + 1) * HEADS_PER_GROUP].astype(F32)
        onehot = jnp.asarray(_t5_bucket(rel * dil)[:, None] == np.arange(REL_BUCKETS)[None, :], F32)
        vec = jnp.einsum("rb,bh->hr", onehot, tab, precision=lax.Precision.HIGHEST)
        period = span + 2
        vec = jnp.pad(vec, ((0, 0), (0, period - span)))
        flat = jnp.tile(vec, (1, ATTN_QBLK))[:, :ATTN_QBLK * (span + 1)]
        toep = flat.reshape(HEADS_PER_GROUP, ATTN_QBLK, span + 1)[:, :, ATTN_QBLK - 1:3 * ATTN_QBLK - 1]
        out.append(jnp.stack([jnp.where(jnp.asarray(k)[None], toep, NEG_INF) for k in keep], axis=1))
    return jnp.stack(out)


def _attn_kernel(q_ref, kp_ref, km_ref, kn_ref, vp_ref, vm_ref, vn_ref, bias_ref, *rest,
                 rows, dil, cpb, has_prev, final):
    rest = list(rest)
    op_ref, lp_ref = (rest.pop(0), rest.pop(0)) if has_prev else (None, None)
    o_ref = rest.pop(0)
    lse_ref = None if final else rest.pop(0)
    o_s, l_s = rest if rest else (o_ref, lse_ref)
    c = pl.program_id(2)
    i = pl.program_id(1)
    scale = LANES ** -0.5
    n_sub = rows // ATTN_QBLK

    def window(prev_ref, main_ref, next_ref, cc, j):
        lo, hi = j * ATTN_QBLK - ATTN_SIDE, (j + 1) * ATTN_QBLK + ATTN_SIDE
        parts = []
        if lo < 0:
            parts.append(prev_ref[cc])
        parts.append(main_ref[cc, max(lo, 0):min(hi, rows)])
        if hi > rows:
            parts.append(next_ref[cc])
        return parts[0] if len(parts) == 1 else jnp.concatenate(parts, axis=0)

    for cc in range(cpb):
        scores = []
        for j in range(n_sub):
            r0 = j * ATTN_QBLK
            variant = 0
            if j == 0:
                variant = variant + (i == 0).astype(jnp.int32)
            if j == n_sub - 1:
                variant = variant + 2 * (i == pl.num_programs(1) - 1).astype(jnp.int32)
            scores.append(lax.dot_general(q_ref[cc, r0:r0 + ATTN_QBLK],
                                          window(kp_ref, km_ref, kn_ref, cc, j), _NT,
                                          preferred_element_type=F32) * scale + bias_ref[variant])
        for j, s in enumerate(scores):
            r0 = j * ATTN_QBLK
            m = jnp.max(s, axis=-1, keepdims=True)
            p = jnp.exp(s - m)
            l = jnp.sum(p, axis=-1, keepdims=True)
            o = jnp.dot(p.astype(BF16), window(vp_ref, vm_ref, vn_ref, cc, j),
                        preferred_element_type=F32) / l
            lse = jnp.broadcast_to(m + jnp.log(l), (ATTN_QBLK, LANES))
            if dil == 1:
                o_s[r0:r0 + ATTN_QBLK] = o
                l_s[r0:r0 + ATTN_QBLK] = lse
            else:
                start = r0 * dil + c * cpb + cc
                o_s[pl.ds(start, ATTN_QBLK, stride=dil), :] = o
                l_s[pl.ds(start, ATTN_QBLK, stride=dil), :] = lse

    if has_prev:
        @pl.when(c == dil // cpb - 1)
        def _():
            def fold(t, carry):
                r = pl.ds(pl.multiple_of(t * ATTN_FOLD_ROWS, ATTN_FOLD_ROWS), ATTN_FOLD_ROWS)
                lp, lc = lp_ref[r, :], l_s[r, :]
                m = jnp.maximum(lp, lc)
                wp, wc = jnp.exp(lp - m), jnp.exp(lc - m)
                den = wp + wc
                o = (wp * op_ref[r, :] + wc * o_s[r, :]) / den
                o_ref[r, :] = o.astype(o_ref.dtype)
                if not final:
                    lse_ref[r, :] = m + jnp.log(den)
                return carry

            lax.fori_loop(0, rows * dil // ATTN_FOLD_ROWS, fold, 0)


def _attn_group(qkv, bias, gi, prev):
    dil, length = qkv.shape[1], qkv.shape[2]
    s = dil * length
    cpb = min(ATTN_RESIDUES, dil)
    rows = min(ATTN_TILES * ATTN_QBLK // cpb, length)
    nblk = length // rows
    sub = rows // ATTN_SIDE
    last = length // ATTN_SIDE - 1
    hp = HEADS_PER_GROUP
    final = gi == len(ATTN_GROUPS) - 1
    has_prev = prev is not None
    assert has_prev or dil == 1

    def main(kind):
        return pl.BlockSpec((None, cpb, rows, LANES), lambda g, i, c: (kind * hp + g, c, i, 0))

    def prev_blk(kind):
        return pl.BlockSpec((None, cpb, ATTN_SIDE, LANES),
                            lambda g, i, c: (kind * hp + g, c, jnp.maximum(i * sub - 1, 0), 0))

    def next_blk(kind):
        return pl.BlockSpec((None, cpb, ATTN_SIDE, LANES),
                            lambda g, i, c: (kind * hp + g, c, jnp.minimum((i + 1) * sub, last), 0))

    run_spec = pl.BlockSpec((None, rows * dil, LANES), lambda g, i, c: (g, i, 0))
    run_shape = jax.ShapeDtypeStruct((hp, s, LANES), F32)
    in_specs = [main(0), prev_blk(1), main(1), next_blk(1), prev_blk(2), main(2), next_blk(2),
                pl.BlockSpec((None, None, 4, ATTN_QBLK, 2 * ATTN_QBLK), lambda g, i, c: (gi, g, 0, 0, 0))]
    args = [qkv] * 7 + [bias]
    if has_prev:
        in_specs += [run_spec, run_spec]
        args += list(prev)
    if final:
        out_specs = pl.BlockSpec((rows * dil, LANES), lambda g, i, c: (i, g))
        out_shape = jax.ShapeDtypeStruct((s, hp * LANES), BF16)
    else:
        out_specs = [run_spec, run_spec]
        out_shape = [run_shape, run_shape]
    scratch = []
    if has_prev:
        scratch = [pltpu.VMEM((rows * dil, LANES), F32), pltpu.VMEM((rows * dil, LANES), F32)]
    return pl.pallas_call(
        functools.partial(_attn_kernel, rows=rows, dil=dil, cpb=cpb, has_prev=has_prev, final=final),
        grid=(hp, nblk, dil // cpb),
        in_specs=in_specs,
        out_specs=out_specs,
        out_shape=out_shape,
        scratch_shapes=scratch,
        compiler_params=_params("parallel", "arbitrary", "arbitrary"),
        name=f"dilated_attn_g{gi}",
    )(*args)


def _merge_kernel(of_ref, ob_ref, g_ref, oa_ref, gate_ref, x_ref, pa_ref, pb_ref, wo_ref, hw_ref,
                  nw_ref, xo_ref, ho_ref, hs):
    for h in range(HGRN_HEADS):
        lanes = slice((h % 2) * LANES, (h % 2 + 1) * LANES)
        o = of_ref[h // 2, :, lanes] + ob_ref[h // 2, :, lanes]
        o = o * lax.rsqrt(jnp.mean(o * o, axis=-1, keepdims=True) + NORM_EPS)
        g = g_ref[h // 2, :, lanes]
        hs[:, h * LANES:(h + 1) * LANES] = (o * hw_ref[h] * (g * _sigmoid(g))).astype(BF16)
    d = x_ref.shape[1]
    a = jnp.dot(hs[...], pa_ref[...], preferred_element_type=F32)
    b = jnp.dot(oa_ref[...], pb_ref[...], preferred_element_type=F32)
    merged = _sigmoid(gate_ref[:, :d]) * a.astype(BF16) + _sigmoid(gate_ref[:, d:]) * b.astype(BF16)
    xn = x_ref[...] + jnp.dot(merged, wo_ref[...], preferred_element_type=F32)
    xo_ref[...] = xn
    ho_ref[...] = _rms(xn, nw_ref[...]).astype(ho_ref.dtype)


def _merge(o_hgrn, proj_h, o_attn, gates, x, pa, pb, wo, hgrn_norm_w, norm_w, tm=256):
    s, d = x.shape
    pairs = HGRN_HEADS // 2
    hspec = pl.BlockSpec((pairs, tm, 2 * LANES), lambda i: (0, i, 0))

    def wspec(rows):
        return pl.BlockSpec((rows, d), lambda i: (0, 0))

    return pl.pallas_call(
        _merge_kernel,
        grid=(s // tm,),
        in_specs=[hspec, hspec,
                  pl.BlockSpec((pairs, tm, 2 * LANES), lambda i: (4, i, 0)),
                  pl.BlockSpec((tm, ATTN_OUT_WIDTH), lambda i: (i, 0)),
                  pl.BlockSpec((tm, 2 * d), lambda i: (i, 0)),
                  pl.BlockSpec((tm, d), lambda i: (i, 0)),
                  wspec(HGRN_WIDTH), wspec(ATTN_OUT_WIDTH), wspec(d),
                  pl.BlockSpec((HGRN_HEADS, 1, LANES), lambda i: (0, 0, 0)),
                  pl.BlockSpec((1, d), lambda i: (0, 0))],
        out_specs=[pl.BlockSpec((tm, d), lambda i: (i, 0)),
                   pl.BlockSpec((tm, d), lambda i: (i, 0))],
        out_shape=[jax.ShapeDtypeStruct((s, d), F32), jax.ShapeDtypeStruct((s, d), BF16)],
        scratch_shapes=[pltpu.VMEM((tm, HGRN_WIDTH), BF16)],
        compiler_params=_params("parallel"),
        name="merge_out_proj",
    )(o_hgrn[0], o_hgrn[1], proj_h, o_attn, gates, x, pa, pb, wo,
      hgrn_norm_w.reshape(HGRN_HEADS, 1, LANES), norm_w.reshape(1, d))


def _mlp_kernel(h_ref, x_ref, wu_ref, wd_ref, nw_ref, xo_ref, ho_ref, acc):
    f = pl.program_id(1)

    @pl.when(f == 0)
    def _():
        acc[...] = x_ref[...]

    u = jnp.maximum(jnp.dot(h_ref[...], wu_ref[...], preferred_element_type=F32), 0.0)
    acc[...] += jnp.dot((u * u).astype(BF16), wd_ref[...], preferred_element_type=F32)

    @pl.when(f == pl.num_programs(1) - 1)
    def _():
        xn = acc[...]
        xo_ref[...] = xn
        ho_ref[...] = _rms(xn, nw_ref[...]).astype(ho_ref.dtype)


def _mlp(h, x, wu, wd, norm_w, h_dtype, tm=512, tf=1024):
    s, d = x.shape
    ff = wu.shape[1]
    return pl.pallas_call(
        _mlp_kernel,
        grid=(s // tm, ff // tf),
        in_specs=[pl.BlockSpec((tm, d), lambda i, f: (i, 0)),
                  pl.BlockSpec((tm, d), lambda i, f: (i, 0)),
                  pl.BlockSpec((d, tf), lambda i, f: (0, f)),
                  pl.BlockSpec((tf, d), lambda i, f: (f, 0)),
                  pl.BlockSpec((1, d), lambda i, f: (0, 0))],
        out_specs=[pl.BlockSpec((tm, d), lambda i, f: (i, 0)),
                   pl.BlockSpec((tm, d), lambda i, f: (i, 0))],
        out_shape=[jax.ShapeDtypeStruct((s, d), F32), jax.ShapeDtypeStruct((s, d), h_dtype)],
        scratch_shapes=[pltpu.VMEM((tm, d), F32)],
        compiler_params=_params("parallel", "arbitrary"),
        name="mlp_relu2",
    )(h, x, wu, wd, norm_w.reshape(1, d))


def _lower_bounds(logits):
    p = jax.nn.softmax(logits.astype(F32), axis=0)
    return jnp.cumsum(p, axis=0) - p[0:1]


def kernel(x, w_in, hgrn_lb_fwd, hgrn_lb_bwd, hgrn_norm_w, rel_bias_table, w_branch_hgrn,
           w_branch_attn, w_out, norm_mix_w, norm_mlp_w, w_up, w_down, final_norm_w):
    batch, s, d = x.shape
    depth = w_in.shape[0]
    assert batch == 1
    x = x.reshape(s, d)
    lb = jnp.stack([_lower_bounds(hgrn_lb_fwd), _lower_bounds(hgrn_lb_bwd)], axis=1)
    lb = lb.reshape(-1, 2, HGRN_HEADS // 2, 1, 2 * LANES)
    bias = _attn_bias(rel_bias_table)
    n_h = 5 * HGRN_WIDTH
    n_a = 3 * ATTN_WIDTH
    cast_rows = 256
    h = _norm(x, norm_mix_w[0])
    w_in_b = _cast_layer(w_in, 0)
    for l in range(depth):
        proj_h, wd_b = _proj(h, w_in_b, l, 0, n_h, F32, 2 * LANES, "proj_hgrn",
                             (w_down, 0, w_down.shape[1] // cast_rows))
        gates, wu_b = _proj(h, w_in_b, l, n_h + n_a, 2 * d, BF16, 0, "proj_gates",
                            (w_up, 1, w_up.shape[2] // cast_rows))
        last = l == depth - 1
        scan = _hgrn(proj_h, lb[l], w_in, None if last else l + 1)
        o_hgrn = scan[:2]
        attn = None
        cast = []
        for gi, wt in enumerate((w_out, w_branch_hgrn, w_branch_attn)):
            qkv, wt_b = _proj_attn(h, w_in_b, l, n_h, gi, (wt, 0, wt.shape[1] // cast_rows))
            attn = _attn_group(qkv, bias, gi, attn)
            cast.append(wt_b)
        wo_b, pa_b, pb_b = cast
        x, h2 = _merge(o_hgrn, proj_h, attn, gates, x, pa_b, pb_b, wo_b, hgrn_norm_w[l],
                       norm_mlp_w[l])
        x, h = _mlp(h2, x, wu_b, wd_b, final_norm_w if last else norm_mix_w[l + 1],
                    F32 if last else BF16)
        if not last:
            w_in_b = scan[2]
    return h.reshape(batch, s, d)
```

```python
import functools

import jax
import jax.numpy as jnp
import numpy as np
from jax import lax
from jax.experimental import pallas as pl
from jax.experimental.pallas import tpu as pltpu

F32 = jnp.float32
BF16 = jnp.bfloat16

LANES = 128
HGRN_HEADS = 8
HGRN_WIDTH = HGRN_HEADS * LANES
CHUNK = 128
HGRN_BLOCK = 2 * CHUNK
N_LEVELS = 7
MXU_LEVEL_BELOW = 8
N_MXU_LEVELS = 3
MIN_FORGET = 1e-30
ATTN_GROUPS = ((128, 1), (512, 4), (2048, 16))
HEADS_PER_GROUP = 4
N_ATTN_HEADS = HEADS_PER_GROUP * len(ATTN_GROUPS)
ATTN_WIDTH = N_ATTN_HEADS * LANES
ATTN_OUT_WIDTH = HEADS_PER_GROUP * LANES
ATTN_QBLK = 128
ATTN_SIDE = 64
ATTN_TILES = 16
ATTN_RESIDUES = 4
ATTN_FOLD_ROWS = 256
REL_BUCKETS = 32
REL_MAX_DISTANCE = 1024
NORM_EPS = 1e-6
NEG_INF = -1e30
PROJ_ROW_TILES = 4
VMEM_LIMIT = 56 * 1024 * 1024
MLP_VMEM_LIMIT = 62 * 1024 * 1024

_NT = (((1,), (1,)), ((), ()))
_TN = (((0,), (0,)), ((), ()))


def _params(*sem):
    return pltpu.CompilerParams(dimension_semantics=sem, vmem_limit_bytes=VMEM_LIMIT)


def _rms(x, w):
    return x * lax.rsqrt(jnp.mean(x * x, axis=-1, keepdims=True) + NORM_EPS) * w


def _sigmoid(x):
    return 0.5 * jnp.tanh(0.5 * x) + 0.5


def _norm_kernel(x_ref, w_ref, o_ref):
    o_ref[...] = _rms(x_ref[...], w_ref[...]).astype(o_ref.dtype)


def _norm(x, w, tm=512):
    s, d = x.shape
    return pl.pallas_call(
        _norm_kernel,
        grid=(s // tm,),
        in_specs=[pl.BlockSpec((tm, d), lambda i: (i, 0)),
                  pl.BlockSpec((1, d), lambda i: (0, 0))],
        out_specs=pl.BlockSpec((tm, d), lambda i: (i, 0)),
        out_shape=jax.ShapeDtypeStruct((s, d), BF16),
        compiler_params=_params("parallel"),
        name="rmsnorm_in",
    )(x, w.reshape(1, d))


def _cast_rider(cast_in, cast_out, n_chunks):
    step = pl.program_id(0) * pl.num_programs(1) + pl.program_id(1)

    @pl.when(step < n_chunks)
    def _():
        cast_out[...] = cast_in[...].astype(cast_out.dtype)


def _rider_specs(rider, layer, n_j):
    w, axis, n_chunks = rider
    r, c = w.shape[1:]

    def chunk(i, j):
        return jnp.minimum(i * n_j + j, n_chunks - 1)

    if axis == 0:
        blk = (r // n_chunks, c)
        in_spec = pl.BlockSpec((None,) + blk, lambda i, j: (layer, chunk(i, j), 0))
        out_spec = pl.BlockSpec(blk, lambda i, j: (chunk(i, j), 0))
    else:
        blk = (r, c // n_chunks)
        in_spec = pl.BlockSpec((None,) + blk, lambda i, j: (layer, 0, chunk(i, j)))
        out_spec = pl.BlockSpec(blk, lambda i, j: (0, chunk(i, j)))
    return in_spec, out_spec, jax.ShapeDtypeStruct((r, c), BF16)


def _proj_kernel(x_ref, w_ref, cast_in, o_ref, cast_out, *, slabs, n_chunks):
    _cast_rider(cast_in, cast_out, n_chunks)
    acc = jnp.dot(x_ref[...], w_ref[...].astype(BF16), preferred_element_type=F32)
    if slabs:
        width = o_ref.shape[-1]
        for j in range(slabs):
            o_ref[j] = acc[:, j * width:(j + 1) * width].astype(o_ref.dtype)
    else:
        o_ref[...] = acc.astype(o_ref.dtype)


def _proj(h, w, layer, col0, n, out_dtype, slab_width, name, rider, tn=512):
    s, d = h.shape
    tm = s // PROJ_ROW_TILES
    off = col0 // tn
    n_j = n // tn
    assert rider[2] <= PROJ_ROW_TILES * n_j
    if slab_width:
        slabs = tn // slab_width
        out_shape = jax.ShapeDtypeStruct((n // slab_width, s, slab_width), out_dtype)
        out_spec = pl.BlockSpec((slabs, tm, slab_width), lambda i, j: (j, i, 0))
    else:
        slabs = 0
        out_shape = jax.ShapeDtypeStruct((s, n), out_dtype)
        out_spec = pl.BlockSpec((tm, tn), lambda i, j: (i, j))
    r_in, r_out, r_shape = _rider_specs(rider, layer, n_j)
    return pl.pallas_call(
        functools.partial(_proj_kernel, slabs=slabs, n_chunks=rider[2]),
        grid=(s // tm, n_j),
        in_specs=[pl.BlockSpec((tm, d), lambda i, j: (i, 0)),
                  pl.BlockSpec((None, d, tn), lambda i, j: (layer, 0, j + off)), r_in],
        out_specs=[out_spec, r_out],
        out_shape=[out_shape, r_shape],
        compiler_params=_params("arbitrary", "arbitrary"),
        name=name,
    )(h, w, rider[0])


def _proj_attn_kernel(x_ref, w_ref, cast_in, o_ref, cast_out, scr, *, dil, n_chunks):
    _cast_rider(cast_in, cast_out, n_chunks)
    acc = jnp.dot(x_ref[...], w_ref[...].astype(BF16), preferred_element_type=F32)
    if dil == 1:
        for hh in range(HEADS_PER_GROUP):
            o_ref[hh, 0] = acc[:, hh * LANES:(hh + 1) * LANES].astype(o_ref.dtype)
    else:
        rows = scr.shape[1] // dil
        for hh in range(HEADS_PER_GROUP):
            scr[hh] = acc[:, hh * LANES:(hh + 1) * LANES]
        for hh in range(HEADS_PER_GROUP):
            for c in range(dil):
                o_ref[hh, c] = scr[hh, pl.ds(c, rows, stride=dil), :].astype(o_ref.dtype)


def _proj_attn(h, w, layer, col0, gi, rider):
    s, d = h.shape
    tm = s // PROJ_ROW_TILES
    assert rider[2] <= PROJ_ROW_TILES * 3
    dil = ATTN_GROUPS[gi][1]
    tn = ATTN_OUT_WIDTH
    off = col0 // tn + gi
    n_groups = len(ATTN_GROUPS)
    r_in, r_out, r_shape = _rider_specs(rider, layer, 3)
    return pl.pallas_call(
        functools.partial(_proj_attn_kernel, dil=dil, n_chunks=rider[2]),
        grid=(s // tm, 3),
        in_specs=[pl.BlockSpec((tm, d), lambda i, j: (i, 0)),
                  pl.BlockSpec((None, d, tn), lambda i, j: (layer, 0, off + n_groups * j)), r_in],
        out_specs=[pl.BlockSpec((HEADS_PER_GROUP, dil, tm // dil, LANES), lambda i, j: (j, 0, i, 0)),
                   r_out],
        out_shape=[jax.ShapeDtypeStruct((3 * HEADS_PER_GROUP, dil, s // dil, LANES), BF16), r_shape],
        scratch_shapes=[pltpu.VMEM((HEADS_PER_GROUP, tm, LANES), F32)],
        compiler_params=_params("arbitrary", "arbitrary"),
        name=f"proj_attn_g{gi}",
    )(h, w, rider[0])


def _hgrn_constants():
    c = CHUNK
    t = np.arange(c)[:, None]
    s = np.arange(c)[None, :]
    mats = [s <= t]
    masks = []
    for lvl in range(N_LEVELS):
        m = c >> (lvl + 1)
        p = (t // (2 * m)) * (2 * m) + m - 1
        if m < MXU_LEVEL_BELOW:
            mats.append(((t > p) & (s > p) & (s <= t)) | ((t <= p) & (s > t) & (s <= p)))
        masks.append((t // (2 * m) == s // (2 * m)) & (t % (2 * m) >= m) & (s % (2 * m) < m))
    w = np.stack(mats).astype(np.float32)
    k = np.stack(masks).astype(np.float32)
    w = np.stack([w, w[:, ::-1, ::-1]]).reshape(2, len(mats) * c, c)
    w = np.concatenate([w, w], axis=2)
    k = np.stack([k, k[:, ::-1, ::-1]])
    k = np.concatenate([k, k], axis=3)
    return jnp.asarray(w, BF16), jnp.asarray(k, BF16)


def _block_diag(x):
    zero = jnp.zeros((x.shape[0], LANES), x.dtype)
    return jnp.concatenate([jnp.concatenate([x[:, :LANES], zero], axis=1),
                            jnp.concatenate([zero, x[:, LANES:]], axis=1)], axis=0)


def _block_diag_t(x):
    zero = jnp.zeros((LANES, x.shape[1]), x.dtype)
    return jnp.concatenate([jnp.concatenate([x[:LANES], zero], axis=1),
                            jnp.concatenate([zero, x[LANES:]], axis=1)], axis=0)


def _hgrn_chunk(q, z, v, lb, wmat, m_ref, st, backward):
    half = 0.5 * (1.0 - lb)
    ct = half * jnp.tanh(0.5 * z)
    kk = half - ct
    logf = jnp.log2(jnp.maximum((lb + half) + ct, MIN_FORGET))
    hi = logf.astype(BF16)
    lo = (logf - hi.astype(F32)).astype(BF16)
    dec = jnp.dot(wmat, jnp.concatenate([hi, lo], axis=0), preferred_element_type=F32)
    b = dec[0:CHUNK]
    total = b[0:1] if backward else b[CHUNK - 1:CHUNK]
    q_in = (q * jnp.exp2(b)).astype(BF16)
    k_out = (kk * jnp.exp2(total - b)).astype(BF16)
    vb = v.astype(BF16)
    qb = q.astype(BF16)
    kb = kk.astype(BF16)
    o = jnp.dot(q_in, _block_diag_t(st.astype(BF16).T), preferred_element_type=F32)
    qk = q * kk
    diag = jnp.concatenate(
        [jnp.broadcast_to(jnp.sum(qk[:, :LANES], axis=-1, keepdims=True), (CHUNK, LANES)),
         jnp.broadcast_to(jnp.sum(qk[:, LANES:], axis=-1, keepdims=True), (CHUNK, LANES))], axis=1)
    o = o + diag * v
    a = None
    for lvl in range(N_LEVELS):
        m = CHUNK >> (lvl + 1)
        if m >= MXU_LEVEL_BELOW:
            parts = []
            for j in range(CHUNK // (2 * m)):
                lo_rows = b[2 * m * j:2 * m * j + m]
                hi_rows = b[2 * m * j + m:2 * m * (j + 1)]
                if backward:
                    piv = b[2 * m * j + m:2 * m * j + m + 1]
                    parts += [lo_rows - piv, piv - hi_rows]
                else:
                    piv = b[2 * m * j + m - 1:2 * m * j + m]
                    parts += [piv - lo_rows, hi_rows - piv]
            nl = jnp.concatenate(parts, axis=0)
        else:
            i = 1 + lvl - (N_LEVELS - N_MXU_LEVELS)
            nl = dec[i * CHUNK:(i + 1) * CHUNK]
        el = jnp.exp2(nl).astype(BF16)
        sc = jnp.dot(qb * el, _block_diag_t((kb * el).T), preferred_element_type=F32)
        sc = sc.astype(BF16) * m_ref[lvl]
        a = sc if a is None else a + sc
    o = o + jnp.dot(a, _block_diag(vb), preferred_element_type=F32)
    v_rows = jnp.concatenate([vb[:, :LANES], vb[:, LANES:]], axis=0)
    st_new = st * jnp.exp2(total) + lax.dot_general(v_rows, _block_diag(k_out), _TN,
                                                    preferred_element_type=F32)
    return o, st_new


def _hgrn_kernel(qf_ref, zf_ref, vf_ref, qb_ref, zb_ref, vb_ref, lb_ref, w_ref, m_ref,
                 of_ref, ob_ref, stf_ref, stb_ref):
    @pl.when(pl.program_id(0) == 0)
    def _():
        stf_ref[...] = jnp.zeros_like(stf_ref)
        stb_ref[...] = jnp.zeros_like(stb_ref)

    n_sub = qf_ref.shape[1] // CHUNK

    def sub_chunk(t, carry):
        rf = pl.ds(pl.multiple_of(t * CHUNK, CHUNK), CHUNK)
        rb = pl.ds(pl.multiple_of((n_sub - 1 - t) * CHUNK, CHUNK), CHUNK)
        for p in range(HGRN_HEADS // 2):
            o, st = _hgrn_chunk(qf_ref[p, rf, :], zf_ref[p, rf, :], vf_ref[p, rf, :], lb_ref[0, p],
                                w_ref[0], m_ref.at[0], stf_ref[p], False)
            of_ref[p, rf, :] = o
            stf_ref[p] = st
            o, st = _hgrn_chunk(qb_ref[p, rb, :], zb_ref[p, rb, :], vb_ref[p, rb, :], lb_ref[1, p],
                                w_ref[1], m_ref.at[1], stb_ref[p], True)
            ob_ref[p, rb, :] = o
            stb_ref[p] = st
        return carry

    lax.fori_loop(0, n_sub, sub_chunk, 0)


def _hgrn(proj_h, lb):
    s = proj_h.shape[1]
    n = s // HGRN_BLOCK
    pairs = HGRN_HEADS // 2
    wmat, masks = _hgrn_constants()
    hb = (pairs, HGRN_BLOCK, 2 * LANES)
    fwd = lambda slab: pl.BlockSpec(hb, lambda c: (slab, c, 0))
    bwd = lambda slab: pl.BlockSpec(hb, lambda c: (slab, n - 1 - c, 0))
    const = lambda a: pl.BlockSpec(a.shape, lambda c: (0,) * a.ndim)
    out_shape = jax.ShapeDtypeStruct((pairs, s, 2 * LANES), F32)
    state = pltpu.VMEM((pairs, LANES, 2 * LANES), F32)
    return pl.pallas_call(
        _hgrn_kernel,
        grid=(n,),
        in_specs=[fwd(0), fwd(1), fwd(3), bwd(0), bwd(2), bwd(3), const(lb), const(wmat), const(masks)],
        out_specs=[fwd(0), bwd(0)],
        out_shape=[out_shape, out_shape],
        scratch_shapes=[state, state],
        compiler_params=_params("arbitrary"),
        name="hgrn2_scan",
    )(proj_h, proj_h, proj_h, proj_h, proj_h, proj_h, lb, wmat, masks)


def _t5_bucket(rel):
    half = REL_BUCKETS // 2
    ret = (rel > 0).astype(np.int32) * half
    n = np.abs(rel)
    max_exact = half // 2
    large = max_exact + (np.log(np.maximum(n, 1) / max_exact)
                         / np.log(REL_MAX_DISTANCE / max_exact)
                         * (half - max_exact)).astype(np.int32)
    large = np.minimum(large, half - 1)
    return (ret + np.where(n < max_exact, n, large)).astype(np.int32)


def _attn_bias(rel_bias_table):
    span = 3 * ATTN_QBLK - 1
    rel = np.arange(span) - (ATTN_QBLK - 1) - ATTN_SIDE
    col = np.arange(2 * ATTN_QBLK)[None, :]
    band = np.abs(col - ATTN_SIDE - np.arange(ATTN_QBLK)[:, None]) <= ATTN_SIDE
    after_start, before_end = col >= ATTN_SIDE, col < 2 * ATTN_QBLK - ATTN_SIDE
    keep = [band, band & after_start, band & before_end, band & after_start & before_end]
    out = []
    for gi, (_, dil) in enumerate(ATTN_GROUPS):
        tab = rel_bias_table[:, gi * HEADS_PER_GROUP:(gi + 1) * HEADS_PER_GROUP].astype(F32)
        onehot = jnp.asarray(_t5_bucket(rel * dil)[:, None] == np.arange(REL_BUCKETS)[None, :], F32)
        vec = jnp.einsum("rb,bh->hr", onehot, tab, precision=lax.Precision.HIGHEST)
        period = span + 2
        vec = jnp.pad(vec, ((0, 0), (0, period - span)))
        flat = jnp.tile(vec, (1, ATTN_QBLK))[:, :ATTN_QBLK * (span + 1)]
        toep = flat.reshape(HEADS_PER_GROUP, ATTN_QBLK, span + 1)[:, :, ATTN_QBLK - 1:3 * ATTN_QBLK - 1]
        out.append(jnp.stack([jnp.where(jnp.asarray(k)[None], toep, NEG_INF) for k in keep], axis=1))
    return jnp.stack(out)


def _attn_kernel(q_ref, kp_ref, km_ref, kn_ref, vp_ref, vm_ref, vn_ref, bias_ref, *rest,
                 rows, dil, cpb, has_prev, final):
    rest = list(rest)
    op_ref, lp_ref = (rest.pop(0), rest.pop(0)) if has_prev else (None, None)
    o_ref = rest.pop(0)
    lse_ref = None if final else rest.pop(0)
    o_s, l_s = rest if rest else (o_ref, lse_ref)
    c = pl.program_id(2)
    i = pl.program_id(1)
    scale = LANES ** -0.5
    n_sub = rows // ATTN_QBLK

    def window(prev_ref, main_ref, next_ref, cc, j):
        lo, hi = j * ATTN_QBLK - ATTN_SIDE, (j + 1) * ATTN_QBLK + ATTN_SIDE
        parts = []
        if lo < 0:
            parts.append(prev_ref[cc])
        parts.append(main_ref[cc, max(lo, 0):min(hi, rows)])
        if hi > rows:
            parts.append(next_ref[cc])
        return parts[0] if len(parts) == 1 else jnp.concatenate(parts, axis=0)

    for cc in range(cpb):
        scores = []
        for j in range(n_sub):
            r0 = j * ATTN_QBLK
            variant = 0
            if j == 0:
                variant = variant + (i == 0).astype(jnp.int32)
            if j == n_sub - 1:
                variant = variant + 2 * (i == pl.num_programs(1) - 1).astype(jnp.int32)
            scores.append(lax.dot_general(q_ref[cc, r0:r0 + ATTN_QBLK],
                                          window(kp_ref, km_ref, kn_ref, cc, j), _NT,
                                          preferred_element_type=F32) * scale + bias_ref[variant])
        for j, s in enumerate(scores):
            r0 = j * ATTN_QBLK
            m = jnp.max(s, axis=-1, keepdims=True)
            p = jnp.exp(s - m)
            l = jnp.sum(p, axis=-1, keepdims=True)
            o = jnp.dot(p.astype(BF16), window(vp_ref, vm_ref, vn_ref, cc, j),
                        preferred_element_type=F32) / l
            lse = jnp.broadcast_to(m + jnp.log(l), (ATTN_QBLK, LANES))
            if dil == 1:
                o_s[r0:r0 + ATTN_QBLK] = o
                l_s[r0:r0 + ATTN_QBLK] = lse
            else:
                start = r0 * dil + c * cpb + cc
                o_s[pl.ds(start, ATTN_QBLK, stride=dil), :] = o
                l_s[pl.ds(start, ATTN_QBLK, stride=dil), :] = lse

    if has_prev:
        @pl.when(c == dil // cpb - 1)
        def _():
            def fold(t, carry):
                r = pl.ds(pl.multiple_of(t * ATTN_FOLD_ROWS, ATTN_FOLD_ROWS), ATTN_FOLD_ROWS)
                lp, lc = lp_ref[r, :], l_s[r, :]
                m = jnp.maximum(lp, lc)
                wp, wc = jnp.exp(lp - m), jnp.exp(lc - m)
                den = wp + wc
                o = (wp * op_ref[r, :] + wc * o_s[r, :]) / den
                o_ref[r, :] = o.astype(o_ref.dtype)
                if not final:
                    lse_ref[r, :] = m + jnp.log(den)
                return carry

            lax.fori_loop(0, rows * dil // ATTN_FOLD_ROWS, fold, 0)


def _attn_group(qkv, bias, gi, prev):
    dil, length = qkv.shape[1], qkv.shape[2]
    s = dil * length
    cpb = min(ATTN_RESIDUES, dil)
    rows = min(ATTN_TILES * ATTN_QBLK // cpb, length)
    nblk = length // rows
    sub = rows // ATTN_SIDE
    last = length // ATTN_SIDE - 1
    hp = HEADS_PER_GROUP
    final = gi == len(ATTN_GROUPS) - 1
    has_prev = prev is not None
    assert has_prev or dil == 1

    def main(kind):
        return pl.BlockSpec((None, cpb, rows, LANES), lambda g, i, c: (kind * hp + g, c, i, 0))

    def prev_blk(kind):
        return pl.BlockSpec((None, cpb, ATTN_SIDE, LANES),
                            lambda g, i, c: (kind * hp + g, c, jnp.maximum(i * sub - 1, 0), 0))

    def next_blk(kind):
        return pl.BlockSpec((None, cpb, ATTN_SIDE, LANES),
                            lambda g, i, c: (kind * hp + g, c, jnp.minimum((i + 1) * sub, last), 0))

    run_spec = pl.BlockSpec((None, rows * dil, LANES), lambda g, i, c: (g, i, 0))
    run_shape = jax.ShapeDtypeStruct((hp, s, LANES), F32)
    in_specs = [main(0), prev_blk(1), main(1), next_blk(1), prev_blk(2), main(2), next_blk(2),
                pl.BlockSpec((None, None, 4, ATTN_QBLK, 2 * ATTN_QBLK), lambda g, i, c: (gi, g, 0, 0, 0))]
    args = [qkv] * 7 + [bias]
    if has_prev:
        in_specs += [run_spec, run_spec]
        args += list(prev)
    if final:
        out_specs = pl.BlockSpec((rows * dil, LANES), lambda g, i, c: (i, g))
        out_shape = jax.ShapeDtypeStruct((s, hp * LANES), BF16)
    else:
        out_specs = [run_spec, run_spec]
        out_shape = [run_shape, run_shape]
    scratch = []
    if has_prev:
        scratch = [pltpu.VMEM((rows * dil, LANES), F32), pltpu.VMEM((rows * dil, LANES), F32)]
    return pl.pallas_call(
        functools.partial(_attn_kernel, rows=rows, dil=dil, cpb=cpb, has_prev=has_prev, final=final),
        grid=(hp, nblk, dil // cpb),
        in_specs=in_specs,
        out_specs=out_specs,
        out_shape=out_shape,
        scratch_shapes=scratch,
        compiler_params=_params("parallel", "arbitrary", "arbitrary"),
        name=f"dilated_attn_g{gi}",
    )(*args)


def _merge_kernel(of_ref, ob_ref, g_ref, oa_ref, gate_ref, x_ref, pa_ref, pb_ref, wo_ref, hw_ref,
                  nw_ref, xo_ref, ho_ref, hs):
    for h in range(HGRN_HEADS):
        lanes = slice((h % 2) * LANES, (h % 2 + 1) * LANES)
        o = of_ref[h // 2, :, lanes] + ob_ref[h // 2, :, lanes]
        o = o * lax.rsqrt(jnp.mean(o * o, axis=-1, keepdims=True) + NORM_EPS)
        g = g_ref[h // 2, :, lanes]
        hs[:, h * LANES:(h + 1) * LANES] = (o * hw_ref[h] * (g * _sigmoid(g))).astype(BF16)
    d = x_ref.shape[1]
    a = jnp.dot(hs[...], pa_ref[...], preferred_element_type=F32)
    b = jnp.dot(oa_ref[...], pb_ref[...], preferred_element_type=F32)
    merged = _sigmoid(gate_ref[:, :d]) * a.astype(BF16) + _sigmoid(gate_ref[:, d:]) * b.astype(BF16)
    xn = x_ref[...] + jnp.dot(merged, wo_ref[...], preferred_element_type=F32)
    xo_ref[...] = xn
    ho_ref[...] = _rms(xn, nw_ref[...]).astype(ho_ref.dtype)


def _merge(o_hgrn, proj_h, o_attn, gates, x, pa, pb, wo, hgrn_norm_w, norm_w, tm=256):
    s, d = x.shape
    pairs = HGRN_HEADS // 2
    hspec = pl.BlockSpec((pairs, tm, 2 * LANES), lambda i: (0, i, 0))

    def wspec(rows):
        return pl.BlockSpec((rows, d), lambda i: (0, 0))

    return pl.pallas_call(
        _merge_kernel,
        grid=(s // tm,),
        in_specs=[hspec, hspec,
                  pl.BlockSpec((pairs, tm, 2 * LANES), lambda i: (4, i, 0)),
                  pl.BlockSpec((tm, ATTN_OUT_WIDTH), lambda i: (i, 0)),
                  pl.BlockSpec((tm, 2 * d), lambda i: (i, 0)),
                  pl.BlockSpec((tm, d), lambda i: (i, 0)),
                  wspec(HGRN_WIDTH), wspec(ATTN_OUT_WIDTH), wspec(d),
                  pl.BlockSpec((HGRN_HEADS, 1, LANES), lambda i: (0, 0, 0)),
                  pl.BlockSpec((1, d), lambda i: (0, 0))],
        out_specs=[pl.BlockSpec((tm, d), lambda i: (i, 0)),
                   pl.BlockSpec((tm, d), lambda i: (i, 0))],
        out_shape=[jax.ShapeDtypeStruct((s, d), F32), jax.ShapeDtypeStruct((s, d), BF16)],
        scratch_shapes=[pltpu.VMEM((tm, HGRN_WIDTH), BF16)],
        compiler_params=_params("parallel"),
        name="merge_out_proj",
    )(o_hgrn[0], o_hgrn[1], proj_h, o_attn, gates, x, pa, pb, wo,
      hgrn_norm_w.reshape(HGRN_HEADS, 1, LANES), norm_w.reshape(1, d))


def _mlp_kernel(h_ref, x_ref, wu_ref, wd_ref, nw_ref, xo_ref, ho_ref):
    f = pl.program_id(1)

    @pl.when(f == 0)
    def _():
        xo_ref[...] = x_ref[...]

    u = jnp.maximum(jnp.dot(h_ref[...], wu_ref[...], preferred_element_type=F32), 0.0)
    xo_ref[...] += jnp.dot((u * u).astype(BF16), wd_ref[...], preferred_element_type=F32)

    @pl.when(f == pl.num_programs(1) - 1)
    def _():
        ho_ref[...] = _rms(xo_ref[...], nw_ref[...]).astype(ho_ref.dtype)


def _mlp(h, x, wu, wd, norm_w, h_dtype, tm=512, tf=2048):
    s, d = x.shape
    ff = wu.shape[1]
    return pl.pallas_call(
        _mlp_kernel,
        grid=(s // tm, ff // tf),
        in_specs=[pl.BlockSpec((tm, d), lambda i, f: (i, 0), pipeline_mode=pl.Buffered(1)),
                  pl.BlockSpec((tm, d), lambda i, f: (i, 0)),
                  pl.BlockSpec((d, tf), lambda i, f: (0, f)),
                  pl.BlockSpec((tf, d), lambda i, f: (f, 0)),
                  pl.BlockSpec((1, d), lambda i, f: (0, 0))],
        out_specs=[pl.BlockSpec((tm, d), lambda i, f: (i, 0)),
                   pl.BlockSpec((tm, d), lambda i, f: (i, 0))],
        out_shape=[jax.ShapeDtypeStruct((s, d), F32), jax.ShapeDtypeStruct((s, d), h_dtype)],
        compiler_params=pltpu.CompilerParams(dimension_semantics=("parallel", "arbitrary"),
                                             vmem_limit_bytes=MLP_VMEM_LIMIT),
        name="mlp_relu2",
    )(h, x, wu, wd, norm_w.reshape(1, d))


def _lower_bounds(logits):
    p = jax.nn.softmax(logits.astype(F32), axis=0)
    return jnp.cumsum(p, axis=0) - p[0:1]


def kernel(x, w_in, hgrn_lb_fwd, hgrn_lb_bwd, hgrn_norm_w, rel_bias_table, w_branch_hgrn,
           w_branch_attn, w_out, norm_mix_w, norm_mlp_w, w_up, w_down, final_norm_w):
    batch, s, d = x.shape
    depth = w_in.shape[0]
    assert batch == 1
    x = x.reshape(s, d)
    lb = jnp.stack([_lower_bounds(hgrn_lb_fwd), _lower_bounds(hgrn_lb_bwd)], axis=1)
    lb = lb.reshape(-1, 2, HGRN_HEADS // 2, 1, 2 * LANES)
    bias = _attn_bias(rel_bias_table)
    n_h = 5 * HGRN_WIDTH
    n_a = 3 * ATTN_WIDTH
    cast_rows = 256
    h = _norm(x, norm_mix_w[0])
    for l in range(depth):
        proj_h, wd_b = _proj(h, w_in, l, 0, n_h, F32, 2 * LANES, "proj_hgrn",
                             (w_down, 0, w_down.shape[1] // cast_rows))
        gates, wu_b = _proj(h, w_in, l, n_h + n_a, 2 * d, BF16, 0, "proj_gates",
                            (w_up, 1, w_up.shape[2] // cast_rows))
        o_hgrn = _hgrn(proj_h, lb[l])
        attn = None
        cast = []
        for gi, wt in enumerate((w_out, w_branch_hgrn, w_branch_attn)):
            qkv, wt_b = _proj_attn(h, w_in, l, n_h, gi, (wt, 0, wt.shape[1] // cast_rows))
            attn = _attn_group(qkv, bias, gi, attn)
            cast.append(wt_b)
        wo_b, pa_b, pb_b = cast
        x, h2 = _merge(o_hgrn, proj_h, attn, gates, x, pa_b, pb_b, wo_b, hgrn_norm_w[l],
                       norm_mlp_w[l])
        last = l == depth - 1
        x, h = _mlp(h2, x, wu_b, wd_b, final_norm_w if last else norm_mix_w[l + 1],
                    F32 if last else BF16)
    return h.reshape(batch, s, d)
```

```python
import functools

import jax
import jax.numpy as jnp
import numpy as np
from jax import lax
from jax.experimental import pallas as pl
from jax.experimental.pallas import tpu as pltpu

F32 = jnp.float32
BF16 = jnp.bfloat16

LANES = 128
HGRN_HEADS = 8
HGRN_WIDTH = HGRN_HEADS * LANES
CHUNK = 128
HGRN_BLOCK = 2 * CHUNK
N_LEVELS = 7
MXU_LEVEL_BELOW = 8
N_MXU_LEVELS = 3
MIN_FORGET = 1e-30
ATTN_GROUPS = ((128, 1), (512, 4), (2048, 16))
HEADS_PER_GROUP = 4
N_ATTN_HEADS = HEADS_PER_GROUP * len(ATTN_GROUPS)
ATTN_WIDTH = N_ATTN_HEADS * LANES
ATTN_OUT_WIDTH = HEADS_PER_GROUP * LANES
ATTN_QBLK = 128
ATTN_SIDE = 64
ATTN_TILES = 16
ATTN_RESIDUES = 4
ATTN_FOLD_ROWS = 256
REL_BUCKETS = 32
REL_MAX_DISTANCE = 1024
NORM_EPS = 1e-6
NEG_INF = -1e30
PROJ_ROW_TILES = 4
VMEM_LIMIT = 56 * 1024 * 1024

_NT = (((1,), (1,)), ((), ()))
_TN = (((0,), (0,)), ((), ()))


def _params(*sem):
    return pltpu.CompilerParams(dimension_semantics=sem, vmem_limit_bytes=VMEM_LIMIT)


def _rms(x, w):
    return x * lax.rsqrt(jnp.mean(x * x, axis=-1, keepdims=True) + NORM_EPS) * w


def _sigmoid(x):
    return 0.5 * jnp.tanh(0.5 * x) + 0.5


def _norm_kernel(x_ref, w_ref, o_ref):
    o_ref[...] = _rms(x_ref[...], w_ref[...]).astype(o_ref.dtype)


def _norm(x, w, tm=512):
    s, d = x.shape
    return pl.pallas_call(
        _norm_kernel,
        grid=(s // tm,),
        in_specs=[pl.BlockSpec((tm, d), lambda i: (i, 0)),
                  pl.BlockSpec((1, d), lambda i: (0, 0))],
        out_specs=pl.BlockSpec((tm, d), lambda i: (i, 0)),
        out_shape=jax.ShapeDtypeStruct((s, d), BF16),
        compiler_params=_params("parallel"),
        name="rmsnorm_in",
    )(x, w.reshape(1, d))


def _cast_rider(cast_in, cast_out, n_chunks):
    step = pl.program_id(0) * pl.num_programs(1) + pl.program_id(1)

    @pl.when(step < n_chunks)
    def _():
        cast_out[...] = cast_in[...].astype(cast_out.dtype)


def _rider_specs(rider, layer, n_j):
    w, axis, n_chunks = rider
    r, c = w.shape[1:]

    def chunk(i, j):
        return jnp.minimum(i * n_j + j, n_chunks - 1)

    if axis == 0:
        blk = (r // n_chunks, c)
        in_spec = pl.BlockSpec((None,) + blk, lambda i, j: (layer, chunk(i, j), 0))
        out_spec = pl.BlockSpec(blk, lambda i, j: (chunk(i, j), 0))
    else:
        blk = (r, c // n_chunks)
        in_spec = pl.BlockSpec((None,) + blk, lambda i, j: (layer, 0, chunk(i, j)))
        out_spec = pl.BlockSpec(blk, lambda i, j: (0, chunk(i, j)))
    return in_spec, out_spec, jax.ShapeDtypeStruct((r, c), BF16)


def _proj_kernel(x_ref, w_ref, cast_in, o_ref, cast_out, *, slabs, n_chunks):
    _cast_rider(cast_in, cast_out, n_chunks)
    acc = jnp.dot(x_ref[...], w_ref[...].astype(BF16), preferred_element_type=F32)
    if slabs:
        width = o_ref.shape[-1]
        for j in range(slabs):
            o_ref[j] = acc[:, j * width:(j + 1) * width].astype(o_ref.dtype)
    else:
        o_ref[...] = acc.astype(o_ref.dtype)


def _proj(h, w, layer, col0, n, out_dtype, slab_width, name, rider, tn=512):
    s, d = h.shape
    tm = s // PROJ_ROW_TILES
    off = col0 // tn
    n_j = n // tn
    assert rider[2] <= PROJ_ROW_TILES * n_j
    if slab_width:
        slabs = tn // slab_width
        out_shape = jax.ShapeDtypeStruct((n // slab_width, s, slab_width), out_dtype)
        out_spec = pl.BlockSpec((slabs, tm, slab_width), lambda i, j: (j, i, 0))
    else:
        slabs = 0
        out_shape = jax.ShapeDtypeStruct((s, n), out_dtype)
        out_spec = pl.BlockSpec((tm, tn), lambda i, j: (i, j))
    r_in, r_out, r_shape = _rider_specs(rider, layer, n_j)
    return pl.pallas_call(
        functools.partial(_proj_kernel, slabs=slabs, n_chunks=rider[2]),
        grid=(s // tm, n_j),
        in_specs=[pl.BlockSpec((tm, d), lambda i, j: (i, 0)),
                  pl.BlockSpec((None, d, tn), lambda i, j: (layer, 0, j + off)), r_in],
        out_specs=[out_spec, r_out],
        out_shape=[out_shape, r_shape],
        compiler_params=_params("arbitrary", "arbitrary"),
        name=name,
    )(h, w, rider[0])


def _proj_attn_kernel(x_ref, w_ref, cast_in, o_ref, cast_out, scr, *, dil, n_chunks):
    _cast_rider(cast_in, cast_out, n_chunks)
    acc = jnp.dot(x_ref[...], w_ref[...].astype(BF16), preferred_element_type=F32)
    if dil == 1:
        for hh in range(HEADS_PER_GROUP):
            o_ref[hh, 0] = acc[:, hh * LANES:(hh + 1) * LANES].astype(o_ref.dtype)
    else:
        rows = scr.shape[1] // dil
        for hh in range(HEADS_PER_GROUP):
            scr[hh] = acc[:, hh * LANES:(hh + 1) * LANES]
        for hh in range(HEADS_PER_GROUP):
            for c in range(dil):
                o_ref[hh, c] = scr[hh, pl.ds(c, rows, stride=dil), :].astype(o_ref.dtype)


def _proj_attn(h, w, layer, col0, gi, rider):
    s, d = h.shape
    tm = s // PROJ_ROW_TILES
    assert rider[2] <= PROJ_ROW_TILES * 3
    dil = ATTN_GROUPS[gi][1]
    tn = ATTN_OUT_WIDTH
    off = col0 // tn + gi
    n_groups = len(ATTN_GROUPS)
    r_in, r_out, r_shape = _rider_specs(rider, layer, 3)
    return pl.pallas_call(
        functools.partial(_proj_attn_kernel, dil=dil, n_chunks=rider[2]),
        grid=(s // tm, 3),
        in_specs=[pl.BlockSpec((tm, d), lambda i, j: (i, 0)),
                  pl.BlockSpec((None, d, tn), lambda i, j: (layer, 0, off + n_groups * j)), r_in],
        out_specs=[pl.BlockSpec((HEADS_PER_GROUP, dil, tm // dil, LANES), lambda i, j: (j, 0, i, 0)),
                   r_out],
        out_shape=[jax.ShapeDtypeStruct((3 * HEADS_PER_GROUP, dil, s // dil, LANES), BF16), r_shape],
        scratch_shapes=[pltpu.VMEM((HEADS_PER_GROUP, tm, LANES), F32)],
        compiler_params=_params("arbitrary", "arbitrary"),
        name=f"proj_attn_g{gi}",
    )(h, w, rider[0])


def _hgrn_constants():
    c = CHUNK
    t = np.arange(c)[:, None]
    s = np.arange(c)[None, :]
    mats = [s <= t]
    masks = []
    for lvl in range(N_LEVELS):
        m = c >> (lvl + 1)
        p = (t // (2 * m)) * (2 * m) + m - 1
        if m < MXU_LEVEL_BELOW:
            mats.append(((t > p) & (s > p) & (s <= t)) | ((t <= p) & (s > t) & (s <= p)))
        masks.append((t // (2 * m) == s // (2 * m)) & (t % (2 * m) >= m) & (s % (2 * m) < m))
    w = np.stack(mats).astype(np.float32)
    k = np.stack(masks).astype(np.float32)
    w = np.stack([w, w[:, ::-1, ::-1]]).reshape(2, len(mats) * c, c)
    w = np.concatenate([w, w], axis=2)
    k = np.stack([k, k[:, ::-1, ::-1]])
    k = np.concatenate([k, k], axis=3)
    return jnp.asarray(w, BF16), jnp.asarray(k, BF16)


def _block_diag(x):
    zero = jnp.zeros((x.shape[0], LANES), x.dtype)
    return jnp.concatenate([jnp.concatenate([x[:, :LANES], zero], axis=1),
                            jnp.concatenate([zero, x[:, LANES:]], axis=1)], axis=0)


def _block_diag_t(x):
    zero = jnp.zeros((LANES, x.shape[1]), x.dtype)
    return jnp.concatenate([jnp.concatenate([x[:LANES], zero], axis=1),
                            jnp.concatenate([zero, x[LANES:]], axis=1)], axis=0)


def _hgrn_chunk(q, z, v, lb, wmat, m_ref, st, backward):
    half = 0.5 * (1.0 - lb)
    ct = half * jnp.tanh(0.5 * z)
    kk = half - ct
    logf = jnp.log2(jnp.maximum((lb + half) + ct, MIN_FORGET))
    hi = logf.astype(BF16)
    lo = (logf - hi.astype(F32)).astype(BF16)
    dec = jnp.dot(wmat, jnp.concatenate([hi, lo], axis=0), preferred_element_type=F32)
    b = dec[0:CHUNK]
    total = b[0:1] if backward else b[CHUNK - 1:CHUNK]
    q_in = (q * jnp.exp2(b)).astype(BF16)
    k_out = (kk * jnp.exp2(total - b)).astype(BF16)
    vb = v.astype(BF16)
    qb = q.astype(BF16)
    kb = kk.astype(BF16)
    o = jnp.dot(q_in, _block_diag_t(st.astype(BF16).T), preferred_element_type=F32)
    qk = q * kk
    diag = jnp.concatenate(
        [jnp.broadcast_to(jnp.sum(qk[:, :LANES], axis=-1, keepdims=True), (CHUNK, LANES)),
         jnp.broadcast_to(jnp.sum(qk[:, LANES:], axis=-1, keepdims=True), (CHUNK, LANES))], axis=1)
    o = o + diag * v
    a = None
    for lvl in range(N_LEVELS):
        m = CHUNK >> (lvl + 1)
        if m >= MXU_LEVEL_BELOW:
            parts = []
            for j in range(CHUNK // (2 * m)):
                lo_rows = b[2 * m * j:2 * m * j + m]
                hi_rows = b[2 * m * j + m:2 * m * (j + 1)]
                if backward:
                    piv = b[2 * m * j + m:2 * m * j + m + 1]
                    parts += [lo_rows - piv, piv - hi_rows]
                else:
                    piv = b[2 * m * j + m - 1:2 * m * j + m]
                    parts += [piv - lo_rows, hi_rows - piv]
            nl = jnp.concatenate(parts, axis=0)
        else:
            i = 1 + lvl - (N_LEVELS - N_MXU_LEVELS)
            nl = dec[i * CHUNK:(i + 1) * CHUNK]
        el = jnp.exp2(nl).astype(BF16)
        sc = jnp.dot(qb * el, _block_diag_t((kb * el).T), preferred_element_type=F32)
        sc = sc.astype(BF16) * m_ref[lvl]
        a = sc if a is None else a + sc
    o = o + jnp.dot(a, _block_diag(vb), preferred_element_type=F32)
    v_rows = jnp.concatenate([vb[:, :LANES], vb[:, LANES:]], axis=0)
    st_new = st * jnp.exp2(total) + lax.dot_general(v_rows, _block_diag(k_out), _TN,
                                                    preferred_element_type=F32)
    return o, st_new


def _hgrn_kernel(qf_ref, zf_ref, vf_ref, qb_ref, zb_ref, vb_ref, lb_ref, w_ref, m_ref,
                 of_ref, ob_ref, stf_ref, stb_ref):
    @pl.when(pl.program_id(0) == 0)
    def _():
        stf_ref[...] = jnp.zeros_like(stf_ref)
        stb_ref[...] = jnp.zeros_like(stb_ref)

    n_sub = qf_ref.shape[1] // CHUNK

    def sub_chunk(t, carry):
        rf = pl.ds(pl.multiple_of(t * CHUNK, CHUNK), CHUNK)
        rb = pl.ds(pl.multiple_of((n_sub - 1 - t) * CHUNK, CHUNK), CHUNK)
        for p in range(HGRN_HEADS // 2):
            o, st = _hgrn_chunk(qf_ref[p, rf, :], zf_ref[p, rf, :], vf_ref[p, rf, :], lb_ref[0, p],
                                w_ref[0], m_ref.at[0], stf_ref[p], False)
            of_ref[p, rf, :] = o
            stf_ref[p] = st
            o, st = _hgrn_chunk(qb_ref[p, rb, :], zb_ref[p, rb, :], vb_ref[p, rb, :], lb_ref[1, p],
                                w_ref[1], m_ref.at[1], stb_ref[p], True)
            ob_ref[p, rb, :] = o
            stb_ref[p] = st
        return carry

    lax.fori_loop(0, n_sub, sub_chunk, 0)


def _hgrn(proj_h, lb):
    s = proj_h.shape[1]
    n = s // HGRN_BLOCK
    pairs = HGRN_HEADS // 2
    wmat, masks = _hgrn_constants()
    hb = (pairs, HGRN_BLOCK, 2 * LANES)
    fwd = lambda slab: pl.BlockSpec(hb, lambda c: (slab, c, 0))
    bwd = lambda slab: pl.BlockSpec(hb, lambda c: (slab, n - 1 - c, 0))
    const = lambda a: pl.BlockSpec(a.shape, lambda c: (0,) * a.ndim)
    out_shape = jax.ShapeDtypeStruct((pairs, s, 2 * LANES), F32)
    state = pltpu.VMEM((pairs, LANES, 2 * LANES), F32)
    return pl.pallas_call(
        _hgrn_kernel,
        grid=(n,),
        in_specs=[fwd(0), fwd(1), fwd(3), bwd(0), bwd(2), bwd(3), const(lb), const(wmat), const(masks)],
        out_specs=[fwd(0), bwd(0)],
        out_shape=[out_shape, out_shape],
        scratch_shapes=[state, state],
        compiler_params=_params("arbitrary"),
        name="hgrn2_scan",
    )(proj_h, proj_h, proj_h, proj_h, proj_h, proj_h, lb, wmat, masks)


def _t5_bucket(rel):
    half = REL_BUCKETS // 2
    ret = (rel > 0).astype(np.int32) * half
    n = np.abs(rel)
    max_exact = half // 2
    large = max_exact + (np.log(np.maximum(n, 1) / max_exact)
                         / np.log(REL_MAX_DISTANCE / max_exact)
                         * (half - max_exact)).astype(np.int32)
    large = np.minimum(large, half - 1)
    return (ret + np.where(n < max_exact, n, large)).astype(np.int32)


def _attn_bias(rel_bias_table):
    span = 3 * ATTN_QBLK - 1
    rel = np.arange(span) - (ATTN_QBLK - 1) - ATTN_SIDE
    col = np.arange(2 * ATTN_QBLK)[None, :]
    band = np.abs(col - ATTN_SIDE - np.arange(ATTN_QBLK)[:, None]) <= ATTN_SIDE
    after_start, before_end = col >= ATTN_SIDE, col < 2 * ATTN_QBLK - ATTN_SIDE
    keep = [band, band & after_start, band & before_end, band & after_start & before_end]
    out = []
    for gi, (_, dil) in enumerate(ATTN_GROUPS):
        tab = rel_bias_table[:, gi * HEADS_PER_GROUP:(gi + 1) * HEADS_PER_GROUP].astype(F32)
        onehot = jnp.asarray(_t5_bucket(rel * dil)[:, None] == np.arange(REL_BUCKETS)[None, :], F32)
        vec = jnp.einsum("rb,bh->hr", onehot, tab, precision=lax.Precision.HIGHEST)
        period = span + 2
        vec = jnp.pad(vec, ((0, 0), (0, period - span)))
        flat = jnp.tile(vec, (1, ATTN_QBLK))[:, :ATTN_QBLK * (span + 1)]
        toep = flat.reshape(HEADS_PER_GROUP, ATTN_QBLK, span + 1)[:, :, ATTN_QBLK - 1:3 * ATTN_QBLK - 1]
        out.append(jnp.stack([jnp.where(jnp.asarray(k)[None], toep, NEG_INF) for k in keep], axis=1))
    return jnp.stack(out)


def _attn_kernel(q_ref, kp_ref, km_ref, kn_ref, vp_ref, vm_ref, vn_ref, bias_ref, *rest,
                 rows, dil, cpb, has_prev, final):
    rest = list(rest)
    op_ref, lp_ref = (rest.pop(0), rest.pop(0)) if has_prev else (None, None)
    o_ref = rest.pop(0)
    lse_ref = None if final else rest.pop(0)
    o_s, l_s = rest if rest else (o_ref, lse_ref)
    c = pl.program_id(2)
    i = pl.program_id(1)
    scale = LANES ** -0.5
    n_sub = rows // ATTN_QBLK

    def window(prev_ref, main_ref, next_ref, cc, j):
        lo, hi = j * ATTN_QBLK - ATTN_SIDE, (j + 1) * ATTN_QBLK + ATTN_SIDE
        parts = []
        if lo < 0:
            parts.append(prev_ref[cc])
        parts.append(main_ref[cc, max(lo, 0):min(hi, rows)])
        if hi > rows:
            parts.append(next_ref[cc])
        return parts[0] if len(parts) == 1 else jnp.concatenate(parts, axis=0)

    for cc in range(cpb):
        scores = []
        for j in range(n_sub):
            r0 = j * ATTN_QBLK
            variant = 0
            if j == 0:
                variant = variant + (i == 0).astype(jnp.int32)
            if j == n_sub - 1:
                variant = variant + 2 * (i == pl.num_programs(1) - 1).astype(jnp.int32)
            scores.append(lax.dot_general(q_ref[cc, r0:r0 + ATTN_QBLK],
                                          window(kp_ref, km_ref, kn_ref, cc, j), _NT,
                                          preferred_element_type=F32) * scale + bias_ref[variant])
        for j, s in enumerate(scores):
            r0 = j * ATTN_QBLK
            m = jnp.max(s, axis=-1, keepdims=True)
            p = jnp.exp(s - m)
            l = jnp.sum(p, axis=-1, keepdims=True)
            o = jnp.dot(p.astype(BF16), window(vp_ref, vm_ref, vn_ref, cc, j),
                        preferred_element_type=F32) / l
            lse = jnp.broadcast_to(m + jnp.log(l), (ATTN_QBLK, LANES))
            if dil == 1:
                o_s[r0:r0 + ATTN_QBLK] = o
                l_s[r0:r0 + ATTN_QBLK] = lse
            else:
                start = r0 * dil + c * cpb + cc
                o_s[pl.ds(start, ATTN_QBLK, stride=dil), :] = o
                l_s[pl.ds(start, ATTN_QBLK, stride=dil), :] = lse

    if has_prev:
        @pl.when(c == dil // cpb - 1)
        def _():
            def fold(t, carry):
                r = pl.ds(pl.multiple_of(t * ATTN_FOLD_ROWS, ATTN_FOLD_ROWS), ATTN_FOLD_ROWS)
                lp, lc = lp_ref[r, :], l_s[r, :]
                m = jnp.maximum(lp, lc)
                wp, wc = jnp.exp(lp - m), jnp.exp(lc - m)
                den = wp + wc
                o = (wp * op_ref[r, :] + wc * o_s[r, :]) / den
                o_ref[r, :] = o.astype(o_ref.dtype)
                if not final:
                    lse_ref[r, :] = m + jnp.log(den)
                return carry

            lax.fori_loop(0, rows * dil // ATTN_FOLD_ROWS, fold, 0)


def _attn_group(qkv, bias, gi, prev):
    dil, length = qkv.shape[1], qkv.shape[2]
    s = dil * length
    cpb = min(ATTN_RESIDUES, dil)
    rows = min(ATTN_TILES * ATTN_QBLK // cpb, length)
    nblk = length // rows
    sub = rows // ATTN_SIDE
    last = length // ATTN_SIDE - 1
    hp = HEADS_PER_GROUP
    final = gi == len(ATTN_GROUPS) - 1
    has_prev = prev is not None
    assert has_prev or dil == 1

    def main(kind):
        return pl.BlockSpec((None, cpb, rows, LANES), lambda g, i, c: (kind * hp + g, c, i, 0))

    def prev_blk(kind):
        return pl.BlockSpec((None, cpb, ATTN_SIDE, LANES),
                            lambda g, i, c: (kind * hp + g, c, jnp.maximum(i * sub - 1, 0), 0))

    def next_blk(kind):
        return pl.BlockSpec((None, cpb, ATTN_SIDE, LANES),
                            lambda g, i, c: (kind * hp + g, c, jnp.minimum((i + 1) * sub, last), 0))

    run_spec = pl.BlockSpec((None, rows * dil, LANES), lambda g, i, c: (g, i, 0))
    run_shape = jax.ShapeDtypeStruct((hp, s, LANES), F32)
    in_specs = [main(0), prev_blk(1), main(1), next_blk(1), prev_blk(2), main(2), next_blk(2),
                pl.BlockSpec((None, None, 4, ATTN_QBLK, 2 * ATTN_QBLK), lambda g, i, c: (gi, g, 0, 0, 0))]
    args = [qkv] * 7 + [bias]
    if has_prev:
        in_specs += [run_spec, run_spec]
        args += list(prev)
    if final:
        out_specs = pl.BlockSpec((rows * dil, LANES), lambda g, i, c: (i, g))
        out_shape = jax.ShapeDtypeStruct((s, hp * LANES), BF16)
    else:
        out_specs = [run_spec, run_spec]
        out_shape = [run_shape, run_shape]
    scratch = []
    if has_prev:
        scratch = [pltpu.VMEM((rows * dil, LANES), F32), pltpu.VMEM((rows * dil, LANES), F32)]
    return pl.pallas_call(
        functools.partial(_attn_kernel, rows=rows, dil=dil, cpb=cpb, has_prev=has_prev, final=final),
        grid=(hp, nblk, dil // cpb),
        in_specs=in_specs,
        out_specs=out_specs,
        out_shape=out_shape,
        scratch_shapes=scratch,
        compiler_params=_params("parallel", "arbitrary", "arbitrary"),
        name=f"dilated_attn_g{gi}",
    )(*args)


def _merge_kernel(of_ref, ob_ref, g_ref, oa_ref, gate_ref, x_ref, pa_ref, pb_ref, wo_ref, hw_ref,
                  nw_ref, xo_ref, ho_ref, hs):
    for h in range(HGRN_HEADS):
        lanes = slice((h % 2) * LANES, (h % 2 + 1) * LANES)
        o = of_ref[h // 2, :, lanes] + ob_ref[h // 2, :, lanes]
        o = o * lax.rsqrt(jnp.mean(o * o, axis=-1, keepdims=True) + NORM_EPS)
        g = g_ref[h // 2, :, lanes]
        hs[:, h * LANES:(h + 1) * LANES] = (o * hw_ref[h] * (g * _sigmoid(g))).astype(BF16)
    d = x_ref.shape[1]
    a = jnp.dot(hs[...], pa_ref[...], preferred_element_type=F32)
    b = jnp.dot(oa_ref[...], pb_ref[...], preferred_element_type=F32)
    merged = _sigmoid(gate_ref[:, :d]) * a.astype(BF16) + _sigmoid(gate_ref[:, d:]) * b.astype(BF16)
    xn = x_ref[...] + jnp.dot(merged, wo_ref[...], preferred_element_type=F32)
    xo_ref[...] = xn
    ho_ref[...] = _rms(xn, nw_ref[...]).astype(ho_ref.dtype)


def _merge(o_hgrn, proj_h, o_attn, gates, x, pa, pb, wo, hgrn_norm_w, norm_w, tm=256):
    s, d = x.shape
    pairs = HGRN_HEADS // 2
    hspec = pl.BlockSpec((pairs, tm, 2 * LANES), lambda i: (0, i, 0))

    def wspec(rows):
        return pl.BlockSpec((rows, d), lambda i: (0, 0))

    return pl.pallas_call(
        _merge_kernel,
        grid=(s // tm,),
        in_specs=[hspec, hspec,
                  pl.BlockSpec((pairs, tm, 2 * LANES), lambda i: (4, i, 0)),
                  pl.BlockSpec((tm, ATTN_OUT_WIDTH), lambda i: (i, 0)),
                  pl.BlockSpec((tm, 2 * d), lambda i: (i, 0)),
                  pl.BlockSpec((tm, d), lambda i: (i, 0)),
                  wspec(HGRN_WIDTH), wspec(ATTN_OUT_WIDTH), wspec(d),
                  pl.BlockSpec((HGRN_HEADS, 1, LANES), lambda i: (0, 0, 0)),
                  pl.BlockSpec((1, d), lambda i: (0, 0))],
        out_specs=[pl.BlockSpec((tm, d), lambda i: (i, 0)),
                   pl.BlockSpec((tm, d), lambda i: (i, 0))],
        out_shape=[jax.ShapeDtypeStruct((s, d), F32), jax.ShapeDtypeStruct((s, d), BF16)],
        scratch_shapes=[pltpu.VMEM((tm, HGRN_WIDTH), BF16)],
        compiler_params=_params("parallel"),
        name="merge_out_proj",
    )(o_hgrn[0], o_hgrn[1], proj_h, o_attn, gates, x, pa, pb, wo,
      hgrn_norm_w.reshape(HGRN_HEADS, 1, LANES), norm_w.reshape(1, d))


def _mlp_kernel(h_ref, x_ref, wu_ref, wd_ref, nw_ref, xo_ref, ho_ref, acc):
    f = pl.program_id(1)

    @pl.when(f == 0)
    def _():
        acc[...] = x_ref[...]

    u = jnp.maximum(jnp.dot(h_ref[...], wu_ref[...], preferred_element_type=F32), 0.0)
    acc[...] += jnp.dot((u * u).astype(BF16), wd_ref[...], preferred_element_type=F32)

    @pl.when(f == pl.num_programs(1) - 1)
    def _():
        xn = acc[...]
        xo_ref[...] = xn
        ho_ref[...] = _rms(xn, nw_ref[...]).astype(ho_ref.dtype)


def _mlp(h, x, wu, wd, norm_w, h_dtype, tm=512, tf=1024):
    s, d = x.shape
    ff = wu.shape[1]
    return pl.pallas_call(
        _mlp_kernel,
        grid=(s // tm, ff // tf),
        in_specs=[pl.BlockSpec((tm, d), lambda i, f: (i, 0)),
                  pl.BlockSpec((tm, d), lambda i, f: (i, 0)),
                  pl.BlockSpec((d, tf), lambda i, f: (0, f)),
                  pl.BlockSpec((tf, d), lambda i, f: (f, 0)),
                  pl.BlockSpec((1, d), lambda i, f: (0, 0))],
        out_specs=[pl.BlockSpec((tm, d), lambda i, f: (i, 0)),
                   pl.BlockSpec((tm, d), lambda i, f: (i, 0))],
        out_shape=[jax.ShapeDtypeStruct((s, d), F32), jax.ShapeDtypeStruct((s, d), h_dtype)],
        scratch_shapes=[pltpu.VMEM((tm, d), F32)],
        compiler_params=_params("parallel", "arbitrary"),
        name="mlp_relu2",
    )(h, x, wu, wd, norm_w.reshape(1, d))


def _lower_bounds(logits):
    p = jax.nn.softmax(logits.astype(F32), axis=0)
    return jnp.cumsum(p, axis=0) - p[0:1]


def kernel(x, w_in, hgrn_lb_fwd, hgrn_lb_bwd, hgrn_norm_w, rel_bias_table, w_branch_hgrn,
           w_branch_attn, w_out, norm_mix_w, norm_mlp_w, w_up, w_down, final_norm_w):
    batch, s, d = x.shape
    depth = w_in.shape[0]
    assert batch == 1
    x = x.reshape(s, d)
    lb = jnp.stack([_lower_bounds(hgrn_lb_fwd), _lower_bounds(hgrn_lb_bwd)], axis=1)
    lb = lb.reshape(-1, 2, HGRN_HEADS // 2, 1, 2 * LANES)
    bias = _attn_bias(rel_bias_table)
    n_h = 5 * HGRN_WIDTH
    n_a = 3 * ATTN_WIDTH
    cast_rows = 256
    h = _norm(x, norm_mix_w[0])
    for l in range(depth):
        proj_h, wd_b = _proj(h, w_in, l, 0, n_h, F32, 2 * LANES, "proj_hgrn",
                             (w_down, 0, w_down.shape[1] // cast_rows))
        gates, wu_b = _proj(h, w_in, l, n_h + n_a, 2 * d, BF16, 0, "proj_gates",
                            (w_up, 1, w_up.shape[2] // cast_rows))
        o_hgrn = _hgrn(proj_h, lb[l])
        attn = None
        cast = []
        for gi, wt in enumerate((w_out, w_branch_hgrn, w_branch_attn)):
            qkv, wt_b = _proj_attn(h, w_in, l, n_h, gi, (wt, 0, wt.shape[1] // cast_rows))
            attn = _attn_group(qkv, bias, gi, attn)
            cast.append(wt_b)
        wo_b, pa_b, pb_b = cast
        x, h2 = _merge(o_hgrn, proj_h, attn, gates, x, pa_b, pb_b, wo_b, hgrn_norm_w[l],
                       norm_mlp_w[l])
        last = l == depth - 1
        x, h = _mlp(h2, x, wu_b, wd_b, final_norm_w if last else norm_mix_w[l + 1],
                    F32 if last else BF16)
    return h.reshape(batch, s, d)
```

```python
import functools

import jax
import jax.numpy as jnp
import numpy as np
from jax import lax
from jax.experimental import pallas as pl
from jax.experimental.pallas import tpu as pltpu

F32 = jnp.float32
BF16 = jnp.bfloat16

LANES = 128
HGRN_HEADS = 8
HGRN_WIDTH = HGRN_HEADS * LANES
CHUNK = 128
HGRN_BLOCK = 4 * CHUNK
N_LEVELS = 7
MXU_LEVEL_BELOW = 8
N_MXU_LEVELS = 3
MIN_FORGET = 1e-30
ATTN_GROUPS = ((128, 1), (512, 4), (2048, 16))
HEADS_PER_GROUP = 4
N_ATTN_HEADS = HEADS_PER_GROUP * len(ATTN_GROUPS)
ATTN_WIDTH = N_ATTN_HEADS * LANES
ATTN_OUT_WIDTH = HEADS_PER_GROUP * LANES
ATTN_QBLK = 128
ATTN_SIDE = 64
ATTN_TILES = 32
ATTN_QK_BATCH = 8
ATTN_RESIDUES = 8
ATTN_FOLD_ROWS = 256
REL_BUCKETS = 32
REL_MAX_DISTANCE = 1024
NORM_EPS = 1e-6
NEG_INF = -1e30
PROJ_ROW_TILES = 4
VMEM_LIMIT = 56 * 1024 * 1024

_NT = (((1,), (1,)), ((), ()))
_TN = (((0,), (0,)), ((), ()))


def _params(*sem):
    return pltpu.CompilerParams(dimension_semantics=sem, vmem_limit_bytes=VMEM_LIMIT)


def _rms(x, w):
    return x * lax.rsqrt(jnp.mean(x * x, axis=-1, keepdims=True) + NORM_EPS) * w


def _sigmoid(x):
    return 0.5 * jnp.tanh(0.5 * x) + 0.5


def _norm_kernel(x_ref, w_ref, o_ref):
    o_ref[...] = _rms(x_ref[...], w_ref[...]).astype(o_ref.dtype)


def _norm(x, w, tm=512):
    s, d = x.shape
    return pl.pallas_call(
        _norm_kernel,
        grid=(s // tm,),
        in_specs=[pl.BlockSpec((tm, d), lambda i: (i, 0)),
                  pl.BlockSpec((1, d), lambda i: (0, 0))],
        out_specs=pl.BlockSpec((tm, d), lambda i: (i, 0)),
        out_shape=jax.ShapeDtypeStruct((s, d), BF16),
        compiler_params=_params("parallel"),
        name="rmsnorm_in",
    )(x, w.reshape(1, d))


def _cast_rider(cast_in, cast_out, n_chunks):
    step = pl.program_id(0) * pl.num_programs(1) + pl.program_id(1)

    @pl.when(step < n_chunks)
    def _():
        cast_out[...] = cast_in[...].astype(cast_out.dtype)


def _rider_specs(rider, layer, n_j):
    w, axis, n_chunks = rider
    r, c = w.shape[1:]

    def chunk(i, j):
        return jnp.minimum(i * n_j + j, n_chunks - 1)

    if axis == 0:
        blk = (r // n_chunks, c)
        in_spec = pl.BlockSpec((None,) + blk, lambda i, j: (layer, chunk(i, j), 0))
        out_spec = pl.BlockSpec(blk, lambda i, j: (chunk(i, j), 0))
    else:
        blk = (r, c // n_chunks)
        in_spec = pl.BlockSpec((None,) + blk, lambda i, j: (layer, 0, chunk(i, j)))
        out_spec = pl.BlockSpec(blk, lambda i, j: (0, chunk(i, j)))
    return in_spec, out_spec, jax.ShapeDtypeStruct((r, c), BF16)


def _proj_kernel(x_ref, w_ref, cast_in, o_ref, cast_out, *, slabs, n_chunks):
    _cast_rider(cast_in, cast_out, n_chunks)
    acc = jnp.dot(x_ref[...], w_ref[...].astype(BF16), preferred_element_type=F32)
    if slabs:
        width = o_ref.shape[-1]
        for j in range(slabs):
            o_ref[j] = acc[:, j * width:(j + 1) * width].astype(o_ref.dtype)
    else:
        o_ref[...] = acc.astype(o_ref.dtype)


def _proj(h, w, layer, col0, n, out_dtype, slab_width, name, rider, tn=512):
    s, d = h.shape
    tm = s // PROJ_ROW_TILES
    off = col0 // tn
    n_j = n // tn
    assert rider[2] <= PROJ_ROW_TILES * n_j
    if slab_width:
        slabs = tn // slab_width
        out_shape = jax.ShapeDtypeStruct((n // slab_width, s, slab_width), out_dtype)
        out_spec = pl.BlockSpec((slabs, tm, slab_width), lambda i, j: (j, i, 0))
    else:
        slabs = 0
        out_shape = jax.ShapeDtypeStruct((s, n), out_dtype)
        out_spec = pl.BlockSpec((tm, tn), lambda i, j: (i, j))
    r_in, r_out, r_shape = _rider_specs(rider, layer, n_j)
    return pl.pallas_call(
        functools.partial(_proj_kernel, slabs=slabs, n_chunks=rider[2]),
        grid=(s // tm, n_j),
        in_specs=[pl.BlockSpec((tm, d), lambda i, j: (i, 0)),
                  pl.BlockSpec((None, d, tn), lambda i, j: (layer, 0, j + off)), r_in],
        out_specs=[out_spec, r_out],
        out_shape=[out_shape, r_shape],
        compiler_params=_params("arbitrary", "arbitrary"),
        name=name,
    )(h, w, rider[0])


def _proj_attn_kernel(x_ref, w_ref, cast_in, o_ref, cast_out, scr, *, dil, n_chunks):
    _cast_rider(cast_in, cast_out, n_chunks)
    acc = jnp.dot(x_ref[...], w_ref[...].astype(BF16), preferred_element_type=F32)
    if dil == 1:
        for hh in range(HEADS_PER_GROUP):
            o_ref[hh, 0] = acc[:, hh * LANES:(hh + 1) * LANES].astype(o_ref.dtype)
    else:
        rows = scr.shape[1] // dil
        for hh in range(HEADS_PER_GROUP):
            scr[hh] = acc[:, hh * LANES:(hh + 1) * LANES]
        for hh in range(HEADS_PER_GROUP):
            for c in range(dil):
                o_ref[hh, c] = scr[hh, pl.ds(c, rows, stride=dil), :].astype(o_ref.dtype)


def _proj_attn(h, w, layer, col0, gi, rider):
    s, d = h.shape
    tm = s // PROJ_ROW_TILES
    assert rider[2] <= PROJ_ROW_TILES * 3
    dil = ATTN_GROUPS[gi][1]
    tn = ATTN_OUT_WIDTH
    off = col0 // tn + gi
    n_groups = len(ATTN_GROUPS)
    r_in, r_out, r_shape = _rider_specs(rider, layer, 3)
    return pl.pallas_call(
        functools.partial(_proj_attn_kernel, dil=dil, n_chunks=rider[2]),
        grid=(s // tm, 3),
        in_specs=[pl.BlockSpec((tm, d), lambda i, j: (i, 0)),
                  pl.BlockSpec((None, d, tn), lambda i, j: (layer, 0, off + n_groups * j)), r_in],
        out_specs=[pl.BlockSpec((HEADS_PER_GROUP, dil, tm // dil, LANES), lambda i, j: (j, 0, i, 0)),
                   r_out],
        out_shape=[jax.ShapeDtypeStruct((3 * HEADS_PER_GROUP, dil, s // dil, LANES), BF16), r_shape],
        scratch_shapes=[pltpu.VMEM((HEADS_PER_GROUP, tm, LANES), F32)],
        compiler_params=_params("arbitrary", "arbitrary"),
        name=f"proj_attn_g{gi}",
    )(h, w, rider[0])


def _hgrn_constants():
    c = CHUNK
    t = np.arange(c)[:, None]
    s = np.arange(c)[None, :]
    mats = [s <= t]
    masks = []
    for lvl in range(N_LEVELS):
        m = c >> (lvl + 1)
        p = (t // (2 * m)) * (2 * m) + m - 1
        if m < MXU_LEVEL_BELOW:
            mats.append(((t > p) & (s > p) & (s <= t)) | ((t <= p) & (s > t) & (s <= p)))
        masks.append((t // (2 * m) == s // (2 * m)) & (t % (2 * m) >= m) & (s % (2 * m) < m))
    w = np.stack(mats).astype(np.float32)
    k = np.stack(masks).astype(np.float32)
    w = np.stack([w, w[:, ::-1, ::-1]]).reshape(2, len(mats) * c, c)
    w = np.concatenate([w, w], axis=2)
    k = np.stack([k, k[:, ::-1, ::-1]])
    k = np.concatenate([k, k], axis=3)
    return jnp.asarray(w, BF16), jnp.asarray(k, BF16)


def _block_diag(x):
    zero = jnp.zeros((x.shape[0], LANES), x.dtype)
    return jnp.concatenate([jnp.concatenate([x[:, :LANES], zero], axis=1),
                            jnp.concatenate([zero, x[:, LANES:]], axis=1)], axis=0)


def _block_diag_t(x):
    zero = jnp.zeros((LANES, x.shape[1]), x.dtype)
    return jnp.concatenate([jnp.concatenate([x[:LANES], zero], axis=1),
                            jnp.concatenate([zero, x[LANES:]], axis=1)], axis=0)


def _hgrn_chunk(q, z, v, lb, wmat, m_ref, st, backward):
    half = 0.5 * (1.0 - lb)
    ct = half * jnp.tanh(0.5 * z)
    kk = half - ct
    logf = jnp.log2(jnp.maximum((lb + half) + ct, MIN_FORGET))
    hi = logf.astype(BF16)
    lo = (logf - hi.astype(F32)).astype(BF16)
    dec = jnp.dot(wmat, jnp.concatenate([hi, lo], axis=0), preferred_element_type=F32)
    b = dec[0:CHUNK]
    total = b[0:1] if backward else b[CHUNK - 1:CHUNK]
    q_in = (q * jnp.exp2(b)).astype(BF16)
    k_out = (kk * jnp.exp2(total - b)).astype(BF16)
    vb = v.astype(BF16)
    qb = q.astype(BF16)
    kb = kk.astype(BF16)
    o = jnp.dot(q_in, _block_diag_t(st.astype(BF16).T), preferred_element_type=F32)
    qk = q * kk
    diag = jnp.concatenate(
        [jnp.broadcast_to(jnp.sum(qk[:, :LANES], axis=-1, keepdims=True), (CHUNK, LANES)),
         jnp.broadcast_to(jnp.sum(qk[:, LANES:], axis=-1, keepdims=True), (CHUNK, LANES))], axis=1)
    o = o + diag * v
    a = None
    for lvl in range(N_LEVELS):
        m = CHUNK >> (lvl + 1)
        if m >= MXU_LEVEL_BELOW:
            parts = []
            for j in range(CHUNK // (2 * m)):
                lo_rows = b[2 * m * j:2 * m * j + m]
                hi_rows = b[2 * m * j + m:2 * m * (j + 1)]
                if backward:
                    piv = b[2 * m * j + m:2 * m * j + m + 1]
                    parts += [lo_rows - piv, piv - hi_rows]
                else:
                    piv = b[2 * m * j + m - 1:2 * m * j + m]
                    parts += [piv - lo_rows, hi_rows - piv]
            nl = jnp.concatenate(parts, axis=0)
        else:
            i = 1 + lvl - (N_LEVELS - N_MXU_LEVELS)
            nl = dec[i * CHUNK:(i + 1) * CHUNK]
        el = jnp.exp2(nl).astype(BF16)
        sc = jnp.dot(qb * el, _block_diag_t((kb * el).T), preferred_element_type=F32)
        sc = sc.astype(BF16) * m_ref[lvl]
        a = sc if a is None else a + sc
    o = o + jnp.dot(a, _block_diag(vb), preferred_element_type=F32)
    v_rows = jnp.concatenate([vb[:, :LANES], vb[:, LANES:]], axis=0)
    st_new = st * jnp.exp2(total) + lax.dot_general(v_rows, _block_diag(k_out), _TN,
                                                    preferred_element_type=F32)
    return o, st_new


def _hgrn_kernel(qf_ref, zf_ref, vf_ref, qb_ref, zb_ref, vb_ref, lb_ref, w_ref, m_ref,
                 of_ref, ob_ref, stf_ref, stb_ref):
    @pl.when(pl.program_id(0) == 0)
    def _():
        stf_ref[...] = jnp.zeros_like(stf_ref)
        stb_ref[...] = jnp.zeros_like(stb_ref)

    n_sub = qf_ref.shape[1] // CHUNK

    def sub_chunk(t, carry):
        rf = pl.ds(pl.multiple_of(t * CHUNK, CHUNK), CHUNK)
        rb = pl.ds(pl.multiple_of((n_sub - 1 - t) * CHUNK, CHUNK), CHUNK)
        for p in range(HGRN_HEADS // 2):
            o, st = _hgrn_chunk(qf_ref[p, rf, :], zf_ref[p, rf, :], vf_ref[p, rf, :], lb_ref[0, p],
                                w_ref[0], m_ref.at[0], stf_ref[p], False)
            of_ref[p, rf, :] = o
            stf_ref[p] = st
            o, st = _hgrn_chunk(qb_ref[p, rb, :], zb_ref[p, rb, :], vb_ref[p, rb, :], lb_ref[1, p],
                                w_ref[1], m_ref.at[1], stb_ref[p], True)
            ob_ref[p, rb, :] = o
            stb_ref[p] = st
        return carry

    lax.fori_loop(0, n_sub, sub_chunk, 0)


def _hgrn(proj_h, lb):
    s = proj_h.shape[1]
    n = s // HGRN_BLOCK
    pairs = HGRN_HEADS // 2
    wmat, masks = _hgrn_constants()
    hb = (pairs, HGRN_BLOCK, 2 * LANES)
    fwd = lambda slab: pl.BlockSpec(hb, lambda c: (slab, c, 0))
    bwd = lambda slab: pl.BlockSpec(hb, lambda c: (slab, n - 1 - c, 0))
    const = lambda a: pl.BlockSpec(a.shape, lambda c: (0,) * a.ndim)
    out_shape = jax.ShapeDtypeStruct((pairs, s, 2 * LANES), F32)
    state = pltpu.VMEM((pairs, LANES, 2 * LANES), F32)
    return pl.pallas_call(
        _hgrn_kernel,
        grid=(n,),
        in_specs=[fwd(0), fwd(1), fwd(3), bwd(0), bwd(2), bwd(3), const(lb), const(wmat), const(masks)],
        out_specs=[fwd(0), bwd(0)],
        out_shape=[out_shape, out_shape],
        scratch_shapes=[state, state],
        compiler_params=_params("arbitrary"),
        name="hgrn2_scan",
    )(proj_h, proj_h, proj_h, proj_h, proj_h, proj_h, lb, wmat, masks)


def _t5_bucket(rel):
    half = REL_BUCKETS // 2
    ret = (rel > 0).astype(np.int32) * half
    n = np.abs(rel)
    max_exact = half // 2
    large = max_exact + (np.log(np.maximum(n, 1) / max_exact)
                         / np.log(REL_MAX_DISTANCE / max_exact)
                         * (half - max_exact)).astype(np.int32)
    large = np.minimum(large, half - 1)
    return (ret + np.where(n < max_exact, n, large)).astype(np.int32)


def _attn_bias(rel_bias_table):
    span = 3 * ATTN_QBLK - 1
    rel = np.arange(span) - (ATTN_QBLK - 1) - ATTN_SIDE
    col = np.arange(2 * ATTN_QBLK)[None, :]
    band = np.abs(col - ATTN_SIDE - np.arange(ATTN_QBLK)[:, None]) <= ATTN_SIDE
    after_start, before_end = col >= ATTN_SIDE, col < 2 * ATTN_QBLK - ATTN_SIDE
    keep = [band, band & after_start, band & before_end, band & after_start & before_end]
    out = []
    for gi, (_, dil) in enumerate(ATTN_GROUPS):
        tab = rel_bias_table[:, gi * HEADS_PER_GROUP:(gi + 1) * HEADS_PER_GROUP].astype(F32)
        onehot = jnp.asarray(_t5_bucket(rel * dil)[:, None] == np.arange(REL_BUCKETS)[None, :], F32)
        vec = jnp.einsum("rb,bh->hr", onehot, tab, precision=lax.Precision.HIGHEST)
        period = span + 2
        vec = jnp.pad(vec, ((0, 0), (0, period - span)))
        flat = jnp.tile(vec, (1, ATTN_QBLK))[:, :ATTN_QBLK * (span + 1)]
        toep = flat.reshape(HEADS_PER_GROUP, ATTN_QBLK, span + 1)[:, :, ATTN_QBLK - 1:3 * ATTN_QBLK - 1]
        out.append(jnp.stack([jnp.where(jnp.asarray(k)[None], toep, NEG_INF) for k in keep], axis=1))
    return jnp.stack(out)


def _attn_kernel(q_ref, kp_ref, km_ref, kn_ref, vp_ref, vm_ref, vn_ref, bias_ref, *rest,
                 rows, dil, cpb, has_prev, final):
    rest = list(rest)
    op_ref, lp_ref = (rest.pop(0), rest.pop(0)) if has_prev else (None, None)
    o_ref = rest.pop(0)
    lse_ref = None if final else rest.pop(0)
    o_s, l_s = rest if rest else (o_ref, lse_ref)
    c = pl.program_id(2)
    i = pl.program_id(1)
    scale = LANES ** -0.5
    n_sub = rows // ATTN_QBLK

    def window(prev_ref, main_ref, next_ref, cc, j):
        lo, hi = j * ATTN_QBLK - ATTN_SIDE, (j + 1) * ATTN_QBLK + ATTN_SIDE
        parts = []
        if lo < 0:
            parts.append(prev_ref[cc])
        parts.append(main_ref[cc, max(lo, 0):min(hi, rows)])
        if hi > rows:
            parts.append(next_ref[cc])
        return parts[0] if len(parts) == 1 else jnp.concatenate(parts, axis=0)

    tiles = [(cc, j) for cc in range(cpb) for j in range(n_sub)]
    for t0 in range(0, len(tiles), ATTN_QK_BATCH):
        batch = tiles[t0:t0 + ATTN_QK_BATCH]
        scores = []
        for cc, j in batch:
            r0 = j * ATTN_QBLK
            variant = 0
            if j == 0:
                variant = variant + (i == 0).astype(jnp.int32)
            if j == n_sub - 1:
                variant = variant + 2 * (i == pl.num_programs(1) - 1).astype(jnp.int32)
            scores.append(lax.dot_general(q_ref[cc, r0:r0 + ATTN_QBLK],
                                          window(kp_ref, km_ref, kn_ref, cc, j), _NT,
                                          preferred_element_type=F32) * scale + bias_ref[variant])
        for (cc, j), s in zip(batch, scores):
            r0 = j * ATTN_QBLK
            m = jnp.max(s, axis=-1, keepdims=True)
            p = jnp.exp(s - m)
            l = jnp.sum(p, axis=-1, keepdims=True)
            o = jnp.dot(p.astype(BF16), window(vp_ref, vm_ref, vn_ref, cc, j),
                        preferred_element_type=F32) / l
            lse = jnp.broadcast_to(m + jnp.log(l), (ATTN_QBLK, LANES))
            if dil == 1:
                o_s[r0:r0 + ATTN_QBLK] = o
                l_s[r0:r0 + ATTN_QBLK] = lse
            else:
                start = r0 * dil + c * cpb + cc
                o_s[pl.ds(start, ATTN_QBLK, stride=dil), :] = o
                l_s[pl.ds(start, ATTN_QBLK, stride=dil), :] = lse

    if has_prev:
        @pl.when(c == dil // cpb - 1)
        def _():
            def fold(t, carry):
                r = pl.ds(pl.multiple_of(t * ATTN_FOLD_ROWS, ATTN_FOLD_ROWS), ATTN_FOLD_ROWS)
                lp, lc = lp_ref[r, :], l_s[r, :]
                m = jnp.maximum(lp, lc)
                wp, wc = jnp.exp(lp - m), jnp.exp(lc - m)
                den = wp + wc
                o = (wp * op_ref[r, :] + wc * o_s[r, :]) / den
                o_ref[r, :] = o.astype(o_ref.dtype)
                if not final:
                    lse_ref[r, :] = m + jnp.log(den)
                return carry

            lax.fori_loop(0, rows * dil // ATTN_FOLD_ROWS, fold, 0)


def _attn_group(qkv, bias, gi, prev):
    dil, length = qkv.shape[1], qkv.shape[2]
    s = dil * length
    cpb = min(ATTN_RESIDUES, dil)
    rows = min(ATTN_TILES * ATTN_QBLK // cpb, length)
    nblk = length // rows
    sub = rows // ATTN_SIDE
    last = length // ATTN_SIDE - 1
    hp = HEADS_PER_GROUP
    final = gi == len(ATTN_GROUPS) - 1
    has_prev = prev is not None
    assert has_prev or dil == 1

    def main(kind):
        return pl.BlockSpec((None, cpb, rows, LANES), lambda g, i, c: (kind * hp + g, c, i, 0))

    def prev_blk(kind):
        return pl.BlockSpec((None, cpb, ATTN_SIDE, LANES),
                            lambda g, i, c: (kind * hp + g, c, jnp.maximum(i * sub - 1, 0), 0))

    def next_blk(kind):
        return pl.BlockSpec((None, cpb, ATTN_SIDE, LANES),
                            lambda g, i, c: (kind * hp + g, c, jnp.minimum((i + 1) * sub, last), 0))

    run_spec = pl.BlockSpec((None, rows * dil, LANES), lambda g, i, c: (g, i, 0))
    run_shape = jax.ShapeDtypeStruct((hp, s, LANES), F32)
    in_specs = [main(0), prev_blk(1), main(1), next_blk(1), prev_blk(2), main(2), next_blk(2),
                pl.BlockSpec((None, None, 4, ATTN_QBLK, 2 * ATTN_QBLK), lambda g, i, c: (gi, g, 0, 0, 0))]
    args = [qkv] * 7 + [bias]
    if has_prev:
        in_specs += [run_spec, run_spec]
        args += list(prev)
    if final:
        out_specs = pl.BlockSpec((rows * dil, LANES), lambda g, i, c: (i, g))
        out_shape = jax.ShapeDtypeStruct((s, hp * LANES), BF16)
    else:
        out_specs = [run_spec, run_spec]
        out_shape = [run_shape, run_shape]
    scratch = []
    if has_prev:
        scratch = [pltpu.VMEM((rows * dil, LANES), F32), pltpu.VMEM((rows * dil, LANES), F32)]
    return pl.pallas_call(
        functools.partial(_attn_kernel, rows=rows, dil=dil, cpb=cpb, has_prev=has_prev, final=final),
        grid=(hp, nblk, dil // cpb),
        in_specs=in_specs,
        out_specs=out_specs,
        out_shape=out_shape,
        scratch_shapes=scratch,
        compiler_params=_params("parallel", "arbitrary", "arbitrary"),
        name=f"dilated_attn_g{gi}",
    )(*args)


def _merge_kernel(of_ref, ob_ref, g_ref, oa_ref, gate_ref, x_ref, pa_ref, pb_ref, wo_ref, hw_ref,
                  nw_ref, xo_ref, ho_ref, hs):
    for h in range(HGRN_HEADS):
        lanes = slice((h % 2) * LANES, (h % 2 + 1) * LANES)
        o = of_ref[h // 2, :, lanes] + ob_ref[h // 2, :, lanes]
        o = o * lax.rsqrt(jnp.mean(o * o, axis=-1, keepdims=True) + NORM_EPS)
        g = g_ref[h // 2, :, lanes]
        hs[:, h * LANES:(h + 1) * LANES] = (o * hw_ref[h] * (g * _sigmoid(g))).astype(BF16)
    d = x_ref.shape[1]
    a = jnp.dot(hs[...], pa_ref[...], preferred_element_type=F32)
    b = jnp.dot(oa_ref[...], pb_ref[...], preferred_element_type=F32)
    merged = _sigmoid(gate_ref[:, :d]) * a.astype(BF16) + _sigmoid(gate_ref[:, d:]) * b.astype(BF16)
    xn = x_ref[...] + jnp.dot(merged, wo_ref[...], preferred_element_type=F32)
    xo_ref[...] = xn
    ho_ref[...] = _rms(xn, nw_ref[...]).astype(ho_ref.dtype)


def _merge(o_hgrn, proj_h, o_attn, gates, x, pa, pb, wo, hgrn_norm_w, norm_w, tm=256):
    s, d = x.shape
    pairs = HGRN_HEADS // 2
    hspec = pl.BlockSpec((pairs, tm, 2 * LANES), lambda i: (0, i, 0))

    def wspec(rows):
        return pl.BlockSpec((rows, d), lambda i: (0, 0))

    return pl.pallas_call(
        _merge_kernel,
        grid=(s // tm,),
        in_specs=[hspec, hspec,
                  pl.BlockSpec((pairs, tm, 2 * LANES), lambda i: (4, i, 0)),
                  pl.BlockSpec((tm, ATTN_OUT_WIDTH), lambda i: (i, 0)),
                  pl.BlockSpec((tm, 2 * d), lambda i: (i, 0)),
                  pl.BlockSpec((tm, d), lambda i: (i, 0)),
                  wspec(HGRN_WIDTH), wspec(ATTN_OUT_WIDTH), wspec(d),
                  pl.BlockSpec((HGRN_HEADS, 1, LANES), lambda i: (0, 0, 0)),
                  pl.BlockSpec((1, d), lambda i: (0, 0))],
        out_specs=[pl.BlockSpec((tm, d), lambda i: (i, 0)),
                   pl.BlockSpec((tm, d), lambda i: (i, 0))],
        out_shape=[jax.ShapeDtypeStruct((s, d), F32), jax.ShapeDtypeStruct((s, d), BF16)],
        scratch_shapes=[pltpu.VMEM((tm, HGRN_WIDTH), BF16)],
        compiler_params=_params("parallel"),
        name="merge_out_proj",
    )(o_hgrn[0], o_hgrn[1], proj_h, o_attn, gates, x, pa, pb, wo,
      hgrn_norm_w.reshape(HGRN_HEADS, 1, LANES), norm_w.reshape(1, d))


def _mlp_kernel(h_ref, x_ref, wu_ref, wd_ref, nw_ref, xo_ref, ho_ref, acc):
    f = pl.program_id(1)

    @pl.when(f == 0)
    def _():
        acc[...] = x_ref[...]

    u = jnp.maximum(jnp.dot(h_ref[...], wu_ref[...], preferred_element_type=F32), 0.0)
    acc[...] += jnp.dot((u * u).astype(BF16), wd_ref[...], preferred_element_type=F32)

    @pl.when(f == pl.num_programs(1) - 1)
    def _():
        xn = acc[...]
        xo_ref[...] = xn
        ho_ref[...] = _rms(xn, nw_ref[...]).astype(ho_ref.dtype)


def _mlp(h, x, wu, wd, norm_w, h_dtype, tm=512, tf=1024):
    s, d = x.shape
    ff = wu.shape[1]
    return pl.pallas_call(
        _mlp_kernel,
        grid=(s // tm, ff // tf),
        in_specs=[pl.BlockSpec((tm, d), lambda i, f: (i, 0)),
                  pl.BlockSpec((tm, d), lambda i, f: (i, 0)),
                  pl.BlockSpec((d, tf), lambda i, f: (0, f)),
                  pl.BlockSpec((tf, d), lambda i, f: (f, 0)),
                  pl.BlockSpec((1, d), lambda i, f: (0, 0))],
        out_specs=[pl.BlockSpec((tm, d), lambda i, f: (i, 0)),
                   pl.BlockSpec((tm, d), lambda i, f: (i, 0))],
        out_shape=[jax.ShapeDtypeStruct((s, d), F32), jax.ShapeDtypeStruct((s, d), h_dtype)],
        scratch_shapes=[pltpu.VMEM((tm, d), F32)],
        compiler_params=_params("parallel", "arbitrary"),
        name="mlp_relu2",
    )(h, x, wu, wd, norm_w.reshape(1, d))


def _lower_bounds(logits):
    p = jax.nn.softmax(logits.astype(F32), axis=0)
    return jnp.cumsum(p, axis=0) - p[0:1]


def kernel(x, w_in, hgrn_lb_fwd, hgrn_lb_bwd, hgrn_norm_w, rel_bias_table, w_branch_hgrn,
           w_branch_attn, w_out, norm_mix_w, norm_mlp_w, w_up, w_down, final_norm_w):
    batch, s, d = x.shape
    depth = w_in.shape[0]
    assert batch == 1
    x = x.reshape(s, d)
    lb = jnp.stack([_lower_bounds(hgrn_lb_fwd), _lower_bounds(hgrn_lb_bwd)], axis=1)
    lb = lb.reshape(-1, 2, HGRN_HEADS // 2, 1, 2 * LANES)
    bias = _attn_bias(rel_bias_table)
    n_h = 5 * HGRN_WIDTH
    n_a = 3 * ATTN_WIDTH
    cast_rows = 256
    h = _norm(x, norm_mix_w[0])
    for l in range(depth):
        proj_h, wd_b = _proj(h, w_in, l, 0, n_h, F32, 2 * LANES, "proj_hgrn",
                             (w_down, 0, w_down.shape[1] // cast_rows))
        gates, wu_b = _proj(h, w_in, l, n_h + n_a, 2 * d, BF16, 0, "proj_gates",
                            (w_up, 1, w_up.shape[2] // cast_rows))
        o_hgrn = _hgrn(proj_h, lb[l])
        attn = None
        cast = []
        for gi, wt in enumerate((w_out, w_branch_hgrn, w_branch_attn)):
            qkv, wt_b = _proj_attn(h, w_in, l, n_h, gi, (wt, 0, wt.shape[1] // cast_rows))
            attn = _attn_group(qkv, bias, gi, attn)
            cast.append(wt_b)
        wo_b, pa_b, pb_b = cast
        x, h2 = _merge(o_hgrn, proj_h, attn, gates, x, pa_b, pb_b, wo_b, hgrn_norm_w[l],
                       norm_mlp_w[l])
        last = l == depth - 1
        x, h = _mlp(h2, x, wu_b, wd_b, final_norm_w if last else norm_mix_w[l + 1],
                    F32 if last else BF16)
    return h.reshape(batch, s, d)
```

```python
import functools

import jax
import jax.numpy as jnp
import numpy as np
from jax import lax
from jax.experimental import pallas as pl
from jax.experimental.pallas import tpu as pltpu

F32 = jnp.float32
BF16 = jnp.bfloat16

LANES = 128
HGRN_HEADS = 8
HGRN_WIDTH = HGRN_HEADS * LANES
CHUNK = 128
HGRN_BLOCK = 4 * CHUNK
N_LEVELS = 7
MXU_LEVEL_BELOW = 8
N_MXU_LEVELS = 3
MIN_FORGET = 1e-30
ATTN_GROUPS = ((128, 1), (512, 4), (2048, 16))
HEADS_PER_GROUP = 4
N_ATTN_HEADS = HEADS_PER_GROUP * len(ATTN_GROUPS)
ATTN_WIDTH = N_ATTN_HEADS * LANES
ATTN_OUT_WIDTH = HEADS_PER_GROUP * LANES
ATTN_QBLK = 128
ATTN_SIDE = 64
ATTN_TILES = 32
ATTN_QK_BATCH = 8
ATTN_RESIDUES = 8
ATTN_FOLD_ROWS = 256
REL_BUCKETS = 32
REL_MAX_DISTANCE = 1024
NORM_EPS = 1e-6
NEG_INF = -1e30
MERGE_HEAD_GROUP = 4
REGROUP_ROWS = 256
REGROUP_MATMUL_MIN_DIL = 8
PROJ_ROW_TILES = 4
VMEM_LIMIT = 56 * 1024 * 1024

_NT = (((1,), (1,)), ((), ()))
_TN = (((0,), (0,)), ((), ()))


def _params(*sem):
    return pltpu.CompilerParams(dimension_semantics=sem, vmem_limit_bytes=VMEM_LIMIT)


def _rms(x, w):
    return x * lax.rsqrt(jnp.mean(x * x, axis=-1, keepdims=True) + NORM_EPS) * w


def _sigmoid(x):
    return 0.5 * jnp.tanh(0.5 * x) + 0.5


def _norm_kernel(x_ref, w_ref, o_ref):
    o_ref[...] = _rms(x_ref[...], w_ref[...]).astype(o_ref.dtype)


def _norm(x, w, tm=512):
    s, d = x.shape
    return pl.pallas_call(
        _norm_kernel,
        grid=(s // tm,),
        in_specs=[pl.BlockSpec((tm, d), lambda i: (i, 0)),
                  pl.BlockSpec((1, d), lambda i: (0, 0))],
        out_specs=pl.BlockSpec((tm, d), lambda i: (i, 0)),
        out_shape=jax.ShapeDtypeStruct((s, d), BF16),
        compiler_params=_params("parallel"),
        name="rmsnorm_in",
    )(x, w.reshape(1, d))


def _cast_rider(cast_in, cast_out, n_chunks):
    step = pl.program_id(0) * pl.num_programs(1) + pl.program_id(1)

    @pl.when(step < n_chunks)
    def _():
        cast_out[...] = cast_in[...].astype(cast_out.dtype)


def _rider_specs(rider, layer, n_j):
    w, axis, n_chunks = rider
    r, c = w.shape[1:]

    def chunk(i, j):
        return jnp.minimum(i * n_j + j, n_chunks - 1)

    if axis == 0:
        blk = (r // n_chunks, c)
        in_spec = pl.BlockSpec((None,) + blk, lambda i, j: (layer, chunk(i, j), 0))
        out_spec = pl.BlockSpec(blk, lambda i, j: (chunk(i, j), 0))
    else:
        blk = (r, c // n_chunks)
        in_spec = pl.BlockSpec((None,) + blk, lambda i, j: (layer, 0, chunk(i, j)))
        out_spec = pl.BlockSpec(blk, lambda i, j: (0, chunk(i, j)))
    return in_spec, out_spec, jax.ShapeDtypeStruct((r, c), BF16)


def _proj_kernel(x_ref, w_ref, cast_in, o_ref, cast_out, *, slabs, n_chunks):
    _cast_rider(cast_in, cast_out, n_chunks)
    acc = jnp.dot(x_ref[...], w_ref[...].astype(BF16), preferred_element_type=F32)
    if slabs:
        width = o_ref.shape[-1]
        for j in range(slabs):
            o_ref[j] = acc[:, j * width:(j + 1) * width].astype(o_ref.dtype)
    else:
        o_ref[...] = acc.astype(o_ref.dtype)


def _proj(h, w, layer, col0, n, out_dtype, slab_width, name, rider, tn=512):
    s, d = h.shape
    tm = s // PROJ_ROW_TILES
    off = col0 // tn
    n_j = n // tn
    assert rider[2] <= PROJ_ROW_TILES * n_j
    if slab_width:
        slabs = tn // slab_width
        out_shape = jax.ShapeDtypeStruct((n // slab_width, s, slab_width), out_dtype)
        out_spec = pl.BlockSpec((slabs, tm, slab_width), lambda i, j: (j, i, 0))
    else:
        slabs = 0
        out_shape = jax.ShapeDtypeStruct((s, n), out_dtype)
        out_spec = pl.BlockSpec((tm, tn), lambda i, j: (i, j))
    r_in, r_out, r_shape = _rider_specs(rider, layer, n_j)
    return pl.pallas_call(
        functools.partial(_proj_kernel, slabs=slabs, n_chunks=rider[2]),
        grid=(s // tm, n_j),
        in_specs=[pl.BlockSpec((tm, d), lambda i, j: (i, 0)),
                  pl.BlockSpec((None, d, tn), lambda i, j: (layer, 0, j + off)), r_in],
        out_specs=[out_spec, r_out],
        out_shape=[out_shape, r_shape],
        compiler_params=_params("arbitrary", "arbitrary"),
        name=name,
    )(h, w, rider[0])


def _regroup_matrix(dil):
    r = REGROUP_ROWS // dil
    out_row = np.arange(REGROUP_ROWS)
    src = (out_row % r) * dil + out_row // r
    return jnp.asarray(src[:, None] == np.arange(REGROUP_ROWS)[None, :], BF16)


def _proj_attn_kernel(x_ref, w_ref, cast_in, perm_ref, o_ref, cast_out, scr, *, dil, n_chunks):
    _cast_rider(cast_in, cast_out, n_chunks)
    acc = jnp.dot(x_ref[...], w_ref[...].astype(BF16), preferred_element_type=F32)
    if dil == 1:
        for hh in range(HEADS_PER_GROUP):
            o_ref[hh, 0] = acc[:, hh * LANES:(hh + 1) * LANES].astype(o_ref.dtype)
    elif dil < REGROUP_MATMUL_MIN_DIL:
        rows = scr.shape[1] // dil
        for hh in range(HEADS_PER_GROUP):
            scr[hh] = acc[:, hh * LANES:(hh + 1) * LANES]
        for hh in range(HEADS_PER_GROUP):
            for c in range(dil):
                o_ref[hh, c] = scr[hh, pl.ds(c, rows, stride=dil), :].astype(o_ref.dtype)
    else:
        r = REGROUP_ROWS // dil
        rounded = acc.astype(BF16)
        for b in range(acc.shape[0] // REGROUP_ROWS):
            blk = jnp.dot(perm_ref[...], rounded[b * REGROUP_ROWS:(b + 1) * REGROUP_ROWS],
                          preferred_element_type=F32).astype(o_ref.dtype)
            for hh in range(HEADS_PER_GROUP):
                for c in range(dil):
                    o_ref[hh, c, b * r:(b + 1) * r, :] = blk[c * r:(c + 1) * r,
                                                             hh * LANES:(hh + 1) * LANES]


def _proj_attn(h, w, layer, col0, gi, rider):
    s, d = h.shape
    tm = s // PROJ_ROW_TILES
    assert rider[2] <= PROJ_ROW_TILES * 3
    dil = ATTN_GROUPS[gi][1]
    tn = ATTN_OUT_WIDTH
    off = col0 // tn + gi
    n_groups = len(ATTN_GROUPS)
    r_in, r_out, r_shape = _rider_specs(rider, layer, 3)
    perm = _regroup_matrix(dil)
    return pl.pallas_call(
        functools.partial(_proj_attn_kernel, dil=dil, n_chunks=rider[2]),
        grid=(s // tm, 3),
        in_specs=[pl.BlockSpec((tm, d), lambda i, j: (i, 0)),
                  pl.BlockSpec((None, d, tn), lambda i, j: (layer, 0, off + n_groups * j)), r_in,
                  pl.BlockSpec(perm.shape, lambda i, j: (0, 0))],
        out_specs=[pl.BlockSpec((HEADS_PER_GROUP, dil, tm // dil, LANES), lambda i, j: (j, 0, i, 0)),
                   r_out],
        out_shape=[jax.ShapeDtypeStruct((3 * HEADS_PER_GROUP, dil, s // dil, LANES), BF16), r_shape],
        scratch_shapes=[pltpu.VMEM((HEADS_PER_GROUP, tm, LANES), F32)],
        compiler_params=_params("arbitrary", "arbitrary"),
        name=f"proj_attn_g{gi}",
    )(h, w, rider[0], perm)


def _hgrn_constants():
    c = CHUNK
    t = np.arange(c)[:, None]
    s = np.arange(c)[None, :]
    mats = [s <= t]
    masks = []
    for lvl in range(N_LEVELS):
        m = c >> (lvl + 1)
        p = (t // (2 * m)) * (2 * m) + m - 1
        if m < MXU_LEVEL_BELOW:
            mats.append(((t > p) & (s > p) & (s <= t)) | ((t <= p) & (s > t) & (s <= p)))
        masks.append((t // (2 * m) == s // (2 * m)) & (t % (2 * m) >= m) & (s % (2 * m) < m))
    w = np.stack(mats).astype(np.float32)
    k = np.stack(masks).astype(np.float32)
    w = np.stack([w, w[:, ::-1, ::-1]]).reshape(2, len(mats) * c, c)
    w = np.concatenate([w, w], axis=2)
    k = np.stack([k, k[:, ::-1, ::-1]])
    k = np.concatenate([k, k], axis=3)
    return jnp.asarray(w, BF16), jnp.asarray(k, BF16)


def _block_diag(x):
    zero = jnp.zeros((x.shape[0], LANES), x.dtype)
    return jnp.concatenate([jnp.concatenate([x[:, :LANES], zero], axis=1),
                            jnp.concatenate([zero, x[:, LANES:]], axis=1)], axis=0)


def _block_diag_t(x):
    zero = jnp.zeros((LANES, x.shape[1]), x.dtype)
    return jnp.concatenate([jnp.concatenate([x[:LANES], zero], axis=1),
                            jnp.concatenate([zero, x[LANES:]], axis=1)], axis=0)


def _hgrn_chunk(q, z, v, lb, wmat, m_ref, st, backward):
    half = 0.5 * (1.0 - lb)
    ct = half * jnp.tanh(0.5 * z)
    kk = half - ct
    logf = jnp.log2(jnp.maximum((lb + half) + ct, MIN_FORGET))
    hi = logf.astype(BF16)
    lo = (logf - hi.astype(F32)).astype(BF16)
    dec = jnp.dot(wmat, jnp.concatenate([hi, lo], axis=0), preferred_element_type=F32)
    b = dec[0:CHUNK]
    total = b[0:1] if backward else b[CHUNK - 1:CHUNK]
    q_in = (q * jnp.exp2(b)).astype(BF16)
    k_out = (kk * jnp.exp2(total - b)).astype(BF16)
    vb = v.astype(BF16)
    qb = q.astype(BF16)
    kb = kk.astype(BF16)
    o = jnp.dot(q_in, _block_diag_t(st.astype(BF16).T), preferred_element_type=F32)
    qk = q * kk
    diag = jnp.concatenate(
        [jnp.broadcast_to(jnp.sum(qk[:, :LANES], axis=-1, keepdims=True), (CHUNK, LANES)),
         jnp.broadcast_to(jnp.sum(qk[:, LANES:], axis=-1, keepdims=True), (CHUNK, LANES))], axis=1)
    o = o + diag * v
    a = None
    for lvl in range(N_LEVELS):
        m = CHUNK >> (lvl + 1)
        if m >= MXU_LEVEL_BELOW:
            parts = []
            for j in range(CHUNK // (2 * m)):
                lo_rows = b[2 * m * j:2 * m * j + m]
                hi_rows = b[2 * m * j + m:2 * m * (j + 1)]
                if backward:
                    piv = b[2 * m * j + m:2 * m * j + m + 1]
                    parts += [lo_rows - piv, piv - hi_rows]
                else:
                    piv = b[2 * m * j + m - 1:2 * m * j + m]
                    parts += [piv - lo_rows, hi_rows - piv]
            nl = jnp.concatenate(parts, axis=0)
        else:
            i = 1 + lvl - (N_LEVELS - N_MXU_LEVELS)
            nl = dec[i * CHUNK:(i + 1) * CHUNK]
        el = jnp.exp2(nl).astype(BF16)
        sc = jnp.dot(qb * el, _block_diag_t((kb * el).T), preferred_element_type=F32)
        sc = sc.astype(BF16) * m_ref[lvl]
        a = sc if a is None else a + sc
    o = o + jnp.dot(a, _block_diag(vb), preferred_element_type=F32)
    v_rows = jnp.concatenate([vb[:, :LANES], vb[:, LANES:]], axis=0)
    st_new = st * jnp.exp2(total) + lax.dot_general(v_rows, _block_diag(k_out), _TN,
                                                    preferred_element_type=F32)
    return o, st_new


def _hgrn_kernel(qf_ref, zf_ref, vf_ref, qb_ref, zb_ref, vb_ref, lb_ref, w_ref, m_ref,
                 of_ref, ob_ref, stf_ref, stb_ref):
    @pl.when(pl.program_id(0) == 0)
    def _():
        stf_ref[...] = jnp.zeros_like(stf_ref)
        stb_ref[...] = jnp.zeros_like(stb_ref)

    n_sub = qf_ref.shape[1] // CHUNK

    def sub_chunk(t, carry):
        rf = pl.ds(pl.multiple_of(t * CHUNK, CHUNK), CHUNK)
        rb = pl.ds(pl.multiple_of((n_sub - 1 - t) * CHUNK, CHUNK), CHUNK)
        for p in range(HGRN_HEADS // 2):
            o, st = _hgrn_chunk(qf_ref[p, rf, :], zf_ref[p, rf, :], vf_ref[p, rf, :], lb_ref[0, p],
                                w_ref[0], m_ref.at[0], stf_ref[p], False)
            of_ref[p, rf, :] = o
            stf_ref[p] = st
            o, st = _hgrn_chunk(qb_ref[p, rb, :], zb_ref[p, rb, :], vb_ref[p, rb, :], lb_ref[1, p],
                                w_ref[1], m_ref.at[1], stb_ref[p], True)
            ob_ref[p, rb, :] = o
            stb_ref[p] = st
        return carry

    lax.fori_loop(0, n_sub, sub_chunk, 0)


def _hgrn(proj_h, lb):
    s = proj_h.shape[1]
    n = s // HGRN_BLOCK
    pairs = HGRN_HEADS // 2
    wmat, masks = _hgrn_constants()
    hb = (pairs, HGRN_BLOCK, 2 * LANES)
    fwd = lambda slab: pl.BlockSpec(hb, lambda c: (slab, c, 0))
    bwd = lambda slab: pl.BlockSpec(hb, lambda c: (slab, n - 1 - c, 0))
    const = lambda a: pl.BlockSpec(a.shape, lambda c: (0,) * a.ndim)
    out_shape = jax.ShapeDtypeStruct((pairs, s, 2 * LANES), F32)
    state = pltpu.VMEM((pairs, LANES, 2 * LANES), F32)
    return pl.pallas_call(
        _hgrn_kernel,
        grid=(n,),
        in_specs=[fwd(0), fwd(1), fwd(3), bwd(0), bwd(2), bwd(3), const(lb), const(wmat), const(masks)],
        out_specs=[fwd(0), bwd(0)],
        out_shape=[out_shape, out_shape],
        scratch_shapes=[state, state],
        compiler_params=_params("arbitrary"),
        name="hgrn2_scan",
    )(proj_h, proj_h, proj_h, proj_h, proj_h, proj_h, lb, wmat, masks)


def _t5_bucket(rel):
    half = REL_BUCKETS // 2
    ret = (rel > 0).astype(np.int32) * half
    n = np.abs(rel)
    max_exact = half // 2
    large = max_exact + (np.log(np.maximum(n, 1) / max_exact)
                         / np.log(REL_MAX_DISTANCE / max_exact)
                         * (half - max_exact)).astype(np.int32)
    large = np.minimum(large, half - 1)
    return (ret + np.where(n < max_exact, n, large)).astype(np.int32)


def _attn_bias(rel_bias_table):
    span = 3 * ATTN_QBLK - 1
    rel = np.arange(span) - (ATTN_QBLK - 1) - ATTN_SIDE
    col = np.arange(2 * ATTN_QBLK)[None, :]
    band = np.abs(col - ATTN_SIDE - np.arange(ATTN_QBLK)[:, None]) <= ATTN_SIDE
    after_start, before_end = col >= ATTN_SIDE, col < 2 * ATTN_QBLK - ATTN_SIDE
    keep = [band, band & after_start, band & before_end, band & after_start & before_end]
    out = []
    for gi, (_, dil) in enumerate(ATTN_GROUPS):
        tab = rel_bias_table[:, gi * HEADS_PER_GROUP:(gi + 1) * HEADS_PER_GROUP].astype(F32)
        onehot = jnp.asarray(_t5_bucket(rel * dil)[:, None] == np.arange(REL_BUCKETS)[None, :], F32)
        vec = jnp.einsum("rb,bh->hr", onehot, tab, precision=lax.Precision.HIGHEST)
        period = span + 2
        vec = jnp.pad(vec, ((0, 0), (0, period - span)))
        flat = jnp.tile(vec, (1, ATTN_QBLK))[:, :ATTN_QBLK * (span + 1)]
        toep = flat.reshape(HEADS_PER_GROUP, ATTN_QBLK, span + 1)[:, :, ATTN_QBLK - 1:3 * ATTN_QBLK - 1]
        out.append(jnp.stack([jnp.where(jnp.asarray(k)[None], toep, NEG_INF) for k in keep], axis=1))
    return jnp.stack(out)


def _attn_kernel(q_ref, kp_ref, km_ref, kn_ref, vp_ref, vm_ref, vn_ref, bias_ref, *rest,
                 rows, dil, cpb, has_prev, final):
    rest = list(rest)
    op_ref, lp_ref = (rest.pop(0), rest.pop(0)) if has_prev else (None, None)
    o_ref = rest.pop(0)
    lse_ref = None if final else rest.pop(0)
    o_s, l_s = rest if rest else (o_ref, lse_ref)
    c = pl.program_id(2)
    i = pl.program_id(1)
    scale = LANES ** -0.5
    n_sub = rows // ATTN_QBLK

    def window(prev_ref, main_ref, next_ref, cc, j):
        lo, hi = j * ATTN_QBLK - ATTN_SIDE, (j + 1) * ATTN_QBLK + ATTN_SIDE
        parts = []
        if lo < 0:
            parts.append(prev_ref[cc])
        parts.append(main_ref[cc, max(lo, 0):min(hi, rows)])
        if hi > rows:
            parts.append(next_ref[cc])
        return parts[0] if len(parts) == 1 else jnp.concatenate(parts, axis=0)

    tiles = [(cc, j) for cc in range(cpb) for j in range(n_sub)]
    for t0 in range(0, len(tiles), ATTN_QK_BATCH):
        batch = tiles[t0:t0 + ATTN_QK_BATCH]
        scores = []
        for cc, j in batch:
            r0 = j * ATTN_QBLK
            variant = 0
            if j == 0:
                variant = variant + (i == 0).astype(jnp.int32)
            if j == n_sub - 1:
                variant = variant + 2 * (i == pl.num_programs(1) - 1).astype(jnp.int32)
            scores.append(lax.dot_general(q_ref[cc, r0:r0 + ATTN_QBLK],
                                          window(kp_ref, km_ref, kn_ref, cc, j), _NT,
                                          preferred_element_type=F32) * scale + bias_ref[variant])
        for (cc, j), s in zip(batch, scores):
            r0 = j * ATTN_QBLK
            m = jnp.max(s, axis=-1, keepdims=True)
            p = jnp.exp(s - m)
            l = jnp.sum(p, axis=-1, keepdims=True)
            o = jnp.dot(p.astype(BF16), window(vp_ref, vm_ref, vn_ref, cc, j),
                        preferred_element_type=F32) / l
            lse = jnp.broadcast_to(m + jnp.log(l), (ATTN_QBLK, LANES))
            if dil == 1:
                o_s[r0:r0 + ATTN_QBLK] = o
                l_s[r0:r0 + ATTN_QBLK] = lse
            else:
                start = r0 * dil + c * cpb + cc
                o_s[pl.ds(start, ATTN_QBLK, stride=dil), :] = o
                l_s[pl.ds(start, ATTN_QBLK, stride=dil), :] = lse

    if has_prev:
        @pl.when(c == dil // cpb - 1)
        def _():
            def fold(t, carry):
                r = pl.ds(pl.multiple_of(t * ATTN_FOLD_ROWS, ATTN_FOLD_ROWS), ATTN_FOLD_ROWS)
                lp, lc = lp_ref[r, :], l_s[r, :]
                m = jnp.maximum(lp, lc)
                wp, wc = jnp.exp(lp - m), jnp.exp(lc - m)
                den = wp + wc
                o = (wp * op_ref[r, :] + wc * o_s[r, :]) / den
                o_ref[r, :] = o.astype(o_ref.dtype)
                if not final:
                    lse_ref[r, :] = m + jnp.log(den)
                return carry

            lax.fori_loop(0, rows * dil // ATTN_FOLD_ROWS, fold, 0)


def _attn_group(qkv, bias, gi, prev):
    dil, length = qkv.shape[1], qkv.shape[2]
    s = dil * length
    cpb = min(ATTN_RESIDUES, dil)
    rows = min(ATTN_TILES * ATTN_QBLK // cpb, length)
    nblk = length // rows
    sub = rows // ATTN_SIDE
    last = length // ATTN_SIDE - 1
    hp = HEADS_PER_GROUP
    final = gi == len(ATTN_GROUPS) - 1
    has_prev = prev is not None
    assert has_prev or dil == 1

    def main(kind):
        return pl.BlockSpec((None, cpb, rows, LANES), lambda g, i, c: (kind * hp + g, c, i, 0))

    def prev_blk(kind):
        return pl.BlockSpec((None, cpb, ATTN_SIDE, LANES),
                            lambda g, i, c: (kind * hp + g, c, jnp.maximum(i * sub - 1, 0), 0))

    def next_blk(kind):
        return pl.BlockSpec((None, cpb, ATTN_SIDE, LANES),
                            lambda g, i, c: (kind * hp + g, c, jnp.minimum((i + 1) * sub, last), 0))

    run_spec = pl.BlockSpec((None, rows * dil, LANES), lambda g, i, c: (g, i, 0))
    run_shape = jax.ShapeDtypeStruct((hp, s, LANES), F32)
    in_specs = [main(0), prev_blk(1), main(1), next_blk(1), prev_blk(2), main(2), next_blk(2),
                pl.BlockSpec((None, None, 4, ATTN_QBLK, 2 * ATTN_QBLK), lambda g, i, c: (gi, g, 0, 0, 0))]
    args = [qkv] * 7 + [bias]
    if has_prev:
        in_specs += [run_spec, run_spec]
        args += list(prev)
    if final:
        out_specs = pl.BlockSpec((rows * dil, LANES), lambda g, i, c: (i, g))
        out_shape = jax.ShapeDtypeStruct((s, hp * LANES), BF16)
    else:
        out_specs = [run_spec, run_spec]
        out_shape = [run_shape, run_shape]
    scratch = []
    if has_prev:
        scratch = [pltpu.VMEM((rows * dil, LANES), F32), pltpu.VMEM((rows * dil, LANES), F32)]
    return pl.pallas_call(
        functools.partial(_attn_kernel, rows=rows, dil=dil, cpb=cpb, has_prev=has_prev, final=final),
        grid=(hp, nblk, dil // cpb),
        in_specs=in_specs,
        out_specs=out_specs,
        out_shape=out_shape,
        scratch_shapes=scratch,
        compiler_params=_params("parallel", "arbitrary", "arbitrary"),
        name=f"dilated_attn_g{gi}",
    )(*args)


def _merge_kernel(of_ref, ob_ref, g_ref, oa_ref, gate_ref, x_ref, pa_ref, pb_ref, wo_ref, hw_ref,
                  nw_ref, xo_ref, ho_ref):
    d = x_ref.shape[1]
    b = jnp.dot(oa_ref[...], pb_ref[...], preferred_element_type=F32)
    a = None
    for h0 in range(0, HGRN_HEADS, MERGE_HEAD_GROUP):
        cols = []
        for h in range(h0, h0 + MERGE_HEAD_GROUP):
            lanes = slice((h % 2) * LANES, (h % 2 + 1) * LANES)
            o = of_ref[h // 2, :, lanes] + ob_ref[h // 2, :, lanes]
            o = o * lax.rsqrt(jnp.mean(o * o, axis=-1, keepdims=True) + NORM_EPS)
            g = g_ref[h // 2, :, lanes]
            cols.append((o * hw_ref[h] * (g * _sigmoid(g))).astype(BF16))
        part = jnp.dot(jnp.concatenate(cols, axis=1), pa_ref[h0 * LANES:(h0 + MERGE_HEAD_GROUP) * LANES, :],
                       preferred_element_type=F32)
        a = part if a is None else a + part
    merged = _sigmoid(gate_ref[:, :d]) * a.astype(BF16) + _sigmoid(gate_ref[:, d:]) * b.astype(BF16)
    xn = x_ref[...] + jnp.dot(merged, wo_ref[...], preferred_element_type=F32)
    xo_ref[...] = xn
    ho_ref[...] = _rms(xn, nw_ref[...]).astype(ho_ref.dtype)


def _merge(o_hgrn, proj_h, o_attn, gates, x, pa, pb, wo, hgrn_norm_w, norm_w, tm=256):
    s, d = x.shape
    pairs = HGRN_HEADS // 2
    hspec = pl.BlockSpec((pairs, tm, 2 * LANES), lambda i: (0, i, 0))

    def wspec(rows):
        return pl.BlockSpec((rows, d), lambda i: (0, 0))

    return pl.pallas_call(
        _merge_kernel,
        grid=(s // tm,),
        in_specs=[hspec, hspec,
                  pl.BlockSpec((pairs, tm, 2 * LANES), lambda i: (4, i, 0)),
                  pl.BlockSpec((tm, ATTN_OUT_WIDTH), lambda i: (i, 0)),
                  pl.BlockSpec((tm, 2 * d), lambda i: (i, 0)),
                  pl.BlockSpec((tm, d), lambda i: (i, 0)),
                  wspec(HGRN_WIDTH), wspec(ATTN_OUT_WIDTH), wspec(d),
                  pl.BlockSpec((HGRN_HEADS, 1, LANES), lambda i: (0, 0, 0)),
                  pl.BlockSpec((1, d), lambda i: (0, 0))],
        out_specs=[pl.BlockSpec((tm, d), lambda i: (i, 0)),
                   pl.BlockSpec((tm, d), lambda i: (i, 0))],
        out_shape=[jax.ShapeDtypeStruct((s, d), F32), jax.ShapeDtypeStruct((s, d), BF16)],
        compiler_params=_params("parallel"),
        name="merge_out_proj",
    )(o_hgrn[0], o_hgrn[1], proj_h, o_attn, gates, x, pa, pb, wo,
      hgrn_norm_w.reshape(HGRN_HEADS, 1, LANES), norm_w.reshape(1, d))


def _mlp_kernel(h_ref, x_ref, wu_ref, wd_ref, nw_ref, xo_ref, ho_ref, acc):
    f = pl.program_id(1)

    @pl.when(f == 0)
    def _():
        acc[...] = x_ref[...]

    u = jnp.maximum(jnp.dot(h_ref[...], wu_ref[...], preferred_element_type=F32), 0.0)
    acc[...] += jnp.dot((u * u).astype(BF16), wd_ref[...], preferred_element_type=F32)

    @pl.when(f == pl.num_programs(1) - 1)
    def _():
        xn = acc[...]
        xo_ref[...] = xn
        ho_ref[...] = _rms(xn, nw_ref[...]).astype(ho_ref.dtype)


def _mlp(h, x, wu, wd, norm_w, h_dtype, tm=512, tf=1024):
    s, d = x.shape
    ff = wu.shape[1]
    return pl.pallas_call(
        _mlp_kernel,
        grid=(s // tm, ff // tf),
        in_specs=[pl.BlockSpec((tm, d), lambda i, f: (i, 0)),
                  pl.BlockSpec((tm, d), lambda i, f: (i, 0)),
                  pl.BlockSpec((d, tf), lambda i, f: (0, f)),
                  pl.BlockSpec((tf, d), lambda i, f: (f, 0)),
                  pl.BlockSpec((1, d), lambda i, f: (0, 0))],
        out_specs=[pl.BlockSpec((tm, d), lambda i, f: (i, 0)),
                   pl.BlockSpec((tm, d), lambda i, f: (i, 0))],
        out_shape=[jax.ShapeDtypeStruct((s, d), F32), jax.ShapeDtypeStruct((s, d), h_dtype)],
        scratch_shapes=[pltpu.VMEM((tm, d), F32)],
        compiler_params=_params("parallel", "arbitrary"),
        name="mlp_relu2",
    )(h, x, wu, wd, norm_w.reshape(1, d))


def _lower_bounds(logits):
    p = jax.nn.softmax(logits.astype(F32), axis=0)
    return jnp.cumsum(p, axis=0) - p[0:1]


def kernel(x, w_in, hgrn_lb_fwd, hgrn_lb_bwd, hgrn_norm_w, rel_bias_table, w_branch_hgrn,
           w_branch_attn, w_out, norm_mix_w, norm_mlp_w, w_up, w_down, final_norm_w):
    batch, s, d = x.shape
    depth = w_in.shape[0]
    assert batch == 1
    x = x.reshape(s, d)
    lb = jnp.stack([_lower_bounds(hgrn_lb_fwd), _lower_bounds(hgrn_lb_bwd)], axis=1)
    lb = lb.reshape(-1, 2, HGRN_HEADS // 2, 1, 2 * LANES)
    bias = _attn_bias(rel_bias_table)
    n_h = 5 * HGRN_WIDTH
    n_a = 3 * ATTN_WIDTH
    cast_rows = 256
    h = _norm(x, norm_mix_w[0])
    for l in range(depth):
        proj_h, wd_b = _proj(h, w_in, l, 0, n_h, F32, 2 * LANES, "proj_hgrn",
                             (w_down, 0, w_down.shape[1] // cast_rows))
        gates, wu_b = _proj(h, w_in, l, n_h + n_a, 2 * d, BF16, 0, "proj_gates",
                            (w_up, 1, w_up.shape[2] // cast_rows))
        o_hgrn = _hgrn(proj_h, lb[l])
        attn = None
        cast = []
        for gi, wt in enumerate((w_out, w_branch_hgrn, w_branch_attn)):
            qkv, wt_b = _proj_attn(h, w_in, l, n_h, gi, (wt, 0, wt.shape[1] // cast_rows))
            attn = _attn_group(qkv, bias, gi, attn)
            cast.append(wt_b)
        wo_b, pa_b, pb_b = cast
        x, h2 = _merge(o_hgrn, proj_h, attn, gates, x, pa_b, pb_b, wo_b, hgrn_norm_w[l],
                       norm_mlp_w[l])
        last = l == depth - 1
        x, h = _mlp(h2, x, wu_b, wd_b, final_norm_w if last else norm_mix_w[l + 1],
                    F32 if last else BF16)
    return h.reshape(batch, s, d)
```

```python
import functools

import jax
import jax.numpy as jnp
import numpy as np
from jax import lax
from jax.experimental import pallas as pl
from jax.experimental.pallas import tpu as pltpu

F32 = jnp.float32
BF16 = jnp.bfloat16

LANES = 128
HGRN_HEADS = 8
HGRN_WIDTH = HGRN_HEADS * LANES
CHUNK = 128
HGRN_BLOCK = 4 * CHUNK
N_LEVELS = 7
MXU_LEVEL_BELOW = 8
N_MXU_LEVELS = 3
MIN_FORGET = 1e-30
ATTN_GROUPS = ((128, 1), (512, 4), (2048, 16))
HEADS_PER_GROUP = 4
N_ATTN_HEADS = HEADS_PER_GROUP * len(ATTN_GROUPS)
ATTN_WIDTH = N_ATTN_HEADS * LANES
ATTN_OUT_WIDTH = HEADS_PER_GROUP * LANES
ATTN_QBLK = 128
ATTN_SIDE = 64
ATTN_TILES = 32
ATTN_QK_BATCH = 8
ATTN_RESIDUES = 8
ATTN_FOLD_ROWS = 256
REL_BUCKETS = 32
REL_MAX_DISTANCE = 1024
NORM_EPS = 1e-6
NEG_INF = -1e30
MERGE_HEAD_GROUP = 4
REGROUP_ROWS = 256
REGROUP_MATMUL_MIN_DIL = 8
WEIGHT_RING_SLOTS = 3
PROJ_ROW_TILES = 4
VMEM_LIMIT = 56 * 1024 * 1024

_NT = (((1,), (1,)), ((), ()))
_TN = (((0,), (0,)), ((), ()))


def _params(*sem):
    return pltpu.CompilerParams(dimension_semantics=sem, vmem_limit_bytes=VMEM_LIMIT)


def _rms(x, w):
    return x * lax.rsqrt(jnp.mean(x * x, axis=-1, keepdims=True) + NORM_EPS) * w


def _sigmoid(x):
    return 0.5 * jnp.tanh(0.5 * x) + 0.5


def _norm_kernel(x_ref, w_ref, o_ref):
    o_ref[...] = _rms(x_ref[...], w_ref[...]).astype(o_ref.dtype)


def _norm(x, w, tm=512):
    s, d = x.shape
    return pl.pallas_call(
        _norm_kernel,
        grid=(s // tm,),
        in_specs=[pl.BlockSpec((tm, d), lambda i: (i, 0)),
                  pl.BlockSpec((1, d), lambda i: (0, 0))],
        out_specs=pl.BlockSpec((tm, d), lambda i: (i, 0)),
        out_shape=jax.ShapeDtypeStruct((s, d), BF16),
        compiler_params=_params("parallel"),
        name="rmsnorm_in",
    )(x, w.reshape(1, d))


def _cast_rider(cast_in, cast_out, n_chunks):
    step = pl.program_id(0) * pl.num_programs(1) + pl.program_id(1)

    @pl.when(step < n_chunks)
    def _():
        cast_out[...] = cast_in[...].astype(cast_out.dtype)


def _rider_specs(rider, layer, n_j):
    w, axis, n_chunks = rider
    r, c = w.shape[1:]

    def chunk(i, j):
        return jnp.minimum(i * n_j + j, n_chunks - 1)

    if axis == 0:
        blk = (r // n_chunks, c)
        in_spec = pl.BlockSpec((None,) + blk, lambda i, j: (layer, chunk(i, j), 0))
        out_spec = pl.BlockSpec(blk, lambda i, j: (chunk(i, j), 0))
    else:
        blk = (r, c // n_chunks)
        in_spec = pl.BlockSpec((None,) + blk, lambda i, j: (layer, 0, chunk(i, j)))
        out_spec = pl.BlockSpec(blk, lambda i, j: (0, chunk(i, j)))
    return in_spec, out_spec, jax.ShapeDtypeStruct((r, c), BF16)


def _proj_kernel(x_ref, w_hbm, cast_in, o_ref, cast_out, w_ring, sem, *, slabs, n_chunks, layer, col0):
    _cast_rider(cast_in, cast_out, n_chunks)
    n_j = pl.num_programs(1)
    step = pl.program_id(0) * n_j + pl.program_id(1)
    total = pl.num_programs(0) * n_j
    tn = w_ring.shape[2]

    def tile_copy(k):
        cols = pl.ds(pl.multiple_of(col0 + (k % n_j) * tn, tn), tn)
        slot = k % WEIGHT_RING_SLOTS
        return pltpu.make_async_copy(w_hbm.at[layer, :, cols], w_ring.at[slot], sem.at[slot])

    @pl.when(step == 0)
    def _():
        for k in range(WEIGHT_RING_SLOTS - 1):
            tile_copy(k).start()

    @pl.when(step + WEIGHT_RING_SLOTS - 1 < total)
    def _():
        tile_copy(step + WEIGHT_RING_SLOTS - 1).start()

    tile_copy(step).wait()
    acc = jnp.dot(x_ref[...], w_ring[step % WEIGHT_RING_SLOTS].astype(BF16),
                  preferred_element_type=F32)
    if slabs:
        width = o_ref.shape[-1]
        for j in range(slabs):
            o_ref[j] = acc[:, j * width:(j + 1) * width].astype(o_ref.dtype)
    else:
        o_ref[...] = acc.astype(o_ref.dtype)


def _proj(h, w, layer, col0, n, out_dtype, slab_width, name, rider, tn=512):
    s, d = h.shape
    tm = s // PROJ_ROW_TILES
    n_j = n // tn
    assert rider[2] <= PROJ_ROW_TILES * n_j
    if slab_width:
        slabs = tn // slab_width
        out_shape = jax.ShapeDtypeStruct((n // slab_width, s, slab_width), out_dtype)
        out_spec = pl.BlockSpec((slabs, tm, slab_width), lambda i, j: (j, i, 0))
    else:
        slabs = 0
        out_shape = jax.ShapeDtypeStruct((s, n), out_dtype)
        out_spec = pl.BlockSpec((tm, tn), lambda i, j: (i, j))
    r_in, r_out, r_shape = _rider_specs(rider, layer, n_j)
    assert PROJ_ROW_TILES * n_j >= WEIGHT_RING_SLOTS
    return pl.pallas_call(
        functools.partial(_proj_kernel, slabs=slabs, n_chunks=rider[2], layer=layer, col0=col0),
        grid=(s // tm, n_j),
        in_specs=[pl.BlockSpec((tm, d), lambda i, j: (i, 0)),
                  pl.BlockSpec(memory_space=pl.ANY), r_in],
        out_specs=[out_spec, r_out],
        out_shape=[out_shape, r_shape],
        scratch_shapes=[pltpu.VMEM((WEIGHT_RING_SLOTS, d, tn), w.dtype),
                        pltpu.SemaphoreType.DMA((WEIGHT_RING_SLOTS,))],
        compiler_params=_params("arbitrary", "arbitrary"),
        name=name,
    )(h, w, rider[0])


def _regroup_matrix(dil):
    r = REGROUP_ROWS // dil
    out_row = np.arange(REGROUP_ROWS)
    src = (out_row % r) * dil + out_row // r
    return jnp.asarray(src[:, None] == np.arange(REGROUP_ROWS)[None, :], BF16)


def _proj_attn_kernel(x_ref, w_ref, cast_in, perm_ref, o_ref, cast_out, scr, *, dil, n_chunks):
    _cast_rider(cast_in, cast_out, n_chunks)
    acc = jnp.dot(x_ref[...], w_ref[...].astype(BF16), preferred_element_type=F32)
    if dil == 1:
        for hh in range(HEADS_PER_GROUP):
            o_ref[hh, 0] = acc[:, hh * LANES:(hh + 1) * LANES].astype(o_ref.dtype)
    elif dil < REGROUP_MATMUL_MIN_DIL:
        rows = scr.shape[1] // dil
        for hh in range(HEADS_PER_GROUP):
            scr[hh] = acc[:, hh * LANES:(hh + 1) * LANES]
        for hh in range(HEADS_PER_GROUP):
            for c in range(dil):
                o_ref[hh, c] = scr[hh, pl.ds(c, rows, stride=dil), :].astype(o_ref.dtype)
    else:
        r = REGROUP_ROWS // dil
        rounded = acc.astype(BF16)
        for b in range(acc.shape[0] // REGROUP_ROWS):
            blk = jnp.dot(perm_ref[...], rounded[b * REGROUP_ROWS:(b + 1) * REGROUP_ROWS],
                          preferred_element_type=F32).astype(o_ref.dtype)
            for hh in range(HEADS_PER_GROUP):
                for c in range(dil):
                    o_ref[hh, c, b * r:(b + 1) * r, :] = blk[c * r:(c + 1) * r,
                                                             hh * LANES:(hh + 1) * LANES]


def _proj_attn(h, w, layer, col0, gi, rider):
    s, d = h.shape
    tm = s // PROJ_ROW_TILES
    assert rider[2] <= PROJ_ROW_TILES * 3
    dil = ATTN_GROUPS[gi][1]
    tn = ATTN_OUT_WIDTH
    off = col0 // tn + gi
    n_groups = len(ATTN_GROUPS)
    r_in, r_out, r_shape = _rider_specs(rider, layer, 3)
    perm = _regroup_matrix(dil)
    return pl.pallas_call(
        functools.partial(_proj_attn_kernel, dil=dil, n_chunks=rider[2]),
        grid=(s // tm, 3),
        in_specs=[pl.BlockSpec((tm, d), lambda i, j: (i, 0)),
                  pl.BlockSpec((None, d, tn), lambda i, j: (layer, 0, off + n_groups * j)), r_in,
                  pl.BlockSpec(perm.shape, lambda i, j: (0, 0))],
        out_specs=[pl.BlockSpec((HEADS_PER_GROUP, dil, tm // dil, LANES), lambda i, j: (j, 0, i, 0)),
                   r_out],
        out_shape=[jax.ShapeDtypeStruct((3 * HEADS_PER_GROUP, dil, s // dil, LANES), BF16), r_shape],
        scratch_shapes=[pltpu.VMEM((HEADS_PER_GROUP, tm, LANES), F32)],
        compiler_params=_params("arbitrary", "arbitrary"),
        name=f"proj_attn_g{gi}",
    )(h, w, rider[0], perm)


def _hgrn_constants():
    c = CHUNK
    t = np.arange(c)[:, None]
    s = np.arange(c)[None, :]
    mats = [s <= t]
    masks = []
    for lvl in range(N_LEVELS):
        m = c >> (lvl + 1)
        p = (t // (2 * m)) * (2 * m) + m - 1
        if m < MXU_LEVEL_BELOW:
            mats.append(((t > p) & (s > p) & (s <= t)) | ((t <= p) & (s > t) & (s <= p)))
        masks.append((t // (2 * m) == s // (2 * m)) & (t % (2 * m) >= m) & (s % (2 * m) < m))
    w = np.stack(mats).astype(np.float32)
    k = np.stack(masks).astype(np.float32)
    w = np.stack([w, w[:, ::-1, ::-1]]).reshape(2, len(mats) * c, c)
    w = np.concatenate([w, w], axis=2)
    k = np.stack([k, k[:, ::-1, ::-1]])
    k = np.concatenate([k, k], axis=3)
    return jnp.asarray(w, BF16), jnp.asarray(k, BF16)


def _block_diag(x):
    zero = jnp.zeros((x.shape[0], LANES), x.dtype)
    return jnp.concatenate([jnp.concatenate([x[:, :LANES], zero], axis=1),
                            jnp.concatenate([zero, x[:, LANES:]], axis=1)], axis=0)


def _block_diag_t(x):
    zero = jnp.zeros((LANES, x.shape[1]), x.dtype)
    return jnp.concatenate([jnp.concatenate([x[:LANES], zero], axis=1),
                            jnp.concatenate([zero, x[LANES:]], axis=1)], axis=0)


def _hgrn_chunk(q, z, v, lb, wmat, m_ref, st, backward):
    half = 0.5 * (1.0 - lb)
    ct = half * jnp.tanh(0.5 * z)
    kk = half - ct
    logf = jnp.log2(jnp.maximum((lb + half) + ct, MIN_FORGET))
    hi = logf.astype(BF16)
    lo = (logf - hi.astype(F32)).astype(BF16)
    dec = jnp.dot(wmat, jnp.concatenate([hi, lo], axis=0), preferred_element_type=F32)
    b = dec[0:CHUNK]
    total = b[0:1] if backward else b[CHUNK - 1:CHUNK]
    q_in = (q * jnp.exp2(b)).astype(BF16)
    k_out = (kk * jnp.exp2(total - b)).astype(BF16)
    vb = v.astype(BF16)
    qb = q.astype(BF16)
    kb = kk.astype(BF16)
    o = jnp.dot(q_in, _block_diag_t(st.astype(BF16).T), preferred_element_type=F32)
    qk = q * kk
    diag = jnp.concatenate(
        [jnp.broadcast_to(jnp.sum(qk[:, :LANES], axis=-1, keepdims=True), (CHUNK, LANES)),
         jnp.broadcast_to(jnp.sum(qk[:, LANES:], axis=-1, keepdims=True), (CHUNK, LANES))], axis=1)
    o = o + diag * v
    a = None
    for lvl in range(N_LEVELS):
        m = CHUNK >> (lvl + 1)
        if m >= MXU_LEVEL_BELOW:
            parts = []
            for j in range(CHUNK // (2 * m)):
                lo_rows = b[2 * m * j:2 * m * j + m]
                hi_rows = b[2 * m * j + m:2 * m * (j + 1)]
                if backward:
                    piv = b[2 * m * j + m:2 * m * j + m + 1]
                    parts += [lo_rows - piv, piv - hi_rows]
                else:
                    piv = b[2 * m * j + m - 1:2 * m * j + m]
                    parts += [piv - lo_rows, hi_rows - piv]
            nl = jnp.concatenate(parts, axis=0)
        else:
            i = 1 + lvl - (N_LEVELS - N_MXU_LEVELS)
            nl = dec[i * CHUNK:(i + 1) * CHUNK]
        el = jnp.exp2(nl).astype(BF16)
        sc = jnp.dot(qb * el, _block_diag_t((kb * el).T), preferred_element_type=F32)
        sc = sc.astype(BF16) * m_ref[lvl]
        a = sc if a is None else a + sc
    o = o + jnp.dot(a, _block_diag(vb), preferred_element_type=F32)
    v_rows = jnp.concatenate([vb[:, :LANES], vb[:, LANES:]], axis=0)
    st_new = st * jnp.exp2(total) + lax.dot_general(v_rows, _block_diag(k_out), _TN,
                                                    preferred_element_type=F32)
    return o, st_new


def _hgrn_kernel(qf_ref, zf_ref, vf_ref, qb_ref, zb_ref, vb_ref, lb_ref, w_ref, m_ref,
                 of_ref, ob_ref, stf_ref, stb_ref):
    @pl.when(pl.program_id(0) == 0)
    def _():
        stf_ref[...] = jnp.zeros_like(stf_ref)
        stb_ref[...] = jnp.zeros_like(stb_ref)

    n_sub = qf_ref.shape[1] // CHUNK

    def sub_chunk(t, carry):
        rf = pl.ds(pl.multiple_of(t * CHUNK, CHUNK), CHUNK)
        rb = pl.ds(pl.multiple_of((n_sub - 1 - t) * CHUNK, CHUNK), CHUNK)
        for p in range(HGRN_HEADS // 2):
            o, st = _hgrn_chunk(qf_ref[p, rf, :], zf_ref[p, rf, :], vf_ref[p, rf, :], lb_ref[0, p],
                                w_ref[0], m_ref.at[0], stf_ref[p], False)
            of_ref[p, rf, :] = o
            stf_ref[p] = st
            o, st = _hgrn_chunk(qb_ref[p, rb, :], zb_ref[p, rb, :], vb_ref[p, rb, :], lb_ref[1, p],
                                w_ref[1], m_ref.at[1], stb_ref[p], True)
            ob_ref[p, rb, :] = o
            stb_ref[p] = st
        return carry

    lax.fori_loop(0, n_sub, sub_chunk, 0)


def _hgrn(proj_h, lb):
    s = proj_h.shape[1]
    n = s // HGRN_BLOCK
    pairs = HGRN_HEADS // 2
    wmat, masks = _hgrn_constants()
    hb = (pairs, HGRN_BLOCK, 2 * LANES)
    fwd = lambda slab: pl.BlockSpec(hb, lambda c: (slab, c, 0))
    bwd = lambda slab: pl.BlockSpec(hb, lambda c: (slab, n - 1 - c, 0))
    const = lambda a: pl.BlockSpec(a.shape, lambda c: (0,) * a.ndim)
    out_shape = jax.ShapeDtypeStruct((pairs, s, 2 * LANES), F32)
    state = pltpu.VMEM((pairs, LANES, 2 * LANES), F32)
    return pl.pallas_call(
        _hgrn_kernel,
        grid=(n,),
        in_specs=[fwd(0), fwd(1), fwd(3), bwd(0), bwd(2), bwd(3), const(lb), const(wmat), const(masks)],
        out_specs=[fwd(0), bwd(0)],
        out_shape=[out_shape, out_shape],
        scratch_shapes=[state, state],
        compiler_params=_params("arbitrary"),
        name="hgrn2_scan",
    )(proj_h, proj_h, proj_h, proj_h, proj_h, proj_h, lb, wmat, masks)


def _t5_bucket(rel):
    half = REL_BUCKETS // 2
    ret = (rel > 0).astype(np.int32) * half
    n = np.abs(rel)
    max_exact = half // 2
    large = max_exact + (np.log(np.maximum(n, 1) / max_exact)
                         / np.log(REL_MAX_DISTANCE / max_exact)
                         * (half - max_exact)).astype(np.int32)
    large = np.minimum(large, half - 1)
    return (ret + np.where(n < max_exact, n, large)).astype(np.int32)


def _attn_bias(rel_bias_table):
    span = 3 * ATTN_QBLK - 1
    rel = np.arange(span) - (ATTN_QBLK - 1) - ATTN_SIDE
    col = np.arange(2 * ATTN_QBLK)[None, :]
    band = np.abs(col - ATTN_SIDE - np.arange(ATTN_QBLK)[:, None]) <= ATTN_SIDE
    after_start, before_end = col >= ATTN_SIDE, col < 2 * ATTN_QBLK - ATTN_SIDE
    keep = [band, band & after_start, band & before_end, band & after_start & before_end]
    out = []
    for gi, (_, dil) in enumerate(ATTN_GROUPS):
        tab = rel_bias_table[:, gi * HEADS_PER_GROUP:(gi + 1) * HEADS_PER_GROUP].astype(F32)
        onehot = jnp.asarray(_t5_bucket(rel * dil)[:, None] == np.arange(REL_BUCKETS)[None, :], F32)
        vec = jnp.einsum("rb,bh->hr", onehot, tab, precision=lax.Precision.HIGHEST)
        period = span + 2
        vec = jnp.pad(vec, ((0, 0), (0, period - span)))
        flat = jnp.tile(vec, (1, ATTN_QBLK))[:, :ATTN_QBLK * (span + 1)]
        toep = flat.reshape(HEADS_PER_GROUP, ATTN_QBLK, span + 1)[:, :, ATTN_QBLK - 1:3 * ATTN_QBLK - 1]
        out.append(jnp.stack([jnp.where(jnp.asarray(k)[None], toep, NEG_INF) for k in keep], axis=1))
    return jnp.stack(out)


def _attn_kernel(q_ref, kp_ref, km_ref, kn_ref, vp_ref, vm_ref, vn_ref, bias_ref, *rest,
                 rows, dil, cpb, has_prev, final):
    rest = list(rest)
    op_ref, lp_ref = (rest.pop(0), rest.pop(0)) if has_prev else (None, None)
    o_ref = rest.pop(0)
    lse_ref = None if final else rest.pop(0)
    o_s, l_s = rest if rest else (o_ref, lse_ref)
    c = pl.program_id(2)
    i = pl.program_id(1)
    scale = LANES ** -0.5
    n_sub = rows // ATTN_QBLK

    def window(prev_ref, main_ref, next_ref, cc, j):
        lo, hi = j * ATTN_QBLK - ATTN_SIDE, (j + 1) * ATTN_QBLK + ATTN_SIDE
        parts = []
        if lo < 0:
            parts.append(prev_ref[cc])
        parts.append(main_ref[cc, max(lo, 0):min(hi, rows)])
        if hi > rows:
            parts.append(next_ref[cc])
        return parts[0] if len(parts) == 1 else jnp.concatenate(parts, axis=0)

    tiles = [(cc, j) for cc in range(cpb) for j in range(n_sub)]
    for t0 in range(0, len(tiles), ATTN_QK_BATCH):
        batch = tiles[t0:t0 + ATTN_QK_BATCH]
        scores = []
        for cc, j in batch:
            r0 = j * ATTN_QBLK
            variant = 0
            if j == 0:
                variant = variant + (i == 0).astype(jnp.int32)
            if j == n_sub - 1:
                variant = variant + 2 * (i == pl.num_programs(1) - 1).astype(jnp.int32)
            scores.append(lax.dot_general(q_ref[cc, r0:r0 + ATTN_QBLK],
                                          window(kp_ref, km_ref, kn_ref, cc, j), _NT,
                                          preferred_element_type=F32) * scale + bias_ref[variant])
        for (cc, j), s in zip(batch, scores):
            r0 = j * ATTN_QBLK
            m = jnp.max(s, axis=-1, keepdims=True)
            p = jnp.exp(s - m)
            l = jnp.sum(p, axis=-1, keepdims=True)
            o = jnp.dot(p.astype(BF16), window(vp_ref, vm_ref, vn_ref, cc, j),
                        preferred_element_type=F32) / l
            lse = jnp.broadcast_to(m + jnp.log(l), (ATTN_QBLK, LANES))
            if dil == 1:
                o_s[r0:r0 + ATTN_QBLK] = o
                l_s[r0:r0 + ATTN_QBLK] = lse
            else:
                start = r0 * dil + c * cpb + cc
                o_s[pl.ds(start, ATTN_QBLK, stride=dil), :] = o
                l_s[pl.ds(start, ATTN_QBLK, stride=dil), :] = lse

    if has_prev:
        @pl.when(c == dil // cpb - 1)
        def _():
            def fold(t, carry):
                r = pl.ds(pl.multiple_of(t * ATTN_FOLD_ROWS, ATTN_FOLD_ROWS), ATTN_FOLD_ROWS)
                lp, lc = lp_ref[r, :], l_s[r, :]
                m = jnp.maximum(lp, lc)
                wp, wc = jnp.exp(lp - m), jnp.exp(lc - m)
                den = wp + wc
                o = (wp * op_ref[r, :] + wc * o_s[r, :]) / den
                o_ref[r, :] = o.astype(o_ref.dtype)
                if not final:
                    lse_ref[r, :] = m + jnp.log(den)
                return carry

            lax.fori_loop(0, rows * dil // ATTN_FOLD_ROWS, fold, 0)


def _attn_group(qkv, bias, gi, prev):
    dil, length = qkv.shape[1], qkv.shape[2]
    s = dil * length
    cpb = min(ATTN_RESIDUES, dil)
    rows = min(ATTN_TILES * ATTN_QBLK // cpb, length)
    nblk = length // rows
    sub = rows // ATTN_SIDE
    last = length // ATTN_SIDE - 1
    hp = HEADS_PER_GROUP
    final = gi == len(ATTN_GROUPS) - 1
    has_prev = prev is not None
    assert has_prev or dil == 1

    def main(kind):
        return pl.BlockSpec((None, cpb, rows, LANES), lambda g, i, c: (kind * hp + g, c, i, 0))

    def prev_blk(kind):
        return pl.BlockSpec((None, cpb, ATTN_SIDE, LANES),
                            lambda g, i, c: (kind * hp + g, c, jnp.maximum(i * sub - 1, 0), 0))

    def next_blk(kind):
        return pl.BlockSpec((None, cpb, ATTN_SIDE, LANES),
                            lambda g, i, c: (kind * hp + g, c, jnp.minimum((i + 1) * sub, last), 0))

    run_spec = pl.BlockSpec((None, rows * dil, LANES), lambda g, i, c: (g, i, 0))
    run_shape = jax.ShapeDtypeStruct((hp, s, LANES), F32)
    in_specs = [main(0), prev_blk(1), main(1), next_blk(1), prev_blk(2), main(2), next_blk(2),
                pl.BlockSpec((None, None, 4, ATTN_QBLK, 2 * ATTN_QBLK), lambda g, i, c: (gi, g, 0, 0, 0))]
    args = [qkv] * 7 + [bias]
    if has_prev:
        in_specs += [run_spec, run_spec]
        args += list(prev)
    if final:
        out_specs = pl.BlockSpec((rows * dil, LANES), lambda g, i, c: (i, g))
        out_shape = jax.ShapeDtypeStruct((s, hp * LANES), BF16)
    else:
        out_specs = [run_spec, run_spec]
        out_shape = [run_shape, run_shape]
    scratch = []
    if has_prev:
        scratch = [pltpu.VMEM((rows * dil, LANES), F32), pltpu.VMEM((rows * dil, LANES), F32)]
    return pl.pallas_call(
        functools.partial(_attn_kernel, rows=rows, dil=dil, cpb=cpb, has_prev=has_prev, final=final),
        grid=(hp, nblk, dil // cpb),
        in_specs=in_specs,
        out_specs=out_specs,
        out_shape=out_shape,
        scratch_shapes=scratch,
        compiler_params=_params("parallel", "arbitrary", "arbitrary"),
        name=f"dilated_attn_g{gi}",
    )(*args)


def _merge_kernel(of_ref, ob_ref, g_ref, oa_ref, gate_ref, x_ref, pa_ref, pb_ref, wo_ref, hw_ref,
                  nw_ref, xo_ref, ho_ref):
    d = x_ref.shape[1]
    b = jnp.dot(oa_ref[...], pb_ref[...], preferred_element_type=F32)
    a = None
    for h0 in range(0, HGRN_HEADS, MERGE_HEAD_GROUP):
        cols = []
        for h in range(h0, h0 + MERGE_HEAD_GROUP):
            lanes = slice((h % 2) * LANES, (h % 2 + 1) * LANES)
            o = of_ref[h // 2, :, lanes] + ob_ref[h // 2, :, lanes]
            o = o * lax.rsqrt(jnp.mean(o * o, axis=-1, keepdims=True) + NORM_EPS)
            g = g_ref[h // 2, :, lanes]
            cols.append((o * hw_ref[h] * (g * _sigmoid(g))).astype(BF16))
        part = jnp.dot(jnp.concatenate(cols, axis=1), pa_ref[h0 * LANES:(h0 + MERGE_HEAD_GROUP) * LANES, :],
                       preferred_element_type=F32)
        a = part if a is None else a + part
    merged = _sigmoid(gate_ref[:, :d]) * a.astype(BF16) + _sigmoid(gate_ref[:, d:]) * b.astype(BF16)
    xn = x_ref[...] + jnp.dot(merged, wo_ref[...], preferred_element_type=F32)
    xo_ref[...] = xn
    ho_ref[...] = _rms(xn, nw_ref[...]).astype(ho_ref.dtype)


def _merge(o_hgrn, proj_h, o_attn, gates, x, pa, pb, wo, hgrn_norm_w, norm_w, tm=256):
    s, d = x.shape
    pairs = HGRN_HEADS // 2
    hspec = pl.BlockSpec((pairs, tm, 2 * LANES), lambda i: (0, i, 0))

    def wspec(rows):
        return pl.BlockSpec((rows, d), lambda i: (0, 0))

    return pl.pallas_call(
        _merge_kernel,
        grid=(s // tm,),
        in_specs=[hspec, hspec,
                  pl.BlockSpec((pairs, tm, 2 * LANES), lambda i: (4, i, 0)),
                  pl.BlockSpec((tm, ATTN_OUT_WIDTH), lambda i: (i, 0)),
                  pl.BlockSpec((tm, 2 * d), lambda i: (i, 0)),
                  pl.BlockSpec((tm, d), lambda i: (i, 0)),
                  wspec(HGRN_WIDTH), wspec(ATTN_OUT_WIDTH), wspec(d),
                  pl.BlockSpec((HGRN_HEADS, 1, LANES), lambda i: (0, 0, 0)),
                  pl.BlockSpec((1, d), lambda i: (0, 0))],
        out_specs=[pl.BlockSpec((tm, d), lambda i: (i, 0)),
                   pl.BlockSpec((tm, d), lambda i: (i, 0))],
        out_shape=[jax.ShapeDtypeStruct((s, d), F32), jax.ShapeDtypeStruct((s, d), BF16)],
        compiler_params=_params("parallel"),
        name="merge_out_proj",
    )(o_hgrn[0], o_hgrn[1], proj_h, o_attn, gates, x, pa, pb, wo,
      hgrn_norm_w.reshape(HGRN_HEADS, 1, LANES), norm_w.reshape(1, d))


def _mlp_kernel(h_ref, x_ref, wu_ref, wd_ref, nw_ref, xo_ref, ho_ref, acc):
    f = pl.program_id(1)

    @pl.when(f == 0)
    def _():
        acc[...] = x_ref[...]

    u = jnp.maximum(jnp.dot(h_ref[...], wu_ref[...], preferred_element_type=F32), 0.0)
    acc[...] += jnp.dot((u * u).astype(BF16), wd_ref[...], preferred_element_type=F32)

    @pl.when(f == pl.num_programs(1) - 1)
    def _():
        xn = acc[...]
        xo_ref[...] = xn
        ho_ref[...] = _rms(xn, nw_ref[...]).astype(ho_ref.dtype)


def _mlp(h, x, wu, wd, norm_w, h_dtype, tm=512, tf=1024):
    s, d = x.shape
    ff = wu.shape[1]
    return pl.pallas_call(
        _mlp_kernel,
        grid=(s // tm, ff // tf),
        in_specs=[pl.BlockSpec((tm, d), lambda i, f: (i, 0)),
                  pl.BlockSpec((tm, d), lambda i, f: (i, 0)),
                  pl.BlockSpec((d, tf), lambda i, f: (0, f)),
                  pl.BlockSpec((tf, d), lambda i, f: (f, 0)),
                  pl.BlockSpec((1, d), lambda i, f: (0, 0))],
        out_specs=[pl.BlockSpec((tm, d), lambda i, f: (i, 0)),
                   pl.BlockSpec((tm, d), lambda i, f: (i, 0))],
        out_shape=[jax.ShapeDtypeStruct((s, d), F32), jax.ShapeDtypeStruct((s, d), h_dtype)],
        scratch_shapes=[pltpu.VMEM((tm, d), F32)],
        compiler_params=_params("parallel", "arbitrary"),
        name="mlp_relu2",
    )(h, x, wu, wd, norm_w.reshape(1, d))


def _lower_bounds(logits):
    p = jax.nn.softmax(logits.astype(F32), axis=0)
    return jnp.cumsum(p, axis=0) - p[0:1]


def kernel(x, w_in, hgrn_lb_fwd, hgrn_lb_bwd, hgrn_norm_w, rel_bias_table, w_branch_hgrn,
           w_branch_attn, w_out, norm_mix_w, norm_mlp_w, w_up, w_down, final_norm_w):
    batch, s, d = x.shape
    depth = w_in.shape[0]
    assert batch == 1
    x = x.reshape(s, d)
    lb = jnp.stack([_lower_bounds(hgrn_lb_fwd), _lower_bounds(hgrn_lb_bwd)], axis=1)
    lb = lb.reshape(-1, 2, HGRN_HEADS // 2, 1, 2 * LANES)
    bias = _attn_bias(rel_bias_table)
    n_h = 5 * HGRN_WIDTH
    n_a = 3 * ATTN_WIDTH
    cast_rows = 256
    h = _norm(x, norm_mix_w[0])
    for l in range(depth):
        proj_h, wd_b = _proj(h, w_in, l, 0, n_h, F32, 2 * LANES, "proj_hgrn",
                             (w_down, 0, w_down.shape[1] // cast_rows))
        gates, wu_b = _proj(h, w_in, l, n_h + n_a, 2 * d, BF16, 0, "proj_gates",
                            (w_up, 1, w_up.shape[2] // cast_rows))
        o_hgrn = _hgrn(proj_h, lb[l])
        attn = None
        cast = []
        for gi, wt in enumerate((w_out, w_branch_hgrn, w_branch_attn)):
            qkv, wt_b = _proj_attn(h, w_in, l, n_h, gi, (wt, 0, wt.shape[1] // cast_rows))
            attn = _attn_group(qkv, bias, gi, attn)
            cast.append(wt_b)
        wo_b, pa_b, pb_b = cast
        x, h2 = _merge(o_hgrn, proj_h, attn, gates, x, pa_b, pb_b, wo_b, hgrn_norm_w[l],
                       norm_mlp_w[l])
        last = l == depth - 1
        x, h = _mlp(h2, x, wu_b, wd_b, final_norm_w if last else norm_mix_w[l + 1],
                    F32 if last else BF16)
    return h.reshape(batch, s, d)
```

```python
import functools

import jax
import jax.numpy as jnp
import numpy as np
from jax import lax
from jax.experimental import pallas as pl
from jax.experimental.pallas import tpu as pltpu

F32 = jnp.float32
BF16 = jnp.bfloat16

LANES = 128
HGRN_HEADS = 8
HGRN_WIDTH = HGRN_HEADS * LANES
CHUNK = 128
HGRN_BLOCK = 4 * CHUNK
N_LEVELS = 7
MXU_LEVEL_BELOW = 8
N_MXU_LEVELS = 3
MIN_FORGET = 1e-30
ATTN_GROUPS = ((128, 1), (512, 4), (2048, 16))
HEADS_PER_GROUP = 4
N_ATTN_HEADS = HEADS_PER_GROUP * len(ATTN_GROUPS)
ATTN_WIDTH = N_ATTN_HEADS * LANES
ATTN_OUT_WIDTH = HEADS_PER_GROUP * LANES
ATTN_QBLK = 128
ATTN_SIDE = 64
ATTN_TILES = 32
ATTN_QK_BATCH = 8
ATTN_RESIDUES = 8
ATTN_FOLD_ROWS = 256
REL_BUCKETS = 32
REL_MAX_DISTANCE = 1024
NORM_EPS = 1e-6
NEG_INF = -1e30
MERGE_HEAD_GROUP = 4
REGROUP_ROWS = 256
REGROUP_MATMUL_MIN_DIL = 8
WEIGHT_RING_SLOTS = 3
PROJ_ROW_TILES = 4
VMEM_LIMIT = 56 * 1024 * 1024
MLP_VMEM_LIMIT = 60 * 1024 * 1024

_NT = (((1,), (1,)), ((), ()))
_TN = (((0,), (0,)), ((), ()))


def _params(*sem):
    return pltpu.CompilerParams(dimension_semantics=sem, vmem_limit_bytes=VMEM_LIMIT)


def _rms(x, w):
    return x * lax.rsqrt(jnp.mean(x * x, axis=-1, keepdims=True) + NORM_EPS) * w


def _sigmoid(x):
    return 0.5 * jnp.tanh(0.5 * x) + 0.5


def _norm_kernel(x_ref, w_ref, o_ref):
    o_ref[...] = _rms(x_ref[...], w_ref[...]).astype(o_ref.dtype)


def _norm(x, w, tm=512):
    s, d = x.shape
    return pl.pallas_call(
        _norm_kernel,
        grid=(s // tm,),
        in_specs=[pl.BlockSpec((tm, d), lambda i: (i, 0)),
                  pl.BlockSpec((1, d), lambda i: (0, 0))],
        out_specs=pl.BlockSpec((tm, d), lambda i: (i, 0)),
        out_shape=jax.ShapeDtypeStruct((s, d), BF16),
        compiler_params=_params("parallel"),
        name="rmsnorm_in",
    )(x, w.reshape(1, d))


def _cast_rider(cast_in, cast_out, n_chunks):
    step = pl.program_id(0) * pl.num_programs(1) + pl.program_id(1)

    @pl.when(step < n_chunks)
    def _():
        cast_out[...] = cast_in[...].astype(cast_out.dtype)


def _rider_specs(rider, layer, n_j):
    w, axis, n_chunks = rider
    r, c = w.shape[1:]

    def chunk(i, j):
        return jnp.minimum(i * n_j + j, n_chunks - 1)

    if axis == 0:
        blk = (r // n_chunks, c)
        in_spec = pl.BlockSpec((None,) + blk, lambda i, j: (layer, chunk(i, j), 0))
        out_spec = pl.BlockSpec(blk, lambda i, j: (chunk(i, j), 0))
    else:
        blk = (r, c // n_chunks)
        in_spec = pl.BlockSpec((None,) + blk, lambda i, j: (layer, 0, chunk(i, j)))
        out_spec = pl.BlockSpec(blk, lambda i, j: (0, chunk(i, j)))
    return in_spec, out_spec, jax.ShapeDtypeStruct((r, c), BF16)


def _ring_fetch(tile_copy):
    n_j = pl.num_programs(1)
    step = pl.program_id(0) * n_j + pl.program_id(1)
    total = pl.num_programs(0) * n_j

    @pl.when(step == 0)
    def _():
        for k in range(WEIGHT_RING_SLOTS - 1):
            tile_copy(k).start()

    @pl.when(step + WEIGHT_RING_SLOTS - 1 < total)
    def _():
        tile_copy(step + WEIGHT_RING_SLOTS - 1).start()

    tile_copy(step).wait()
    return step % WEIGHT_RING_SLOTS


def _proj_kernel(x_ref, w_hbm, cast_in, o_ref, cast_out, w_ring, sem, *, slabs, n_chunks, layer, col0):
    _cast_rider(cast_in, cast_out, n_chunks)
    tn = w_ring.shape[2]

    def tile_copy(k):
        cols = pl.ds(pl.multiple_of(col0 + (k % pl.num_programs(1)) * tn, tn), tn)
        slot = k % WEIGHT_RING_SLOTS
        return pltpu.make_async_copy(w_hbm.at[layer, :, cols], w_ring.at[slot], sem.at[slot])

    slot = _ring_fetch(tile_copy)
    acc = jnp.dot(x_ref[...], w_ring[slot].astype(BF16), preferred_element_type=F32)
    if slabs:
        width = o_ref.shape[-1]
        for j in range(slabs):
            o_ref[j] = acc[:, j * width:(j + 1) * width].astype(o_ref.dtype)
    else:
        o_ref[...] = acc.astype(o_ref.dtype)


def _proj(h, w, layer, col0, n, out_dtype, slab_width, name, rider, tn=512):
    s, d = h.shape
    tm = s // PROJ_ROW_TILES
    n_j = n // tn
    assert rider[2] <= PROJ_ROW_TILES * n_j
    if slab_width:
        slabs = tn // slab_width
        out_shape = jax.ShapeDtypeStruct((n // slab_width, s, slab_width), out_dtype)
        out_spec = pl.BlockSpec((slabs, tm, slab_width), lambda i, j: (j, i, 0))
    else:
        slabs = 0
        out_shape = jax.ShapeDtypeStruct((s, n), out_dtype)
        out_spec = pl.BlockSpec((tm, tn), lambda i, j: (i, j))
    r_in, r_out, r_shape = _rider_specs(rider, layer, n_j)
    assert PROJ_ROW_TILES * n_j >= WEIGHT_RING_SLOTS
    return pl.pallas_call(
        functools.partial(_proj_kernel, slabs=slabs, n_chunks=rider[2], layer=layer, col0=col0),
        grid=(s // tm, n_j),
        in_specs=[pl.BlockSpec((tm, d), lambda i, j: (i, 0)),
                  pl.BlockSpec(memory_space=pl.ANY), r_in],
        out_specs=[out_spec, r_out],
        out_shape=[out_shape, r_shape],
        scratch_shapes=[pltpu.VMEM((WEIGHT_RING_SLOTS, d, tn), w.dtype),
                        pltpu.SemaphoreType.DMA((WEIGHT_RING_SLOTS,))],
        compiler_params=_params("arbitrary", "arbitrary"),
        name=name,
    )(h, w, rider[0])


def _regroup_matrix(dil):
    r = REGROUP_ROWS // dil
    out_row = np.arange(REGROUP_ROWS)
    src = (out_row % r) * dil + out_row // r
    return jnp.asarray(src[:, None] == np.arange(REGROUP_ROWS)[None, :], BF16)


def _proj_attn_kernel(x_ref, w_hbm, cast_in, perm_ref, o_ref, cast_out, scr, w_ring, sem, *,
                      dil, n_chunks, layer, col0, col_stride):
    _cast_rider(cast_in, cast_out, n_chunks)
    tn = w_ring.shape[2]

    def tile_copy(k):
        cols = pl.ds(pl.multiple_of(col0 + (k % pl.num_programs(1)) * col_stride, tn), tn)
        slot = k % WEIGHT_RING_SLOTS
        return pltpu.make_async_copy(w_hbm.at[layer, :, cols], w_ring.at[slot], sem.at[slot])

    slot = _ring_fetch(tile_copy)
    acc = jnp.dot(x_ref[...], w_ring[slot].astype(BF16), preferred_element_type=F32)
    if dil == 1:
        for hh in range(HEADS_PER_GROUP):
            o_ref[hh, 0] = acc[:, hh * LANES:(hh + 1) * LANES].astype(o_ref.dtype)
    elif dil < REGROUP_MATMUL_MIN_DIL:
        rows = scr.shape[1] // dil
        for hh in range(HEADS_PER_GROUP):
            scr[hh] = acc[:, hh * LANES:(hh + 1) * LANES]
        for hh in range(HEADS_PER_GROUP):
            for c in range(dil):
                o_ref[hh, c] = scr[hh, pl.ds(c, rows, stride=dil), :].astype(o_ref.dtype)
    else:
        r = REGROUP_ROWS // dil
        rounded = acc.astype(BF16)
        for b in range(acc.shape[0] // REGROUP_ROWS):
            blk = jnp.dot(perm_ref[...], rounded[b * REGROUP_ROWS:(b + 1) * REGROUP_ROWS],
                          preferred_element_type=F32).astype(o_ref.dtype)
            for hh in range(HEADS_PER_GROUP):
                for c in range(dil):
                    o_ref[hh, c, b * r:(b + 1) * r, :] = blk[c * r:(c + 1) * r,
                                                             hh * LANES:(hh + 1) * LANES]


def _proj_attn(h, w, layer, col0, gi, rider):
    s, d = h.shape
    tm = s // PROJ_ROW_TILES
    assert rider[2] <= PROJ_ROW_TILES * 3
    dil = ATTN_GROUPS[gi][1]
    tn = ATTN_OUT_WIDTH
    n_groups = len(ATTN_GROUPS)
    r_in, r_out, r_shape = _rider_specs(rider, layer, 3)
    perm = _regroup_matrix(dil)
    return pl.pallas_call(
        functools.partial(_proj_attn_kernel, dil=dil, n_chunks=rider[2], layer=layer,
                          col0=col0 + gi * tn, col_stride=n_groups * tn),
        grid=(s // tm, 3),
        in_specs=[pl.BlockSpec((tm, d), lambda i, j: (i, 0)),
                  pl.BlockSpec(memory_space=pl.ANY), r_in,
                  pl.BlockSpec(perm.shape, lambda i, j: (0, 0))],
        out_specs=[pl.BlockSpec((HEADS_PER_GROUP, dil, tm // dil, LANES), lambda i, j: (j, 0, i, 0)),
                   r_out],
        out_shape=[jax.ShapeDtypeStruct((3 * HEADS_PER_GROUP, dil, s // dil, LANES), BF16), r_shape],
        scratch_shapes=[pltpu.VMEM((HEADS_PER_GROUP, tm, LANES), F32),
                        pltpu.VMEM((WEIGHT_RING_SLOTS, d, tn), w.dtype),
                        pltpu.SemaphoreType.DMA((WEIGHT_RING_SLOTS,))],
        compiler_params=_params("arbitrary", "arbitrary"),
        name=f"proj_attn_g{gi}",
    )(h, w, rider[0], perm)


def _hgrn_constants():
    c = CHUNK
    t = np.arange(c)[:, None]
    s = np.arange(c)[None, :]
    mats = [s <= t]
    masks = []
    for lvl in range(N_LEVELS):
        m = c >> (lvl + 1)
        p = (t // (2 * m)) * (2 * m) + m - 1
        if m < MXU_LEVEL_BELOW:
            mats.append(((t > p) & (s > p) & (s <= t)) | ((t <= p) & (s > t) & (s <= p)))
        masks.append((t // (2 * m) == s // (2 * m)) & (t % (2 * m) >= m) & (s % (2 * m) < m))
    w = np.stack(mats).astype(np.float32)
    k = np.stack(masks).astype(np.float32)
    w = np.stack([w, w[:, ::-1, ::-1]]).reshape(2, len(mats) * c, c)
    w = np.concatenate([w, w], axis=2)
    k = np.stack([k, k[:, ::-1, ::-1]])
    k = np.concatenate([k, k], axis=3)
    return jnp.asarray(w, BF16), jnp.asarray(k, BF16)


def _block_diag(x):
    zero = jnp.zeros((x.shape[0], LANES), x.dtype)
    return jnp.concatenate([jnp.concatenate([x[:, :LANES], zero], axis=1),
                            jnp.concatenate([zero, x[:, LANES:]], axis=1)], axis=0)


def _block_diag_t(x):
    zero = jnp.zeros((LANES, x.shape[1]), x.dtype)
    return jnp.concatenate([jnp.concatenate([x[:LANES], zero], axis=1),
                            jnp.concatenate([zero, x[LANES:]], axis=1)], axis=0)


def _hgrn_chunk(q, z, v, lb, wmat, m_ref, st, backward):
    half = 0.5 * (1.0 - lb)
    ct = half * jnp.tanh(0.5 * z)
    kk = half - ct
    logf = jnp.log2(jnp.maximum((lb + half) + ct, MIN_FORGET))
    hi = logf.astype(BF16)
    lo = (logf - hi.astype(F32)).astype(BF16)
    dec = jnp.dot(wmat, jnp.concatenate([hi, lo], axis=0), preferred_element_type=F32)
    b = dec[0:CHUNK]
    total = b[0:1] if backward else b[CHUNK - 1:CHUNK]
    q_in = (q * jnp.exp2(b)).astype(BF16)
    k_out = (kk * jnp.exp2(total - b)).astype(BF16)
    vb = v.astype(BF16)
    qb = q.astype(BF16)
    kb = kk.astype(BF16)
    o = jnp.dot(q_in, _block_diag_t(st.astype(BF16).T), preferred_element_type=F32)
    qk = q * kk
    diag = jnp.concatenate(
        [jnp.broadcast_to(jnp.sum(qk[:, :LANES], axis=-1, keepdims=True), (CHUNK, LANES)),
         jnp.broadcast_to(jnp.sum(qk[:, LANES:], axis=-1, keepdims=True), (CHUNK, LANES))], axis=1)
    o = o + diag * v
    a = None
    for lvl in range(N_LEVELS):
        m = CHUNK >> (lvl + 1)
        if m >= MXU_LEVEL_BELOW:
            parts = []
            for j in range(CHUNK // (2 * m)):
                lo_rows = b[2 * m * j:2 * m * j + m]
                hi_rows = b[2 * m * j + m:2 * m * (j + 1)]
                if backward:
                    piv = b[2 * m * j + m:2 * m * j + m + 1]
                    parts += [lo_rows - piv, piv - hi_rows]
                else:
                    piv = b[2 * m * j + m - 1:2 * m * j + m]
                    parts += [piv - lo_rows, hi_rows - piv]
            nl = jnp.concatenate(parts, axis=0)
        else:
            i = 1 + lvl - (N_LEVELS - N_MXU_LEVELS)
            nl = dec[i * CHUNK:(i + 1) * CHUNK]
        el = jnp.exp2(nl).astype(BF16)
        sc = jnp.dot(qb * el, _block_diag_t((kb * el).T), preferred_element_type=F32)
        sc = sc.astype(BF16) * m_ref[lvl]
        a = sc if a is None else a + sc
    o = o + jnp.dot(a, _block_diag(vb), preferred_element_type=F32)
    v_rows = jnp.concatenate([vb[:, :LANES], vb[:, LANES:]], axis=0)
    st_new = st * jnp.exp2(total) + lax.dot_general(v_rows, _block_diag(k_out), _TN,
                                                    preferred_element_type=F32)
    return o, st_new


def _hgrn_kernel(qf_ref, zf_ref, vf_ref, qb_ref, zb_ref, vb_ref, lb_ref, w_ref, m_ref,
                 of_ref, ob_ref, stf_ref, stb_ref):
    @pl.when(pl.program_id(0) == 0)
    def _():
        stf_ref[...] = jnp.zeros_like(stf_ref)
        stb_ref[...] = jnp.zeros_like(stb_ref)

    n_sub = qf_ref.shape[1] // CHUNK

    def sub_chunk(t, carry):
        rf = pl.ds(pl.multiple_of(t * CHUNK, CHUNK), CHUNK)
        rb = pl.ds(pl.multiple_of((n_sub - 1 - t) * CHUNK, CHUNK), CHUNK)
        for p in range(HGRN_HEADS // 2):
            o, st = _hgrn_chunk(qf_ref[p, rf, :], zf_ref[p, rf, :], vf_ref[p, rf, :], lb_ref[0, p],
                                w_ref[0], m_ref.at[0], stf_ref[p], False)
            of_ref[p, rf, :] = o
            stf_ref[p] = st
            o, st = _hgrn_chunk(qb_ref[p, rb, :], zb_ref[p, rb, :], vb_ref[p, rb, :], lb_ref[1, p],
                                w_ref[1], m_ref.at[1], stb_ref[p], True)
            ob_ref[p, rb, :] = o
            stb_ref[p] = st
        return carry

    lax.fori_loop(0, n_sub, sub_chunk, 0)


def _hgrn(proj_h, lb):
    s = proj_h.shape[1]
    n = s // HGRN_BLOCK
    pairs = HGRN_HEADS // 2
    wmat, masks = _hgrn_constants()
    hb = (pairs, HGRN_BLOCK, 2 * LANES)
    fwd = lambda slab: pl.BlockSpec(hb, lambda c: (slab, c, 0))
    bwd = lambda slab: pl.BlockSpec(hb, lambda c: (slab, n - 1 - c, 0))
    const = lambda a: pl.BlockSpec(a.shape, lambda c: (0,) * a.ndim)
    out_shape = jax.ShapeDtypeStruct((pairs, s, 2 * LANES), F32)
    state = pltpu.VMEM((pairs, LANES, 2 * LANES), F32)
    return pl.pallas_call(
        _hgrn_kernel,
        grid=(n,),
        in_specs=[fwd(0), fwd(1), fwd(3), bwd(0), bwd(2), bwd(3), const(lb), const(wmat), const(masks)],
        out_specs=[fwd(0), bwd(0)],
        out_shape=[out_shape, out_shape],
        scratch_shapes=[state, state],
        compiler_params=_params("arbitrary"),
        name="hgrn2_scan",
    )(proj_h, proj_h, proj_h, proj_h, proj_h, proj_h, lb, wmat, masks)


def _t5_bucket(rel):
    half = REL_BUCKETS // 2
    ret = (rel > 0).astype(np.int32) * half
    n = np.abs(rel)
    max_exact = half // 2
    large = max_exact + (np.log(np.maximum(n, 1) / max_exact)
                         / np.log(REL_MAX_DISTANCE / max_exact)
                         * (half - max_exact)).astype(np.int32)
    large = np.minimum(large, half - 1)
    return (ret + np.where(n < max_exact, n, large)).astype(np.int32)


def _attn_bias(rel_bias_table):
    span = 3 * ATTN_QBLK - 1
    rel = np.arange(span) - (ATTN_QBLK - 1) - ATTN_SIDE
    col = np.arange(2 * ATTN_QBLK)[None, :]
    band = np.abs(col - ATTN_SIDE - np.arange(ATTN_QBLK)[:, None]) <= ATTN_SIDE
    after_start, before_end = col >= ATTN_SIDE, col < 2 * ATTN_QBLK - ATTN_SIDE
    keep = [band, band & after_start, band & before_end, band & after_start & before_end]
    out = []
    for gi, (_, dil) in enumerate(ATTN_GROUPS):
        tab = rel_bias_table[:, gi * HEADS_PER_GROUP:(gi + 1) * HEADS_PER_GROUP].astype(F32)
        onehot = jnp.asarray(_t5_bucket(rel * dil)[:, None] == np.arange(REL_BUCKETS)[None, :], F32)
        vec = jnp.einsum("rb,bh->hr", onehot, tab, precision=lax.Precision.HIGHEST)
        period = span + 2
        vec = jnp.pad(vec, ((0, 0), (0, period - span)))
        flat = jnp.tile(vec, (1, ATTN_QBLK))[:, :ATTN_QBLK * (span + 1)]
        toep = flat.reshape(HEADS_PER_GROUP, ATTN_QBLK, span + 1)[:, :, ATTN_QBLK - 1:3 * ATTN_QBLK - 1]
        out.append(jnp.stack([jnp.where(jnp.asarray(k)[None], toep, NEG_INF) for k in keep], axis=1))
    return jnp.stack(out)


def _attn_kernel(q_ref, kp_ref, km_ref, kn_ref, vp_ref, vm_ref, vn_ref, bias_ref, *rest,
                 rows, dil, cpb, has_prev, final):
    rest = list(rest)
    op_ref, lp_ref = (rest.pop(0), rest.pop(0)) if has_prev else (None, None)
    o_ref = rest.pop(0)
    lse_ref = None if final else rest.pop(0)
    o_s, l_s = rest if rest else (o_ref, lse_ref)
    c = pl.program_id(2)
    i = pl.program_id(1)
    scale = LANES ** -0.5
    n_sub = rows // ATTN_QBLK

    def window(prev_ref, main_ref, next_ref, cc, j):
        lo, hi = j * ATTN_QBLK - ATTN_SIDE, (j + 1) * ATTN_QBLK + ATTN_SIDE
        parts = []
        if lo < 0:
            parts.append(prev_ref[cc])
        parts.append(main_ref[cc, max(lo, 0):min(hi, rows)])
        if hi > rows:
            parts.append(next_ref[cc])
        return parts[0] if len(parts) == 1 else jnp.concatenate(parts, axis=0)

    tiles = [(cc, j) for cc in range(cpb) for j in range(n_sub)]
    for t0 in range(0, len(tiles), ATTN_QK_BATCH):
        batch = tiles[t0:t0 + ATTN_QK_BATCH]
        scores = []
        for cc, j in batch:
            r0 = j * ATTN_QBLK
            variant = 0
            if j == 0:
                variant = variant + (i == 0).astype(jnp.int32)
            if j == n_sub - 1:
                variant = variant + 2 * (i == pl.num_programs(1) - 1).astype(jnp.int32)
            scores.append(lax.dot_general(q_ref[cc, r0:r0 + ATTN_QBLK],
                                          window(kp_ref, km_ref, kn_ref, cc, j), _NT,
                                          preferred_element_type=F32) * scale + bias_ref[variant])
        for (cc, j), s in zip(batch, scores):
            r0 = j * ATTN_QBLK
            m = jnp.max(s, axis=-1, keepdims=True)
            p = jnp.exp(s - m)
            l = jnp.sum(p, axis=-1, keepdims=True)
            o = jnp.dot(p.astype(BF16), window(vp_ref, vm_ref, vn_ref, cc, j),
                        preferred_element_type=F32) / l
            lse = jnp.broadcast_to(m + jnp.log(l), (ATTN_QBLK, LANES))
            if dil == 1:
                o_s[r0:r0 + ATTN_QBLK] = o
                l_s[r0:r0 + ATTN_QBLK] = lse
            else:
                start = r0 * dil + c * cpb + cc
                o_s[pl.ds(start, ATTN_QBLK, stride=dil), :] = o
                l_s[pl.ds(start, ATTN_QBLK, stride=dil), :] = lse

    if has_prev:
        @pl.when(c == dil // cpb - 1)
        def _():
            def fold(t, carry):
                r = pl.ds(pl.multiple_of(t * ATTN_FOLD_ROWS, ATTN_FOLD_ROWS), ATTN_FOLD_ROWS)
                lp, lc = lp_ref[r, :], l_s[r, :]
                m = jnp.maximum(lp, lc)
                wp, wc = jnp.exp(lp - m), jnp.exp(lc - m)
                den = wp + wc
                o = (wp * op_ref[r, :] + wc * o_s[r, :]) / den
                o_ref[r, :] = o.astype(o_ref.dtype)
                if not final:
                    lse_ref[r, :] = m + jnp.log(den)
                return carry

            lax.fori_loop(0, rows * dil // ATTN_FOLD_ROWS, fold, 0)


def _attn_group(qkv, bias, gi, prev):
    dil, length = qkv.shape[1], qkv.shape[2]
    s = dil * length
    cpb = min(ATTN_RESIDUES, dil)
    rows = min(ATTN_TILES * ATTN_QBLK // cpb, length)
    nblk = length // rows
    sub = rows // ATTN_SIDE
    last = length // ATTN_SIDE - 1
    hp = HEADS_PER_GROUP
    final = gi == len(ATTN_GROUPS) - 1
    has_prev = prev is not None
    assert has_prev or dil == 1

    def main(kind):
        return pl.BlockSpec((None, cpb, rows, LANES), lambda g, i, c: (kind * hp + g, c, i, 0))

    def prev_blk(kind):
        return pl.BlockSpec((None, cpb, ATTN_SIDE, LANES),
                            lambda g, i, c: (kind * hp + g, c, jnp.maximum(i * sub - 1, 0), 0))

    def next_blk(kind):
        return pl.BlockSpec((None, cpb, ATTN_SIDE, LANES),
                            lambda g, i, c: (kind * hp + g, c, jnp.minimum((i + 1) * sub, last), 0))

    run_spec = pl.BlockSpec((None, rows * dil, LANES), lambda g, i, c: (g, i, 0))
    run_shape = jax.ShapeDtypeStruct((hp, s, LANES), F32)
    in_specs = [main(0), prev_blk(1), main(1), next_blk(1), prev_blk(2), main(2), next_blk(2),
                pl.BlockSpec((None, None, 4, ATTN_QBLK, 2 * ATTN_QBLK), lambda g, i, c: (gi, g, 0, 0, 0))]
    args = [qkv] * 7 + [bias]
    if has_prev:
        in_specs += [run_spec, run_spec]
        args += list(prev)
    if final:
        out_specs = pl.BlockSpec((rows * dil, LANES), lambda g, i, c: (i, g))
        out_shape = jax.ShapeDtypeStruct((s, hp * LANES), BF16)
    else:
        out_specs = [run_spec, run_spec]
        out_shape = [run_shape, run_shape]
    scratch = []
    if has_prev:
        scratch = [pltpu.VMEM((rows * dil, LANES), F32), pltpu.VMEM((rows * dil, LANES), F32)]
    return pl.pallas_call(
        functools.partial(_attn_kernel, rows=rows, dil=dil, cpb=cpb, has_prev=has_prev, final=final),
        grid=(hp, nblk, dil // cpb),
        in_specs=in_specs,
        out_specs=out_specs,
        out_shape=out_shape,
        scratch_shapes=scratch,
        compiler_params=_params("parallel", "arbitrary", "arbitrary"),
        name=f"dilated_attn_g{gi}",
    )(*args)


def _merge_kernel(of_ref, ob_ref, g_ref, oa_ref, gate_ref, x_ref, pa_ref, pb_ref, wo_ref, hw_ref,
                  nw_ref, xo_ref, ho_ref):
    d = x_ref.shape[1]
    b = jnp.dot(oa_ref[...], pb_ref[...], preferred_element_type=F32)
    a = None
    for h0 in range(0, HGRN_HEADS, MERGE_HEAD_GROUP):
        cols = []
        for h in range(h0, h0 + MERGE_HEAD_GROUP):
            lanes = slice((h % 2) * LANES, (h % 2 + 1) * LANES)
            o = of_ref[h // 2, :, lanes] + ob_ref[h // 2, :, lanes]
            o = o * lax.rsqrt(jnp.mean(o * o, axis=-1, keepdims=True) + NORM_EPS)
            g = g_ref[h // 2, :, lanes]
            cols.append((o * hw_ref[h] * (g * _sigmoid(g))).astype(BF16))
        part = jnp.dot(jnp.concatenate(cols, axis=1), pa_ref[h0 * LANES:(h0 + MERGE_HEAD_GROUP) * LANES, :],
                       preferred_element_type=F32)
        a = part if a is None else a + part
    merged = _sigmoid(gate_ref[:, :d]) * a.astype(BF16) + _sigmoid(gate_ref[:, d:]) * b.astype(BF16)
    xn = x_ref[...] + jnp.dot(merged, wo_ref[...], preferred_element_type=F32)
    xo_ref[...] = xn
    ho_ref[...] = _rms(xn, nw_ref[...]).astype(ho_ref.dtype)


def _merge(o_hgrn, proj_h, o_attn, gates, x, pa, pb, wo, hgrn_norm_w, norm_w, tm=256):
    s, d = x.shape
    pairs = HGRN_HEADS // 2
    hspec = pl.BlockSpec((pairs, tm, 2 * LANES), lambda i: (0, i, 0))

    def wspec(rows):
        return pl.BlockSpec((rows, d), lambda i: (0, 0))

    return pl.pallas_call(
        _merge_kernel,
        grid=(s // tm,),
        in_specs=[hspec, hspec,
                  pl.BlockSpec((pairs, tm, 2 * LANES), lambda i: (4, i, 0)),
                  pl.BlockSpec((tm, ATTN_OUT_WIDTH), lambda i: (i, 0)),
                  pl.BlockSpec((tm, 2 * d), lambda i: (i, 0)),
                  pl.BlockSpec((tm, d), lambda i: (i, 0)),
                  wspec(HGRN_WIDTH), wspec(ATTN_OUT_WIDTH), wspec(d),
                  pl.BlockSpec((HGRN_HEADS, 1, LANES), lambda i: (0, 0, 0)),
                  pl.BlockSpec((1, d), lambda i: (0, 0))],
        out_specs=[pl.BlockSpec((tm, d), lambda i: (i, 0)),
                   pl.BlockSpec((tm, d), lambda i: (i, 0))],
        out_shape=[jax.ShapeDtypeStruct((s, d), F32), jax.ShapeDtypeStruct((s, d), BF16)],
        compiler_params=_params("parallel"),
        name="merge_out_proj",
    )(o_hgrn[0], o_hgrn[1], proj_h, o_attn, gates, x, pa, pb, wo,
      hgrn_norm_w.reshape(HGRN_HEADS, 1, LANES), norm_w.reshape(1, d))


def _mlp_kernel(h_ref, x_ref, wu_hbm, wd_hbm, nw_ref, xo_ref, ho_ref, acc, wu_ring, wd_ring, sem):
    f = pl.program_id(1)
    tf = wu_ring.shape[2]

    def up_copy(k):
        cols = pl.ds(pl.multiple_of((k % pl.num_programs(1)) * tf, tf), tf)
        slot = k % WEIGHT_RING_SLOTS
        return pltpu.make_async_copy(wu_hbm.at[:, cols], wu_ring.at[slot], sem.at[0, slot])

    def down_copy(k):
        rows = pl.ds(pl.multiple_of((k % pl.num_programs(1)) * tf, tf), tf)
        slot = k % WEIGHT_RING_SLOTS
        return pltpu.make_async_copy(wd_hbm.at[rows, :], wd_ring.at[slot], sem.at[1, slot])

    slot = _ring_fetch(up_copy)
    _ring_fetch(down_copy)

    @pl.when(f == 0)
    def _():
        acc[...] = x_ref[...]

    u = jnp.maximum(jnp.dot(h_ref[...], wu_ring[slot], preferred_element_type=F32), 0.0)
    acc[...] += jnp.dot((u * u).astype(BF16), wd_ring[slot], preferred_element_type=F32)

    @pl.when(f == pl.num_programs(1) - 1)
    def _():
        xn = acc[...]
        xo_ref[...] = xn
        ho_ref[...] = _rms(xn, nw_ref[...]).astype(ho_ref.dtype)


def _mlp(h, x, wu, wd, norm_w, h_dtype, tm=512, tf=1024):
    s, d = x.shape
    ff = wu.shape[1]
    return pl.pallas_call(
        _mlp_kernel,
        grid=(s // tm, ff // tf),
        in_specs=[pl.BlockSpec((tm, d), lambda i, f: (i, 0)),
                  pl.BlockSpec((tm, d), lambda i, f: (i, 0)),
                  pl.BlockSpec(memory_space=pl.ANY),
                  pl.BlockSpec(memory_space=pl.ANY),
                  pl.BlockSpec((1, d), lambda i, f: (0, 0))],
        out_specs=[pl.BlockSpec((tm, d), lambda i, f: (i, 0)),
                   pl.BlockSpec((tm, d), lambda i, f: (i, 0))],
        out_shape=[jax.ShapeDtypeStruct((s, d), F32), jax.ShapeDtypeStruct((s, d), h_dtype)],
        scratch_shapes=[pltpu.VMEM((tm, d), F32),
                        pltpu.VMEM((WEIGHT_RING_SLOTS, d, tf), wu.dtype),
                        pltpu.VMEM((WEIGHT_RING_SLOTS, tf, d), wd.dtype),
                        pltpu.SemaphoreType.DMA((2, WEIGHT_RING_SLOTS))],
        compiler_params=pltpu.CompilerParams(dimension_semantics=("arbitrary", "arbitrary"),
                                             vmem_limit_bytes=MLP_VMEM_LIMIT),
        name="mlp_relu2",
    )(h, x, wu, wd, norm_w.reshape(1, d))


def _lower_bounds(logits):
    p = jax.nn.softmax(logits.astype(F32), axis=0)
    return jnp.cumsum(p, axis=0) - p[0:1]


def kernel(x, w_in, hgrn_lb_fwd, hgrn_lb_bwd, hgrn_norm_w, rel_bias_table, w_branch_hgrn,
           w_branch_attn, w_out, norm_mix_w, norm_mlp_w, w_up, w_down, final_norm_w):
    batch, s, d = x.shape
    depth = w_in.shape[0]
    assert batch == 1
    x = x.reshape(s, d)
    lb = jnp.stack([_lower_bounds(hgrn_lb_fwd), _lower_bounds(hgrn_lb_bwd)], axis=1)
    lb = lb.reshape(-1, 2, HGRN_HEADS // 2, 1, 2 * LANES)
    bias = _attn_bias(rel_bias_table)
    n_h = 5 * HGRN_WIDTH
    n_a = 3 * ATTN_WIDTH
    cast_rows = 256
    h = _norm(x, norm_mix_w[0])
    for l in range(depth):
        proj_h, wd_b = _proj(h, w_in, l, 0, n_h, F32, 2 * LANES, "proj_hgrn",
                             (w_down, 0, w_down.shape[1] // cast_rows))
        gates, wu_b = _proj(h, w_in, l, n_h + n_a, 2 * d, BF16, 0, "proj_gates",
                            (w_up, 1, w_up.shape[2] // cast_rows))
        o_hgrn = _hgrn(proj_h, lb[l])
        attn = None
        cast = []
        for gi, wt in enumerate((w_out, w_branch_hgrn, w_branch_attn)):
            qkv, wt_b = _proj_attn(h, w_in, l, n_h, gi, (wt, 0, wt.shape[1] // cast_rows))
            attn = _attn_group(qkv, bias, gi, attn)
            cast.append(wt_b)
        wo_b, pa_b, pb_b = cast
        x, h2 = _merge(o_hgrn, proj_h, attn, gates, x, pa_b, pb_b, wo_b, hgrn_norm_w[l],
                       norm_mlp_w[l])
        last = l == depth - 1
        x, h = _mlp(h2, x, wu_b, wd_b, final_norm_w if last else norm_mix_w[l + 1],
                    F32 if last else BF16)
    return h.reshape(batch, s, d)
```

```python
import functools

import jax
import jax.numpy as jnp
import numpy as np
from jax import lax
from jax.experimental import pallas as pl
from jax.experimental.pallas import tpu as pltpu

F32 = jnp.float32
BF16 = jnp.bfloat16

LANES = 128
HGRN_HEADS = 8
HGRN_WIDTH = HGRN_HEADS * LANES
CHUNK = 128
HGRN_BLOCK = 4 * CHUNK
N_LEVELS = 7
MXU_LEVEL_BELOW = 8
N_MXU_LEVELS = 3
MIN_FORGET = 1e-30
ATTN_GROUPS = ((128, 1), (512, 4), (2048, 16))
HEADS_PER_GROUP = 4
N_ATTN_HEADS = HEADS_PER_GROUP * len(ATTN_GROUPS)
ATTN_WIDTH = N_ATTN_HEADS * LANES
ATTN_OUT_WIDTH = HEADS_PER_GROUP * LANES
ATTN_QBLK = 128
ATTN_SIDE = 64
ATTN_TILES = 32
ATTN_QK_BATCH = 8
ATTN_RESIDUES = 8
ATTN_FOLD_ROWS = 256
REL_BUCKETS = 32
REL_MAX_DISTANCE = 1024
NORM_EPS = 1e-6
NEG_INF = -1e30
MERGE_HEAD_GROUP = 4
REGROUP_ROWS = 256
REGROUP_MATMUL_MIN_DIL = 8
WEIGHT_RING_SLOTS = 3
PROJ_ROW_TILES = 4
VMEM_LIMIT = 56 * 1024 * 1024
MLP_VMEM_LIMIT = 60 * 1024 * 1024

_NT = (((1,), (1,)), ((), ()))
_TN = (((0,), (0,)), ((), ()))


def _params(*sem):
    return pltpu.CompilerParams(dimension_semantics=sem, vmem_limit_bytes=VMEM_LIMIT)


def _rms(x, w):
    return x * lax.rsqrt(jnp.mean(x * x, axis=-1, keepdims=True) + NORM_EPS) * w


def _sigmoid(x):
    return 0.5 * jnp.tanh(0.5 * x) + 0.5


def _norm_kernel(x_ref, w_ref, o_ref):
    o_ref[...] = _rms(x_ref[...], w_ref[...]).astype(o_ref.dtype)


def _norm(x, w, tm=512):
    s, d = x.shape
    return pl.pallas_call(
        _norm_kernel,
        grid=(s // tm,),
        in_specs=[pl.BlockSpec((tm, d), lambda i: (i, 0)),
                  pl.BlockSpec((1, d), lambda i: (0, 0))],
        out_specs=pl.BlockSpec((tm, d), lambda i: (i, 0)),
        out_shape=jax.ShapeDtypeStruct((s, d), BF16),
        compiler_params=_params("parallel"),
        name="rmsnorm_in",
    )(x, w.reshape(1, d))


def _cast_rider(cast_in, cast_out, n_chunks):
    step = pl.program_id(0) * pl.num_programs(1) + pl.program_id(1)

    @pl.when(step < n_chunks)
    def _():
        cast_out[...] = cast_in[...].astype(cast_out.dtype)


def _rider_specs(rider, layer, n_j):
    w, axis, n_chunks = rider
    r, c = w.shape[1:]

    def chunk(i, j):
        return jnp.minimum(i * n_j + j, n_chunks - 1)

    if axis == 0:
        blk = (r // n_chunks, c)
        in_spec = pl.BlockSpec((None,) + blk, lambda i, j: (layer, chunk(i, j), 0))
        out_spec = pl.BlockSpec(blk, lambda i, j: (chunk(i, j), 0))
    else:
        blk = (r, c // n_chunks)
        in_spec = pl.BlockSpec((None,) + blk, lambda i, j: (layer, 0, chunk(i, j)))
        out_spec = pl.BlockSpec(blk, lambda i, j: (0, chunk(i, j)))
    return in_spec, out_spec, jax.ShapeDtypeStruct((r, c), BF16)


def _ring_fetch(tile_copy, step=None, total=None):
    if step is None:
        n_j = pl.num_programs(1)
        step = pl.program_id(0) * n_j + pl.program_id(1)
        total = pl.num_programs(0) * n_j

    @pl.when(step == 0)
    def _():
        for k in range(WEIGHT_RING_SLOTS - 1):
            tile_copy(k).start()

    @pl.when(step + WEIGHT_RING_SLOTS - 1 < total)
    def _():
        tile_copy(step + WEIGHT_RING_SLOTS - 1).start()

    tile_copy(step).wait()
    return step % WEIGHT_RING_SLOTS


def _proj_kernel(x_ref, w_hbm, cast_in, o_ref, cast_out, w_ring, sem, *, slabs, n_chunks, layer, col0):
    _cast_rider(cast_in, cast_out, n_chunks)
    tn = w_ring.shape[2]

    def tile_copy(k):
        cols = pl.ds(pl.multiple_of(col0 + (k % pl.num_programs(1)) * tn, tn), tn)
        slot = k % WEIGHT_RING_SLOTS
        return pltpu.make_async_copy(w_hbm.at[layer, :, cols], w_ring.at[slot], sem.at[slot])

    slot = _ring_fetch(tile_copy)
    acc = jnp.dot(x_ref[...], w_ring[slot].astype(BF16), preferred_element_type=F32)
    if slabs:
        width = o_ref.shape[-1]
        for j in range(slabs):
            o_ref[j] = acc[:, j * width:(j + 1) * width].astype(o_ref.dtype)
    else:
        o_ref[...] = acc.astype(o_ref.dtype)


def _proj(h, w, layer, col0, n, out_dtype, slab_width, name, rider, tn=512):
    s, d = h.shape
    tm = s // PROJ_ROW_TILES
    n_j = n // tn
    assert rider[2] <= PROJ_ROW_TILES * n_j
    if slab_width:
        slabs = tn // slab_width
        out_shape = jax.ShapeDtypeStruct((n // slab_width, s, slab_width), out_dtype)
        out_spec = pl.BlockSpec((slabs, tm, slab_width), lambda i, j: (j, i, 0))
    else:
        slabs = 0
        out_shape = jax.ShapeDtypeStruct((s, n), out_dtype)
        out_spec = pl.BlockSpec((tm, tn), lambda i, j: (i, j))
    r_in, r_out, r_shape = _rider_specs(rider, layer, n_j)
    assert PROJ_ROW_TILES * n_j >= WEIGHT_RING_SLOTS
    return pl.pallas_call(
        functools.partial(_proj_kernel, slabs=slabs, n_chunks=rider[2], layer=layer, col0=col0),
        grid=(s // tm, n_j),
        in_specs=[pl.BlockSpec((tm, d), lambda i, j: (i, 0)),
                  pl.BlockSpec(memory_space=pl.ANY), r_in],
        out_specs=[out_spec, r_out],
        out_shape=[out_shape, r_shape],
        scratch_shapes=[pltpu.VMEM((WEIGHT_RING_SLOTS, d, tn), w.dtype),
                        pltpu.SemaphoreType.DMA((WEIGHT_RING_SLOTS,))],
        compiler_params=_params("arbitrary", "arbitrary"),
        name=name,
    )(h, w, rider[0])


def _regroup_matrix(dil):
    r = REGROUP_ROWS // dil
    out_row = np.arange(REGROUP_ROWS)
    src = (out_row % r) * dil + out_row // r
    return jnp.asarray(src[:, None] == np.arange(REGROUP_ROWS)[None, :], BF16)


def _proj_attn_kernel(x_ref, w_hbm, cast_in, perm_ref, o_ref, cast_out, scr, w_ring, sem, *,
                      dil, n_chunks, layer, col0, col_stride):
    _cast_rider(cast_in, cast_out, n_chunks)
    tn = w_ring.shape[2]

    def tile_copy(k):
        cols = pl.ds(pl.multiple_of(col0 + (k % pl.num_programs(1)) * col_stride, tn), tn)
        slot = k % WEIGHT_RING_SLOTS
        return pltpu.make_async_copy(w_hbm.at[layer, :, cols], w_ring.at[slot], sem.at[slot])

    slot = _ring_fetch(tile_copy)
    acc = jnp.dot(x_ref[...], w_ring[slot].astype(BF16), preferred_element_type=F32)
    if dil == 1:
        for hh in range(HEADS_PER_GROUP):
            o_ref[hh, 0] = acc[:, hh * LANES:(hh + 1) * LANES].astype(o_ref.dtype)
    elif dil < REGROUP_MATMUL_MIN_DIL:
        rows = scr.shape[1] // dil
        for hh in range(HEADS_PER_GROUP):
            scr[hh] = acc[:, hh * LANES:(hh + 1) * LANES]
        for hh in range(HEADS_PER_GROUP):
            for c in range(dil):
                o_ref[hh, c] = scr[hh, pl.ds(c, rows, stride=dil), :].astype(o_ref.dtype)
    else:
        r = REGROUP_ROWS // dil
        rounded = acc.astype(BF16)
        for b in range(acc.shape[0] // REGROUP_ROWS):
            blk = jnp.dot(perm_ref[...], rounded[b * REGROUP_ROWS:(b + 1) * REGROUP_ROWS],
                          preferred_element_type=F32).astype(o_ref.dtype)
            for hh in range(HEADS_PER_GROUP):
                for c in range(dil):
                    o_ref[hh, c, b * r:(b + 1) * r, :] = blk[c * r:(c + 1) * r,
                                                             hh * LANES:(hh + 1) * LANES]


def _proj_attn(h, w, layer, col0, gi, rider):
    s, d = h.shape
    tm = s // PROJ_ROW_TILES
    assert rider[2] <= PROJ_ROW_TILES * 3
    dil = ATTN_GROUPS[gi][1]
    tn = ATTN_OUT_WIDTH
    n_groups = len(ATTN_GROUPS)
    r_in, r_out, r_shape = _rider_specs(rider, layer, 3)
    perm = _regroup_matrix(dil)
    return pl.pallas_call(
        functools.partial(_proj_attn_kernel, dil=dil, n_chunks=rider[2], layer=layer,
                          col0=col0 + gi * tn, col_stride=n_groups * tn),
        grid=(s // tm, 3),
        in_specs=[pl.BlockSpec((tm, d), lambda i, j: (i, 0)),
                  pl.BlockSpec(memory_space=pl.ANY), r_in,
                  pl.BlockSpec(perm.shape, lambda i, j: (0, 0))],
        out_specs=[pl.BlockSpec((HEADS_PER_GROUP, dil, tm // dil, LANES), lambda i, j: (j, 0, i, 0)),
                   r_out],
        out_shape=[jax.ShapeDtypeStruct((3 * HEADS_PER_GROUP, dil, s // dil, LANES), BF16), r_shape],
        scratch_shapes=[pltpu.VMEM((HEADS_PER_GROUP, tm, LANES), F32),
                        pltpu.VMEM((WEIGHT_RING_SLOTS, d, tn), w.dtype),
                        pltpu.SemaphoreType.DMA((WEIGHT_RING_SLOTS,))],
        compiler_params=_params("arbitrary", "arbitrary"),
        name=f"proj_attn_g{gi}",
    )(h, w, rider[0], perm)


def _hgrn_constants():
    c = CHUNK
    t = np.arange(c)[:, None]
    s = np.arange(c)[None, :]
    mats = [s <= t]
    masks = []
    for lvl in range(N_LEVELS):
        m = c >> (lvl + 1)
        p = (t // (2 * m)) * (2 * m) + m - 1
        if m < MXU_LEVEL_BELOW:
            mats.append(((t > p) & (s > p) & (s <= t)) | ((t <= p) & (s > t) & (s <= p)))
        masks.append((t // (2 * m) == s // (2 * m)) & (t % (2 * m) >= m) & (s % (2 * m) < m))
    w = np.stack(mats).astype(np.float32)
    k = np.stack(masks).astype(np.float32)
    w = np.stack([w, w[:, ::-1, ::-1]]).reshape(2, len(mats) * c, c)
    w = np.concatenate([w, w], axis=2)
    k = np.stack([k, k[:, ::-1, ::-1]])
    k = np.concatenate([k, k], axis=3)
    return jnp.asarray(w, BF16), jnp.asarray(k, BF16)


def _block_diag(x):
    zero = jnp.zeros((x.shape[0], LANES), x.dtype)
    return jnp.concatenate([jnp.concatenate([x[:, :LANES], zero], axis=1),
                            jnp.concatenate([zero, x[:, LANES:]], axis=1)], axis=0)


def _block_diag_t(x):
    zero = jnp.zeros((LANES, x.shape[1]), x.dtype)
    return jnp.concatenate([jnp.concatenate([x[:LANES], zero], axis=1),
                            jnp.concatenate([zero, x[LANES:]], axis=1)], axis=0)


def _hgrn_chunk(q, z, v, lb, wmat, m_ref, st, backward):
    half = 0.5 * (1.0 - lb)
    ct = half * jnp.tanh(0.5 * z)
    kk = half - ct
    logf = jnp.log2(jnp.maximum((lb + half) + ct, MIN_FORGET))
    hi = logf.astype(BF16)
    lo = (logf - hi.astype(F32)).astype(BF16)
    dec = jnp.dot(wmat, jnp.concatenate([hi, lo], axis=0), preferred_element_type=F32)
    b = dec[0:CHUNK]
    total = b[0:1] if backward else b[CHUNK - 1:CHUNK]
    q_in = (q * jnp.exp2(b)).astype(BF16)
    k_out = (kk * jnp.exp2(total - b)).astype(BF16)
    vb = v.astype(BF16)
    qb = q.astype(BF16)
    kb = kk.astype(BF16)
    o = jnp.dot(q_in, _block_diag_t(st.astype(BF16).T), preferred_element_type=F32)
    qk = q * kk
    diag = jnp.concatenate(
        [jnp.broadcast_to(jnp.sum(qk[:, :LANES], axis=-1, keepdims=True), (CHUNK, LANES)),
         jnp.broadcast_to(jnp.sum(qk[:, LANES:], axis=-1, keepdims=True), (CHUNK, LANES))], axis=1)
    o = o + diag * v
    a = None
    for lvl in range(N_LEVELS):
        m = CHUNK >> (lvl + 1)
        if m >= MXU_LEVEL_BELOW:
            parts = []
            for j in range(CHUNK // (2 * m)):
                lo_rows = b[2 * m * j:2 * m * j + m]
                hi_rows = b[2 * m * j + m:2 * m * (j + 1)]
                if backward:
                    piv = b[2 * m * j + m:2 * m * j + m + 1]
                    parts += [lo_rows - piv, piv - hi_rows]
                else:
                    piv = b[2 * m * j + m - 1:2 * m * j + m]
                    parts += [piv - lo_rows, hi_rows - piv]
            nl = jnp.concatenate(parts, axis=0)
        else:
            i = 1 + lvl - (N_LEVELS - N_MXU_LEVELS)
            nl = dec[i * CHUNK:(i + 1) * CHUNK]
        el = jnp.exp2(nl).astype(BF16)
        sc = jnp.dot(qb * el, _block_diag_t((kb * el).T), preferred_element_type=F32)
        sc = sc.astype(BF16) * m_ref[lvl]
        a = sc if a is None else a + sc
    o = o + jnp.dot(a, _block_diag(vb), preferred_element_type=F32)
    v_rows = jnp.concatenate([vb[:, :LANES], vb[:, LANES:]], axis=0)
    st_new = st * jnp.exp2(total) + lax.dot_general(v_rows, _block_diag(k_out), _TN,
                                                    preferred_element_type=F32)
    return o, st_new


def _hgrn_kernel(qf_ref, zf_ref, vf_ref, qb_ref, zb_ref, vb_ref, lb_ref, w_ref, m_ref,
                 of_ref, ob_ref, stf_ref, stb_ref):
    @pl.when(pl.program_id(0) == 0)
    def _():
        stf_ref[...] = jnp.zeros_like(stf_ref)
        stb_ref[...] = jnp.zeros_like(stb_ref)

    n_sub = qf_ref.shape[1] // CHUNK

    def sub_chunk(t, carry):
        rf = pl.ds(pl.multiple_of(t * CHUNK, CHUNK), CHUNK)
        rb = pl.ds(pl.multiple_of((n_sub - 1 - t) * CHUNK, CHUNK), CHUNK)
        for p in range(HGRN_HEADS // 2):
            o, st = _hgrn_chunk(qf_ref[p, rf, :], zf_ref[p, rf, :], vf_ref[p, rf, :], lb_ref[0, p],
                                w_ref[0], m_ref.at[0], stf_ref[p], False)
            of_ref[p, rf, :] = o
            stf_ref[p] = st
            o, st = _hgrn_chunk(qb_ref[p, rb, :], zb_ref[p, rb, :], vb_ref[p, rb, :], lb_ref[1, p],
                                w_ref[1], m_ref.at[1], stb_ref[p], True)
            ob_ref[p, rb, :] = o
            stb_ref[p] = st
        return carry

    lax.fori_loop(0, n_sub, sub_chunk, 0)


def _hgrn(proj_h, lb):
    s = proj_h.shape[1]
    n = s // HGRN_BLOCK
    pairs = HGRN_HEADS // 2
    wmat, masks = _hgrn_constants()
    hb = (pairs, HGRN_BLOCK, 2 * LANES)
    fwd = lambda slab: pl.BlockSpec(hb, lambda c: (slab, c, 0))
    bwd = lambda slab: pl.BlockSpec(hb, lambda c: (slab, n - 1 - c, 0))
    const = lambda a: pl.BlockSpec(a.shape, lambda c: (0,) * a.ndim)
    out_shape = jax.ShapeDtypeStruct((pairs, s, 2 * LANES), F32)
    state = pltpu.VMEM((pairs, LANES, 2 * LANES), F32)
    return pl.pallas_call(
        _hgrn_kernel,
        grid=(n,),
        in_specs=[fwd(0), fwd(1), fwd(3), bwd(0), bwd(2), bwd(3), const(lb), const(wmat), const(masks)],
        out_specs=[fwd(0), bwd(0)],
        out_shape=[out_shape, out_shape],
        scratch_shapes=[state, state],
        compiler_params=_params("arbitrary"),
        name="hgrn2_scan",
    )(proj_h, proj_h, proj_h, proj_h, proj_h, proj_h, lb, wmat, masks)


def _t5_bucket(rel):
    half = REL_BUCKETS // 2
    ret = (rel > 0).astype(np.int32) * half
    n = np.abs(rel)
    max_exact = half // 2
    large = max_exact + (np.log(np.maximum(n, 1) / max_exact)
                         / np.log(REL_MAX_DISTANCE / max_exact)
                         * (half - max_exact)).astype(np.int32)
    large = np.minimum(large, half - 1)
    return (ret + np.where(n < max_exact, n, large)).astype(np.int32)


def _attn_bias(rel_bias_table):
    span = 3 * ATTN_QBLK - 1
    rel = np.arange(span) - (ATTN_QBLK - 1) - ATTN_SIDE
    col = np.arange(2 * ATTN_QBLK)[None, :]
    band = np.abs(col - ATTN_SIDE - np.arange(ATTN_QBLK)[:, None]) <= ATTN_SIDE
    after_start, before_end = col >= ATTN_SIDE, col < 2 * ATTN_QBLK - ATTN_SIDE
    keep = [band, band & after_start, band & before_end, band & after_start & before_end]
    out = []
    for gi, (_, dil) in enumerate(ATTN_GROUPS):
        tab = rel_bias_table[:, gi * HEADS_PER_GROUP:(gi + 1) * HEADS_PER_GROUP].astype(F32)
        onehot = jnp.asarray(_t5_bucket(rel * dil)[:, None] == np.arange(REL_BUCKETS)[None, :], F32)
        vec = jnp.einsum("rb,bh->hr", onehot, tab, precision=lax.Precision.HIGHEST)
        period = span + 2
        vec = jnp.pad(vec, ((0, 0), (0, period - span)))
        flat = jnp.tile(vec, (1, ATTN_QBLK))[:, :ATTN_QBLK * (span + 1)]
        toep = flat.reshape(HEADS_PER_GROUP, ATTN_QBLK, span + 1)[:, :, ATTN_QBLK - 1:3 * ATTN_QBLK - 1]
        out.append(jnp.stack([jnp.where(jnp.asarray(k)[None], toep, NEG_INF) for k in keep], axis=1))
    return jnp.stack(out)


def _attn_kernel(qkv_hbm, kp_ref, kn_ref, vp_ref, vn_ref, bias_ref, *rest,
                 rows, dil, cpb, has_prev, final):
    rest = list(rest)
    op_ref, lp_ref = (rest.pop(0), rest.pop(0)) if has_prev else (None, None)
    o_ref = rest.pop(0)
    lse_ref = None if final else rest.pop(0)
    q_ring, k_ring, v_ring, sem = (rest.pop(0) for _ in range(4))
    o_s, l_s = rest if rest else (o_ref, lse_ref)
    c = pl.program_id(2)
    i = pl.program_id(1)
    n_i, n_c = pl.num_programs(1), pl.num_programs(2)
    step = (pl.program_id(0) * n_i + i) * n_c + c
    total = pl.num_programs(0) * n_i * n_c

    def block_copy(kind, ring):
        def copy(k):
            g, ik, ck = k // (n_i * n_c), (k // n_c) % n_i, k % n_c
            src = qkv_hbm.at[kind * HEADS_PER_GROUP + g, pl.ds(ck * cpb, cpb),
                             pl.ds(pl.multiple_of(ik * rows, rows), rows), :]
            return pltpu.make_async_copy(src, ring.at[k % WEIGHT_RING_SLOTS],
                                         sem.at[kind, k % WEIGHT_RING_SLOTS])
        return copy

    slot = _ring_fetch(block_copy(0, q_ring), step, total)
    _ring_fetch(block_copy(1, k_ring), step, total)
    _ring_fetch(block_copy(2, v_ring), step, total)
    q_ref, km_ref, vm_ref = q_ring.at[slot], k_ring.at[slot], v_ring.at[slot]
    scale = LANES ** -0.5
    n_sub = rows // ATTN_QBLK

    def window(prev_ref, main_ref, next_ref, cc, j):
        lo, hi = j * ATTN_QBLK - ATTN_SIDE, (j + 1) * ATTN_QBLK + ATTN_SIDE
        parts = []
        if lo < 0:
            parts.append(prev_ref[cc])
        parts.append(main_ref[cc, max(lo, 0):min(hi, rows)])
        if hi > rows:
            parts.append(next_ref[cc])
        return parts[0] if len(parts) == 1 else jnp.concatenate(parts, axis=0)

    tiles = [(cc, j) for cc in range(cpb) for j in range(n_sub)]
    for t0 in range(0, len(tiles), ATTN_QK_BATCH):
        batch = tiles[t0:t0 + ATTN_QK_BATCH]
        scores = []
        for cc, j in batch:
            r0 = j * ATTN_QBLK
            variant = 0
            if j == 0:
                variant = variant + (i == 0).astype(jnp.int32)
            if j == n_sub - 1:
                variant = variant + 2 * (i == pl.num_programs(1) - 1).astype(jnp.int32)
            scores.append(lax.dot_general(q_ref[cc, r0:r0 + ATTN_QBLK],
                                          window(kp_ref, km_ref, kn_ref, cc, j), _NT,
                                          preferred_element_type=F32) * scale + bias_ref[variant])
        for (cc, j), s in zip(batch, scores):
            r0 = j * ATTN_QBLK
            m = jnp.max(s, axis=-1, keepdims=True)
            p = jnp.exp(s - m)
            l = jnp.sum(p, axis=-1, keepdims=True)
            o = jnp.dot(p.astype(BF16), window(vp_ref, vm_ref, vn_ref, cc, j),
                        preferred_element_type=F32) / l
            lse = jnp.broadcast_to(m + jnp.log(l), (ATTN_QBLK, LANES))
            if dil == 1:
                o_s[r0:r0 + ATTN_QBLK] = o
                l_s[r0:r0 + ATTN_QBLK] = lse
            else:
                start = r0 * dil + c * cpb + cc
                o_s[pl.ds(start, ATTN_QBLK, stride=dil), :] = o
                l_s[pl.ds(start, ATTN_QBLK, stride=dil), :] = lse

    if has_prev:
        @pl.when(c == dil // cpb - 1)
        def _():
            def fold(t, carry):
                r = pl.ds(pl.multiple_of(t * ATTN_FOLD_ROWS, ATTN_FOLD_ROWS), ATTN_FOLD_ROWS)
                lp, lc = lp_ref[r, :], l_s[r, :]
                m = jnp.maximum(lp, lc)
                wp, wc = jnp.exp(lp - m), jnp.exp(lc - m)
                den = wp + wc
                o = (wp * op_ref[r, :] + wc * o_s[r, :]) / den
                o_ref[r, :] = o.astype(o_ref.dtype)
                if not final:
                    lse_ref[r, :] = m + jnp.log(den)
                return carry

            lax.fori_loop(0, rows * dil // ATTN_FOLD_ROWS, fold, 0)


def _attn_group(qkv, bias, gi, prev):
    dil, length = qkv.shape[1], qkv.shape[2]
    s = dil * length
    cpb = min(ATTN_RESIDUES, dil)
    rows = min(ATTN_TILES * ATTN_QBLK // cpb, length)
    nblk = length // rows
    sub = rows // ATTN_SIDE
    last = length // ATTN_SIDE - 1
    hp = HEADS_PER_GROUP
    final = gi == len(ATTN_GROUPS) - 1
    has_prev = prev is not None
    assert has_prev or dil == 1

    def prev_blk(kind):
        return pl.BlockSpec((None, cpb, ATTN_SIDE, LANES),
                            lambda g, i, c: (kind * hp + g, c, jnp.maximum(i * sub - 1, 0), 0))

    def next_blk(kind):
        return pl.BlockSpec((None, cpb, ATTN_SIDE, LANES),
                            lambda g, i, c: (kind * hp + g, c, jnp.minimum((i + 1) * sub, last), 0))

    run_spec = pl.BlockSpec((None, rows * dil, LANES), lambda g, i, c: (g, i, 0))
    run_shape = jax.ShapeDtypeStruct((hp, s, LANES), F32)
    in_specs = [pl.BlockSpec(memory_space=pl.ANY), prev_blk(1), next_blk(1), prev_blk(2), next_blk(2),
                pl.BlockSpec((None, None, 4, ATTN_QBLK, 2 * ATTN_QBLK), lambda g, i, c: (gi, g, 0, 0, 0))]
    args = [qkv] * 5 + [bias]
    if has_prev:
        in_specs += [run_spec, run_spec]
        args += list(prev)
    if final:
        out_specs = pl.BlockSpec((rows * dil, LANES), lambda g, i, c: (i, g))
        out_shape = jax.ShapeDtypeStruct((s, hp * LANES), BF16)
    else:
        out_specs = [run_spec, run_spec]
        out_shape = [run_shape, run_shape]
    ring = pltpu.VMEM((WEIGHT_RING_SLOTS, cpb, rows, LANES), qkv.dtype)
    scratch = [ring, ring, ring, pltpu.SemaphoreType.DMA((3, WEIGHT_RING_SLOTS))]
    if has_prev:
        scratch += [pltpu.VMEM((rows * dil, LANES), F32), pltpu.VMEM((rows * dil, LANES), F32)]
    return pl.pallas_call(
        functools.partial(_attn_kernel, rows=rows, dil=dil, cpb=cpb, has_prev=has_prev, final=final),
        grid=(hp, nblk, dil // cpb),
        in_specs=in_specs,
        out_specs=out_specs,
        out_shape=out_shape,
        scratch_shapes=scratch,
        compiler_params=_params("arbitrary", "arbitrary", "arbitrary"),
        name=f"dilated_attn_g{gi}",
    )(*args)


def _merge_kernel(of_hbm, ob_hbm, ph_hbm, oa_ref, gate_hbm, x_hbm, pa_ref, pb_ref, wo_ref, hw_ref,
                  nw_ref, xo_ref, ho_ref, of_ring, ob_ring, g_ring, gate_ring, x_ring, sem):
    step, total = pl.program_id(0), pl.num_programs(0)
    tm, d = x_ring.shape[1:]
    pairs = HGRN_HEADS // 2

    def rows(k):
        return pl.ds(pl.multiple_of(k * tm, tm), tm)

    def slab_copy(hbm, first_slab, ring, n):
        return lambda k: pltpu.make_async_copy(hbm.at[pl.ds(first_slab, pairs), rows(k), :],
                                               ring.at[k % WEIGHT_RING_SLOTS],
                                               sem.at[n, k % WEIGHT_RING_SLOTS])

    def row_copy(hbm, ring, n):
        return lambda k: pltpu.make_async_copy(hbm.at[rows(k), :], ring.at[k % WEIGHT_RING_SLOTS],
                                               sem.at[n, k % WEIGHT_RING_SLOTS])

    slot = _ring_fetch(slab_copy(of_hbm, 0, of_ring, 0), step, total)
    _ring_fetch(slab_copy(ob_hbm, 0, ob_ring, 1), step, total)
    _ring_fetch(slab_copy(ph_hbm, 4 * pairs, g_ring, 2), step, total)
    _ring_fetch(row_copy(gate_hbm, gate_ring, 3), step, total)
    _ring_fetch(row_copy(x_hbm, x_ring, 4), step, total)
    of_ref, ob_ref, g_ref = of_ring.at[slot], ob_ring.at[slot], g_ring.at[slot]
    gate_ref, x_ref = gate_ring.at[slot], x_ring.at[slot]
    b = jnp.dot(oa_ref[...], pb_ref[...], preferred_element_type=F32)
    a = None
    for h0 in range(0, HGRN_HEADS, MERGE_HEAD_GROUP):
        cols = []
        for h in range(h0, h0 + MERGE_HEAD_GROUP):
            lanes = slice((h % 2) * LANES, (h % 2 + 1) * LANES)
            o = of_ref[h // 2, :, lanes] + ob_ref[h // 2, :, lanes]
            o = o * lax.rsqrt(jnp.mean(o * o, axis=-1, keepdims=True) + NORM_EPS)
            g = g_ref[h // 2, :, lanes]
            cols.append((o * hw_ref[h] * (g * _sigmoid(g))).astype(BF16))
        part = jnp.dot(jnp.concatenate(cols, axis=1), pa_ref[h0 * LANES:(h0 + MERGE_HEAD_GROUP) * LANES, :],
                       preferred_element_type=F32)
        a = part if a is None else a + part
    merged = _sigmoid(gate_ref[:, :d]) * a.astype(BF16) + _sigmoid(gate_ref[:, d:]) * b.astype(BF16)
    xn = x_ref[...] + jnp.dot(merged, wo_ref[...], preferred_element_type=F32)
    xo_ref[...] = xn
    ho_ref[...] = _rms(xn, nw_ref[...]).astype(ho_ref.dtype)


def _merge(o_hgrn, proj_h, o_attn, gates, x, pa, pb, wo, hgrn_norm_w, norm_w, tm=256):
    s, d = x.shape
    pairs = HGRN_HEADS // 2
    hbm = pl.BlockSpec(memory_space=pl.ANY)

    def wspec(rows):
        return pl.BlockSpec((rows, d), lambda i: (0, 0), pipeline_mode=pl.Buffered(1))

    return pl.pallas_call(
        _merge_kernel,
        grid=(s // tm,),
        in_specs=[hbm, hbm, hbm,
                  pl.BlockSpec((tm, ATTN_OUT_WIDTH), lambda i: (i, 0)),
                  hbm, hbm,
                  wspec(HGRN_WIDTH), wspec(ATTN_OUT_WIDTH), wspec(d),
                  pl.BlockSpec((HGRN_HEADS, 1, LANES), lambda i: (0, 0, 0)),
                  pl.BlockSpec((1, d), lambda i: (0, 0))],
        out_specs=[pl.BlockSpec((tm, d), lambda i: (i, 0)),
                   pl.BlockSpec((tm, d), lambda i: (i, 0))],
        out_shape=[jax.ShapeDtypeStruct((s, d), F32), jax.ShapeDtypeStruct((s, d), BF16)],
        scratch_shapes=[pltpu.VMEM((WEIGHT_RING_SLOTS, pairs, tm, 2 * LANES), F32),
                        pltpu.VMEM((WEIGHT_RING_SLOTS, pairs, tm, 2 * LANES), F32),
                        pltpu.VMEM((WEIGHT_RING_SLOTS, pairs, tm, 2 * LANES), F32),
                        pltpu.VMEM((WEIGHT_RING_SLOTS, tm, 2 * d), gates.dtype),
                        pltpu.VMEM((WEIGHT_RING_SLOTS, tm, d), F32),
                        pltpu.SemaphoreType.DMA((5, WEIGHT_RING_SLOTS))],
        compiler_params=_params("arbitrary"),
        name="merge_out_proj",
    )(o_hgrn[0], o_hgrn[1], proj_h, o_attn, gates, x, pa, pb, wo,
      hgrn_norm_w.reshape(HGRN_HEADS, 1, LANES), norm_w.reshape(1, d))


def _mlp_kernel(h_ref, x_ref, wu_hbm, wd_hbm, nw_ref, xo_ref, ho_ref, acc, wu_ring, wd_ring, sem):
    f = pl.program_id(1)
    tf = wu_ring.shape[2]

    def up_copy(k):
        cols = pl.ds(pl.multiple_of((k % pl.num_programs(1)) * tf, tf), tf)
        slot = k % WEIGHT_RING_SLOTS
        return pltpu.make_async_copy(wu_hbm.at[:, cols], wu_ring.at[slot], sem.at[0, slot])

    def down_copy(k):
        rows = pl.ds(pl.multiple_of((k % pl.num_programs(1)) * tf, tf), tf)
        slot = k % WEIGHT_RING_SLOTS
        return pltpu.make_async_copy(wd_hbm.at[rows, :], wd_ring.at[slot], sem.at[1, slot])

    slot = _ring_fetch(up_copy)
    _ring_fetch(down_copy)

    @pl.when(f == 0)
    def _():
        acc[...] = x_ref[...]

    u = jnp.maximum(jnp.dot(h_ref[...], wu_ring[slot], preferred_element_type=F32), 0.0)
    acc[...] += jnp.dot((u * u).astype(BF16), wd_ring[slot], preferred_element_type=F32)

    @pl.when(f == pl.num_programs(1) - 1)
    def _():
        xn = acc[...]
        xo_ref[...] = xn
        ho_ref[...] = _rms(xn, nw_ref[...]).astype(ho_ref.dtype)


def _mlp(h, x, wu, wd, norm_w, h_dtype, tm=512, tf=1024):
    s, d = x.shape
    ff = wu.shape[1]
    return pl.pallas_call(
        _mlp_kernel,
        grid=(s // tm, ff // tf),
        in_specs=[pl.BlockSpec((tm, d), lambda i, f: (i, 0)),
                  pl.BlockSpec((tm, d), lambda i, f: (i, 0)),
                  pl.BlockSpec(memory_space=pl.ANY),
                  pl.BlockSpec(memory_space=pl.ANY),
                  pl.BlockSpec((1, d), lambda i, f: (0, 0))],
        out_specs=[pl.BlockSpec((tm, d), lambda i, f: (i, 0)),
                   pl.BlockSpec((tm, d), lambda i, f: (i, 0))],
        out_shape=[jax.ShapeDtypeStruct((s, d), F32), jax.ShapeDtypeStruct((s, d), h_dtype)],
        scratch_shapes=[pltpu.VMEM((tm, d), F32),
                        pltpu.VMEM((WEIGHT_RING_SLOTS, d, tf), wu.dtype),
                        pltpu.VMEM((WEIGHT_RING_SLOTS, tf, d), wd.dtype),
                        pltpu.SemaphoreType.DMA((2, WEIGHT_RING_SLOTS))],
        compiler_params=pltpu.CompilerParams(dimension_semantics=("arbitrary", "arbitrary"),
                                             vmem_limit_bytes=MLP_VMEM_LIMIT),
        name="mlp_relu2",
    )(h, x, wu, wd, norm_w.reshape(1, d))


def _lower_bounds(logits):
    p = jax.nn.softmax(logits.astype(F32), axis=0)
    return jnp.cumsum(p, axis=0) - p[0:1]


def kernel(x, w_in, hgrn_lb_fwd, hgrn_lb_bwd, hgrn_norm_w, rel_bias_table, w_branch_hgrn,
           w_branch_attn, w_out, norm_mix_w, norm_mlp_w, w_up, w_down, final_norm_w):
    batch, s, d = x.shape
    depth = w_in.shape[0]
    assert batch == 1
    x = x.reshape(s, d)
    lb = jnp.stack([_lower_bounds(hgrn_lb_fwd), _lower_bounds(hgrn_lb_bwd)], axis=1)
    lb = lb.reshape(-1, 2, HGRN_HEADS // 2, 1, 2 * LANES)
    bias = _attn_bias(rel_bias_table)
    n_h = 5 * HGRN_WIDTH
    n_a = 3 * ATTN_WIDTH
    cast_rows = 256
    h = _norm(x, norm_mix_w[0])
    for l in range(depth):
        proj_h, wd_b = _proj(h, w_in, l, 0, n_h, F32, 2 * LANES, "proj_hgrn",
                             (w_down, 0, w_down.shape[1] // cast_rows))
        gates, wu_b = _proj(h, w_in, l, n_h + n_a, 2 * d, BF16, 0, "proj_gates",
                            (w_up, 1, w_up.shape[2] // cast_rows))
        o_hgrn = _hgrn(proj_h, lb[l])
        attn = None
        cast = []
        for gi, wt in enumerate((w_out, w_branch_hgrn, w_branch_attn)):
            qkv, wt_b = _proj_attn(h, w_in, l, n_h, gi, (wt, 0, wt.shape[1] // cast_rows))
            attn = _attn_group(qkv, bias, gi, attn)
            cast.append(wt_b)
        wo_b, pa_b, pb_b = cast
        x, h2 = _merge(o_hgrn, proj_h, attn, gates, x, pa_b, pb_b, wo_b, hgrn_norm_w[l],
                       norm_mlp_w[l])
        last = l == depth - 1
        x, h = _mlp(h2, x, wu_b, wd_b, final_norm_w if last else norm_mix_w[l + 1],
                    F32 if last else BF16)
    return h.reshape(batch, s, d)
```

```python
import functools

import jax
import jax.numpy as jnp
import numpy as np
from jax import lax
from jax.experimental import pallas as pl
from jax.experimental.pallas import tpu as pltpu

F32 = jnp.float32
BF16 = jnp.bfloat16

LANES = 128
HGRN_HEADS = 8
HGRN_WIDTH = HGRN_HEADS * LANES
CHUNK = 128
HGRN_BLOCK = 4 * CHUNK
N_LEVELS = 7
MXU_LEVEL_BELOW = 8
N_MXU_LEVELS = 3
MIN_FORGET = 1e-30
ATTN_GROUPS = ((128, 1), (512, 4), (2048, 16))
HEADS_PER_GROUP = 4
N_ATTN_HEADS = HEADS_PER_GROUP * len(ATTN_GROUPS)
ATTN_WIDTH = N_ATTN_HEADS * LANES
ATTN_OUT_WIDTH = HEADS_PER_GROUP * LANES
ATTN_QBLK = 128
ATTN_SIDE = 64
ATTN_TILES = 32
ATTN_QK_BATCH = 8
ATTN_RESIDUES = 8
ATTN_FOLD_ROWS = 256
REL_BUCKETS = 32
REL_MAX_DISTANCE = 1024
NORM_EPS = 1e-6
NEG_INF = -1e30
MERGE_HEAD_GROUP = 4
REGROUP_ROWS = 256
REGROUP_MATMUL_MIN_DIL = 8
WEIGHT_RING_SLOTS = 3
PROJ_ROW_TILES = 4
VMEM_LIMIT = 56 * 1024 * 1024
MLP_VMEM_LIMIT = 60 * 1024 * 1024

_NT = (((1,), (1,)), ((), ()))
_TN = (((0,), (0,)), ((), ()))


def _params(*sem):
    return pltpu.CompilerParams(dimension_semantics=sem, vmem_limit_bytes=VMEM_LIMIT)


def _rms(x, w):
    return x * lax.rsqrt(jnp.mean(x * x, axis=-1, keepdims=True) + NORM_EPS) * w


def _sigmoid(x):
    return 0.5 * jnp.tanh(0.5 * x) + 0.5


def _norm_kernel(x_ref, w_ref, o_ref):
    o_ref[...] = _rms(x_ref[...], w_ref[...]).astype(o_ref.dtype)


def _norm(x, w, tm=512):
    s, d = x.shape
    return pl.pallas_call(
        _norm_kernel,
        grid=(s // tm,),
        in_specs=[pl.BlockSpec((tm, d), lambda i: (i, 0)),
                  pl.BlockSpec((1, d), lambda i: (0, 0))],
        out_specs=pl.BlockSpec((tm, d), lambda i: (i, 0)),
        out_shape=jax.ShapeDtypeStruct((s, d), BF16),
        compiler_params=_params("parallel"),
        name="rmsnorm_in",
    )(x, w.reshape(1, d))


def _cast_rider(cast_in, cast_out, n_chunks):
    step = pl.program_id(0) * pl.num_programs(1) + pl.program_id(1)

    @pl.when(step < n_chunks)
    def _():
        cast_out[...] = cast_in[...].astype(cast_out.dtype)


def _rider_specs(rider, layer, n_j):
    w, axis, n_chunks = rider
    r, c = w.shape[1:]

    def chunk(i, j):
        return jnp.minimum(i * n_j + j, n_chunks - 1)

    if axis == 0:
        blk = (r // n_chunks, c)
        in_spec = pl.BlockSpec((None,) + blk, lambda i, j: (layer, chunk(i, j), 0))
        out_spec = pl.BlockSpec(blk, lambda i, j: (chunk(i, j), 0))
    else:
        blk = (r, c // n_chunks)
        in_spec = pl.BlockSpec((None,) + blk, lambda i, j: (layer, 0, chunk(i, j)))
        out_spec = pl.BlockSpec(blk, lambda i, j: (0, chunk(i, j)))
    return in_spec, out_spec, jax.ShapeDtypeStruct((r, c), BF16)


def _ring_fetch(tile_copy):
    n_j = pl.num_programs(1)
    step = pl.program_id(0) * n_j + pl.program_id(1)
    total = pl.num_programs(0) * n_j

    @pl.when(step == 0)
    def _():
        for k in range(WEIGHT_RING_SLOTS - 1):
            tile_copy(k).start()

    @pl.when(step + WEIGHT_RING_SLOTS - 1 < total)
    def _():
        tile_copy(step + WEIGHT_RING_SLOTS - 1).start()

    tile_copy(step).wait()
    return step % WEIGHT_RING_SLOTS


def _proj_kernel(x_ref, w_hbm, cast_in, o_ref, cast_out, w_ring, sem, *, slabs, n_chunks, layer, col0):
    _cast_rider(cast_in, cast_out, n_chunks)
    tn = w_ring.shape[2]

    def tile_copy(k):
        cols = pl.ds(pl.multiple_of(col0 + (k % pl.num_programs(1)) * tn, tn), tn)
        slot = k % WEIGHT_RING_SLOTS
        return pltpu.make_async_copy(w_hbm.at[layer, :, cols], w_ring.at[slot], sem.at[slot])

    slot = _ring_fetch(tile_copy)
    acc = jnp.dot(x_ref[...], w_ring[slot].astype(BF16), preferred_element_type=F32)
    if slabs:
        width = o_ref.shape[-1]
        for j in range(slabs):
            o_ref[j] = acc[:, j * width:(j + 1) * width].astype(o_ref.dtype)
    else:
        o_ref[...] = acc.astype(o_ref.dtype)


def _proj(h, w, layer, col0, n, out_dtype, slab_width, name, rider, tn=512):
    s, d = h.shape
    tm = s // PROJ_ROW_TILES
    n_j = n // tn
    assert rider[2] <= PROJ_ROW_TILES * n_j
    if slab_width:
        slabs = tn // slab_width
        out_shape = jax.ShapeDtypeStruct((n // slab_width, s, slab_width), out_dtype)
        out_spec = pl.BlockSpec((slabs, tm, slab_width), lambda i, j: (j, i, 0))
    else:
        slabs = 0
        out_shape = jax.ShapeDtypeStruct((s, n), out_dtype)
        out_spec = pl.BlockSpec((tm, tn), lambda i, j: (i, j))
    r_in, r_out, r_shape = _rider_specs(rider, layer, n_j)
    assert PROJ_ROW_TILES * n_j >= WEIGHT_RING_SLOTS
    return pl.pallas_call(
        functools.partial(_proj_kernel, slabs=slabs, n_chunks=rider[2], layer=layer, col0=col0),
        grid=(s // tm, n_j),
        in_specs=[pl.BlockSpec((tm, d), lambda i, j: (i, 0)),
                  pl.BlockSpec(memory_space=pl.ANY), r_in],
        out_specs=[out_spec, r_out],
        out_shape=[out_shape, r_shape],
        scratch_shapes=[pltpu.VMEM((WEIGHT_RING_SLOTS, d, tn), w.dtype),
                        pltpu.SemaphoreType.DMA((WEIGHT_RING_SLOTS,))],
        compiler_params=_params("arbitrary", "arbitrary"),
        name=name,
    )(h, w, rider[0])


def _regroup_matrix(dil):
    r = REGROUP_ROWS // dil
    out_row = np.arange(REGROUP_ROWS)
    src = (out_row % r) * dil + out_row // r
    return jnp.asarray(src[:, None] == np.arange(REGROUP_ROWS)[None, :], BF16)


def _proj_attn_kernel(x_ref, w_hbm, cast_in, perm_ref, o_ref, cast_out, scr, w_ring, sem, *,
                      dil, n_chunks, layer, col0, col_stride):
    _cast_rider(cast_in, cast_out, n_chunks)
    tn = w_ring.shape[2]

    def tile_copy(k):
        cols = pl.ds(pl.multiple_of(col0 + (k % pl.num_programs(1)) * col_stride, tn), tn)
        slot = k % WEIGHT_RING_SLOTS
        return pltpu.make_async_copy(w_hbm.at[layer, :, cols], w_ring.at[slot], sem.at[slot])

    slot = _ring_fetch(tile_copy)
    acc = jnp.dot(x_ref[...], w_ring[slot].astype(BF16), preferred_element_type=F32)
    if dil == 1:
        for hh in range(HEADS_PER_GROUP):
            o_ref[hh, 0] = acc[:, hh * LANES:(hh + 1) * LANES].astype(o_ref.dtype)
    elif dil < REGROUP_MATMUL_MIN_DIL:
        rows = scr.shape[1] // dil
        for hh in range(HEADS_PER_GROUP):
            scr[hh] = acc[:, hh * LANES:(hh + 1) * LANES]
        for hh in range(HEADS_PER_GROUP):
            for c in range(dil):
                o_ref[hh, c] = scr[hh, pl.ds(c, rows, stride=dil), :].astype(o_ref.dtype)
    else:
        r = REGROUP_ROWS // dil
        rounded = acc.astype(BF16)
        for b in range(acc.shape[0] // REGROUP_ROWS):
            blk = jnp.dot(perm_ref[...], rounded[b * REGROUP_ROWS:(b + 1) * REGROUP_ROWS],
                          preferred_element_type=F32).astype(o_ref.dtype)
            for hh in range(HEADS_PER_GROUP):
                for c in range(dil):
                    o_ref[hh, c, b * r:(b + 1) * r, :] = blk[c * r:(c + 1) * r,
                                                             hh * LANES:(hh + 1) * LANES]


def _proj_attn(h, w, layer, col0, gi, rider):
    s, d = h.shape
    tm = s // PROJ_ROW_TILES
    assert rider[2] <= PROJ_ROW_TILES * 3
    dil = ATTN_GROUPS[gi][1]
    tn = ATTN_OUT_WIDTH
    n_groups = len(ATTN_GROUPS)
    r_in, r_out, r_shape = _rider_specs(rider, layer, 3)
    perm = _regroup_matrix(dil)
    return pl.pallas_call(
        functools.partial(_proj_attn_kernel, dil=dil, n_chunks=rider[2], layer=layer,
                          col0=col0 + gi * tn, col_stride=n_groups * tn),
        grid=(s // tm, 3),
        in_specs=[pl.BlockSpec((tm, d), lambda i, j: (i, 0)),
                  pl.BlockSpec(memory_space=pl.ANY), r_in,
                  pl.BlockSpec(perm.shape, lambda i, j: (0, 0))],
        out_specs=[pl.BlockSpec((HEADS_PER_GROUP, dil, tm // dil, LANES), lambda i, j: (j, 0, i, 0)),
                   r_out],
        out_shape=[jax.ShapeDtypeStruct((3 * HEADS_PER_GROUP, dil, s // dil, LANES), BF16), r_shape],
        scratch_shapes=[pltpu.VMEM((HEADS_PER_GROUP, tm, LANES), F32),
                        pltpu.VMEM((WEIGHT_RING_SLOTS, d, tn), w.dtype),
                        pltpu.SemaphoreType.DMA((WEIGHT_RING_SLOTS,))],
        compiler_params=_params("arbitrary", "arbitrary"),
        name=f"proj_attn_g{gi}",
    )(h, w, rider[0], perm)


def _hgrn_constants():
    c = CHUNK
    t = np.arange(c)[:, None]
    s = np.arange(c)[None, :]
    mats = [s <= t]
    masks = []
    for lvl in range(N_LEVELS):
        m = c >> (lvl + 1)
        p = (t // (2 * m)) * (2 * m) + m - 1
        if m < MXU_LEVEL_BELOW:
            mats.append(((t > p) & (s > p) & (s <= t)) | ((t <= p) & (s > t) & (s <= p)))
        masks.append((t // (2 * m) == s // (2 * m)) & (t % (2 * m) >= m) & (s % (2 * m) < m))
    w = np.stack(mats).astype(np.float32)
    k = np.stack(masks).astype(np.float32)
    w = np.stack([w, w[:, ::-1, ::-1]]).reshape(2, len(mats) * c, c)
    w = np.concatenate([w, w], axis=2)
    k = np.stack([k, k[:, ::-1, ::-1]])
    k = np.concatenate([k, k], axis=3)
    return jnp.asarray(w, BF16), jnp.asarray(k, BF16)


def _block_diag(x):
    zero = jnp.zeros((x.shape[0], LANES), x.dtype)
    return jnp.concatenate([jnp.concatenate([x[:, :LANES], zero], axis=1),
                            jnp.concatenate([zero, x[:, LANES:]], axis=1)], axis=0)


def _block_diag_t(x):
    zero = jnp.zeros((LANES, x.shape[1]), x.dtype)
    return jnp.concatenate([jnp.concatenate([x[:LANES], zero], axis=1),
                            jnp.concatenate([zero, x[LANES:]], axis=1)], axis=0)


def _hgrn_chunk(q, z, v, lb, wmat, m_ref, st, backward):
    half = 0.5 * (1.0 - lb)
    ct = half * jnp.tanh(0.5 * z)
    kk = half - ct
    logf = jnp.log2(jnp.maximum((lb + half) + ct, MIN_FORGET))
    hi = logf.astype(BF16)
    lo = (logf - hi.astype(F32)).astype(BF16)
    dec = jnp.dot(wmat, jnp.concatenate([hi, lo], axis=0), preferred_element_type=F32)
    b = dec[0:CHUNK]
    total = b[0:1] if backward else b[CHUNK - 1:CHUNK]
    q_in = (q * jnp.exp2(b)).astype(BF16)
    k_out = (kk * jnp.exp2(total - b)).astype(BF16)
    vb = v.astype(BF16)
    qb = q.astype(BF16)
    kb = kk.astype(BF16)
    o = jnp.dot(q_in, _block_diag_t(st.astype(BF16).T), preferred_element_type=F32)
    qk = q * kk
    diag = jnp.concatenate(
        [jnp.broadcast_to(jnp.sum(qk[:, :LANES], axis=-1, keepdims=True), (CHUNK, LANES)),
         jnp.broadcast_to(jnp.sum(qk[:, LANES:], axis=-1, keepdims=True), (CHUNK, LANES))], axis=1)
    o = o + diag * v
    a = None
    for lvl in range(N_LEVELS):
        m = CHUNK >> (lvl + 1)
        if m >= MXU_LEVEL_BELOW:
            parts = []
            for j in range(CHUNK // (2 * m)):
                lo_rows = b[2 * m * j:2 * m * j + m]
                hi_rows = b[2 * m * j + m:2 * m * (j + 1)]
                if backward:
                    piv = b[2 * m * j + m:2 * m * j + m + 1]
                    parts += [lo_rows - piv, piv - hi_rows]
                else:
                    piv = b[2 * m * j + m - 1:2 * m * j + m]
                    parts += [piv - lo_rows, hi_rows - piv]
            nl = jnp.concatenate(parts, axis=0)
        else:
            i = 1 + lvl - (N_LEVELS - N_MXU_LEVELS)
            nl = dec[i * CHUNK:(i + 1) * CHUNK]
        el = jnp.exp2(nl).astype(BF16)
        sc = jnp.dot(qb * el, _block_diag_t((kb * el).T), preferred_element_type=F32)
        sc = sc.astype(BF16) * m_ref[lvl]
        a = sc if a is None else a + sc
    o = o + jnp.dot(a, _block_diag(vb), preferred_element_type=F32)
    v_rows = jnp.concatenate([vb[:, :LANES], vb[:, LANES:]], axis=0)
    st_new = st * jnp.exp2(total) + lax.dot_general(v_rows, _block_diag(k_out), _TN,
                                                    preferred_element_type=F32)
    return o, st_new


def _hgrn_kernel(qf_ref, zf_ref, vf_ref, qb_ref, zb_ref, vb_ref, lb_ref, w_ref, m_ref,
                 of_ref, ob_ref, stf_ref, stb_ref):
    @pl.when(pl.program_id(0) == 0)
    def _():
        stf_ref[...] = jnp.zeros_like(stf_ref)
        stb_ref[...] = jnp.zeros_like(stb_ref)

    n_sub = qf_ref.shape[1] // CHUNK

    def sub_chunk(t, carry):
        rf = pl.ds(pl.multiple_of(t * CHUNK, CHUNK), CHUNK)
        rb = pl.ds(pl.multiple_of((n_sub - 1 - t) * CHUNK, CHUNK), CHUNK)
        for p in range(HGRN_HEADS // 2):
            o, st = _hgrn_chunk(qf_ref[p, rf, :], zf_ref[p, rf, :], vf_ref[p, rf, :], lb_ref[0, p],
                                w_ref[0], m_ref.at[0], stf_ref[p], False)
            of_ref[p, rf, :] = o
            stf_ref[p] = st
            o, st = _hgrn_chunk(qb_ref[p, rb, :], zb_ref[p, rb, :], vb_ref[p, rb, :], lb_ref[1, p],
                                w_ref[1], m_ref.at[1], stb_ref[p], True)
            ob_ref[p, rb, :] = o
            stb_ref[p] = st
        return carry

    lax.fori_loop(0, n_sub, sub_chunk, 0)


def _hgrn(proj_h, lb):
    s = proj_h.shape[1]
    n = s // HGRN_BLOCK
    pairs = HGRN_HEADS // 2
    wmat, masks = _hgrn_constants()
    hb = (pairs, HGRN_BLOCK, 2 * LANES)
    fwd = lambda slab: pl.BlockSpec(hb, lambda c: (slab, c, 0))
    bwd = lambda slab: pl.BlockSpec(hb, lambda c: (slab, n - 1 - c, 0))
    const = lambda a: pl.BlockSpec(a.shape, lambda c: (0,) * a.ndim)
    out_shape = jax.ShapeDtypeStruct((pairs, s, 2 * LANES), F32)
    state = pltpu.VMEM((pairs, LANES, 2 * LANES), F32)
    return pl.pallas_call(
        _hgrn_kernel,
        grid=(n,),
        in_specs=[fwd(0), fwd(1), fwd(3), bwd(0), bwd(2), bwd(3), const(lb), const(wmat), const(masks)],
        out_specs=[fwd(0), bwd(0)],
        out_shape=[out_shape, out_shape],
        scratch_shapes=[state, state],
        compiler_params=_params("arbitrary"),
        name="hgrn2_scan",
    )(proj_h, proj_h, proj_h, proj_h, proj_h, proj_h, lb, wmat, masks)


def _t5_bucket(rel):
    half = REL_BUCKETS // 2
    ret = (rel > 0).astype(np.int32) * half
    n = np.abs(rel)
    max_exact = half // 2
    large = max_exact + (np.log(np.maximum(n, 1) / max_exact)
                         / np.log(REL_MAX_DISTANCE / max_exact)
                         * (half - max_exact)).astype(np.int32)
    large = np.minimum(large, half - 1)
    return (ret + np.where(n < max_exact, n, large)).astype(np.int32)


def _attn_bias(rel_bias_table):
    span = 3 * ATTN_QBLK - 1
    rel = np.arange(span) - (ATTN_QBLK - 1) - ATTN_SIDE
    col = np.arange(2 * ATTN_QBLK)[None, :]
    band = np.abs(col - ATTN_SIDE - np.arange(ATTN_QBLK)[:, None]) <= ATTN_SIDE
    after_start, before_end = col >= ATTN_SIDE, col < 2 * ATTN_QBLK - ATTN_SIDE
    keep = [band, band & after_start, band & before_end, band & after_start & before_end]
    out = []
    for gi, (_, dil) in enumerate(ATTN_GROUPS):
        tab = rel_bias_table[:, gi * HEADS_PER_GROUP:(gi + 1) * HEADS_PER_GROUP].astype(F32)
        onehot = jnp.asarray(_t5_bucket(rel * dil)[:, None] == np.arange(REL_BUCKETS)[None, :], F32)
        vec = jnp.einsum("rb,bh->hr", onehot, tab, precision=lax.Precision.HIGHEST)
        period = span + 2
        vec = jnp.pad(vec, ((0, 0), (0, period - span)))
        flat = jnp.tile(vec, (1, ATTN_QBLK))[:, :ATTN_QBLK * (span + 1)]
        toep = flat.reshape(HEADS_PER_GROUP, ATTN_QBLK, span + 1)[:, :, ATTN_QBLK - 1:3 * ATTN_QBLK - 1]
        out.append(jnp.stack([jnp.where(jnp.asarray(k)[None], toep, NEG_INF) for k in keep], axis=1))
    return jnp.stack(out)


def _attn_kernel(q_ref, kp_ref, km_ref, kn_ref, vp_ref, vm_ref, vn_ref, bias_ref, *rest,
                 rows, dil, cpb, has_prev, final):
    rest = list(rest)
    op_ref, lp_ref = (rest.pop(0), rest.pop(0)) if has_prev else (None, None)
    o_ref = rest.pop(0)
    lse_ref = None if final else rest.pop(0)
    o_s, l_s = rest if rest else (o_ref, lse_ref)
    c = pl.program_id(2)
    i = pl.program_id(1)
    scale = LANES ** -0.5
    n_sub = rows // ATTN_QBLK

    def window(prev_ref, main_ref, next_ref, cc, j):
        lo, hi = j * ATTN_QBLK - ATTN_SIDE, (j + 1) * ATTN_QBLK + ATTN_SIDE
        parts = []
        if lo < 0:
            parts.append(prev_ref[cc])
        parts.append(main_ref[cc, max(lo, 0):min(hi, rows)])
        if hi > rows:
            parts.append(next_ref[cc])
        return parts[0] if len(parts) == 1 else jnp.concatenate(parts, axis=0)

    tiles = [(cc, j) for cc in range(cpb) for j in range(n_sub)]
    for t0 in range(0, len(tiles), ATTN_QK_BATCH):
        batch = tiles[t0:t0 + ATTN_QK_BATCH]
        scores = []
        for cc, j in batch:
            r0 = j * ATTN_QBLK
            variant = 0
            if j == 0:
                variant = variant + (i == 0).astype(jnp.int32)
            if j == n_sub - 1:
                variant = variant + 2 * (i == pl.num_programs(1) - 1).astype(jnp.int32)
            scores.append(lax.dot_general(q_ref[cc, r0:r0 + ATTN_QBLK],
                                          window(kp_ref, km_ref, kn_ref, cc, j), _NT,
                                          preferred_element_type=F32) * scale + bias_ref[variant])
        for (cc, j), s in zip(batch, scores):
            r0 = j * ATTN_QBLK
            m = jnp.max(s, axis=-1, keepdims=True)
            p = jnp.exp(s - m)
            l = jnp.sum(p, axis=-1, keepdims=True)
            o = jnp.dot(p.astype(BF16), window(vp_ref, vm_ref, vn_ref, cc, j),
                        preferred_element_type=F32) / l
            lse = jnp.broadcast_to(m + jnp.log(l), (ATTN_QBLK, LANES))
            if dil == 1:
                o_s[r0:r0 + ATTN_QBLK] = o
                l_s[r0:r0 + ATTN_QBLK] = lse
            else:
                start = r0 * dil + c * cpb + cc
                o_s[pl.ds(start, ATTN_QBLK, stride=dil), :] = o
                l_s[pl.ds(start, ATTN_QBLK, stride=dil), :] = lse

    if has_prev:
        @pl.when(c == dil // cpb - 1)
        def _():
            def fold(t, carry):
                r = pl.ds(pl.multiple_of(t * ATTN_FOLD_ROWS, ATTN_FOLD_ROWS), ATTN_FOLD_ROWS)
                lp, lc = lp_ref[r, :], l_s[r, :]
                m = jnp.maximum(lp, lc)
                wp, wc = jnp.exp(lp - m), jnp.exp(lc - m)
                den = wp + wc
                o = (wp * op_ref[r, :] + wc * o_s[r, :]) / den
                o_ref[r, :] = o.astype(o_ref.dtype)
                if not final:
                    lse_ref[r, :] = m + jnp.log(den)
                return carry

            lax.fori_loop(0, rows * dil // ATTN_FOLD_ROWS, fold, 0)


def _attn_group(qkv, bias, gi, prev):
    dil, length = qkv.shape[1], qkv.shape[2]
    s = dil * length
    cpb = min(ATTN_RESIDUES, dil)
    rows = min(ATTN_TILES * ATTN_QBLK // cpb, length)
    nblk = length // rows
    sub = rows // ATTN_SIDE
    last = length // ATTN_SIDE - 1
    hp = HEADS_PER_GROUP
    final = gi == len(ATTN_GROUPS) - 1
    has_prev = prev is not None
    assert has_prev or dil == 1

    def main(kind):
        return pl.BlockSpec((None, cpb, rows, LANES), lambda g, i, c: (kind * hp + g, c, i, 0))

    def prev_blk(kind):
        return pl.BlockSpec((None, cpb, ATTN_SIDE, LANES),
                            lambda g, i, c: (kind * hp + g, c, jnp.maximum(i * sub - 1, 0), 0))

    def next_blk(kind):
        return pl.BlockSpec((None, cpb, ATTN_SIDE, LANES),
                            lambda g, i, c: (kind * hp + g, c, jnp.minimum((i + 1) * sub, last), 0))

    run_spec = pl.BlockSpec((None, rows * dil, LANES), lambda g, i, c: (g, i, 0))
    run_shape = jax.ShapeDtypeStruct((hp, s, LANES), F32)
    in_specs = [main(0), prev_blk(1), main(1), next_blk(1), prev_blk(2), main(2), next_blk(2),
                pl.BlockSpec((None, None, 4, ATTN_QBLK, 2 * ATTN_QBLK), lambda g, i, c: (gi, g, 0, 0, 0))]
    args = [qkv] * 7 + [bias]
    if has_prev:
        in_specs += [run_spec, run_spec]
        args += list(prev)
    if final:
        out_specs = pl.BlockSpec((rows * dil, LANES), lambda g, i, c: (i, g))
        out_shape = jax.ShapeDtypeStruct((s, hp * LANES), BF16)
    else:
        out_specs = [run_spec, run_spec]
        out_shape = [run_shape, run_shape]
    scratch = []
    if has_prev:
        scratch = [pltpu.VMEM((rows * dil, LANES), F32), pltpu.VMEM((rows * dil, LANES), F32)]
    return pl.pallas_call(
        functools.partial(_attn_kernel, rows=rows, dil=dil, cpb=cpb, has_prev=has_prev, final=final),
        grid=(hp, nblk, dil // cpb),
        in_specs=in_specs,
        out_specs=out_specs,
        out_shape=out_shape,
        scratch_shapes=scratch,
        compiler_params=_params("parallel", "arbitrary", "arbitrary"),
        name=f"dilated_attn_g{gi}",
    )(*args)


def _merge_kernel(of_ref, ob_ref, g_ref, oa_ref, gate_ref, x_ref, pa_ref, pb_ref, wo_ref, hw_ref,
                  nw_ref, xo_ref, ho_ref):
    d = x_ref.shape[1]
    b = jnp.dot(oa_ref[...], pb_ref[...], preferred_element_type=F32)
    a = None
    for h0 in range(0, HGRN_HEADS, MERGE_HEAD_GROUP):
        cols = []
        for h in range(h0, h0 + MERGE_HEAD_GROUP):
            lanes = slice((h % 2) * LANES, (h % 2 + 1) * LANES)
            o = of_ref[h // 2, :, lanes] + ob_ref[h // 2, :, lanes]
            o = o * lax.rsqrt(jnp.mean(o * o, axis=-1, keepdims=True) + NORM_EPS)
            g = g_ref[h // 2, :, lanes]
            cols.append((o * hw_ref[h] * (g * _sigmoid(g))).astype(BF16))
        part = jnp.dot(jnp.concatenate(cols, axis=1), pa_ref[h0 * LANES:(h0 + MERGE_HEAD_GROUP) * LANES, :],
                       preferred_element_type=F32)
        a = part if a is None else a + part
    merged = _sigmoid(gate_ref[:, :d]) * a.astype(BF16) + _sigmoid(gate_ref[:, d:]) * b.astype(BF16)
    xn = x_ref[...] + jnp.dot(merged, wo_ref[...], preferred_element_type=F32)
    xo_ref[...] = xn
    ho_ref[...] = _rms(xn, nw_ref[...]).astype(ho_ref.dtype)


def _merge(o_hgrn, proj_h, o_attn, gates, x, pa, pb, wo, hgrn_norm_w, norm_w, tm=256):
    s, d = x.shape
    pairs = HGRN_HEADS // 2
    hspec = pl.BlockSpec((pairs, tm, 2 * LANES), lambda i: (0, i, 0))

    def wspec(rows):
        return pl.BlockSpec((rows, d), lambda i: (0, 0))

    return pl.pallas_call(
        _merge_kernel,
        grid=(s // tm,),
        in_specs=[hspec, hspec,
                  pl.BlockSpec((pairs, tm, 2 * LANES), lambda i: (4, i, 0)),
                  pl.BlockSpec((tm, ATTN_OUT_WIDTH), lambda i: (i, 0)),
                  pl.BlockSpec((tm, 2 * d), lambda i: (i, 0)),
                  pl.BlockSpec((tm, d), lambda i: (i, 0)),
                  wspec(HGRN_WIDTH), wspec(ATTN_OUT_WIDTH), wspec(d),
                  pl.BlockSpec((HGRN_HEADS, 1, LANES), lambda i: (0, 0, 0)),
                  pl.BlockSpec((1, d), lambda i: (0, 0))],
        out_specs=[pl.BlockSpec((tm, d), lambda i: (i, 0)),
                   pl.BlockSpec((tm, d), lambda i: (i, 0))],
        out_shape=[jax.ShapeDtypeStruct((s, d), F32), jax.ShapeDtypeStruct((s, d), BF16)],
        compiler_params=_params("parallel"),
        name="merge_out_proj",
    )(o_hgrn[0], o_hgrn[1], proj_h, o_attn, gates, x, pa, pb, wo,
      hgrn_norm_w.reshape(HGRN_HEADS, 1, LANES), norm_w.reshape(1, d))


def _mlp_kernel(h_ref, x_ref, wu_hbm, wd_hbm, nw_ref, xo_ref, ho_ref, acc, wu_ring, wd_ring, sem):
    f = pl.program_id(1)
    tf = wu_ring.shape[2]

    def up_copy(k):
        cols = pl.ds(pl.multiple_of((k % pl.num_programs(1)) * tf, tf), tf)
        slot = k % WEIGHT_RING_SLOTS
        return pltpu.make_async_copy(wu_hbm.at[:, cols], wu_ring.at[slot], sem.at[0, slot])

    def down_copy(k):
        rows = pl.ds(pl.multiple_of((k % pl.num_programs(1)) * tf, tf), tf)
        slot = k % WEIGHT_RING_SLOTS
        return pltpu.make_async_copy(wd_hbm.at[rows, :], wd_ring.at[slot], sem.at[1, slot])

    slot = _ring_fetch(up_copy)
    _ring_fetch(down_copy)

    @pl.when(f == 0)
    def _():
        acc[...] = x_ref[...]

    u = jnp.maximum(jnp.dot(h_ref[...], wu_ring[slot], preferred_element_type=F32), 0.0)
    acc[...] += jnp.dot((u * u).astype(BF16), wd_ring[slot], preferred_element_type=F32)

    @pl.when(f == pl.num_programs(1) - 1)
    def _():
        xn = acc[...]
        xo_ref[...] = xn
        ho_ref[...] = _rms(xn, nw_ref[...]).astype(ho_ref.dtype)


def _mlp(h, x, wu, wd, norm_w, h_dtype, tm=512, tf=1024):
    s, d = x.shape
    ff = wu.shape[1]
    return pl.pallas_call(
        _mlp_kernel,
        grid=(s // tm, ff // tf),
        in_specs=[pl.BlockSpec((tm, d), lambda i, f: (i, 0)),
                  pl.BlockSpec((tm, d), lambda i, f: (i, 0)),
                  pl.BlockSpec(memory_space=pl.ANY),
                  pl.BlockSpec(memory_space=pl.ANY),
                  pl.BlockSpec((1, d), lambda i, f: (0, 0))],
        out_specs=[pl.BlockSpec((tm, d), lambda i, f: (i, 0)),
                   pl.BlockSpec((tm, d), lambda i, f: (i, 0))],
        out_shape=[jax.ShapeDtypeStruct((s, d), F32), jax.ShapeDtypeStruct((s, d), h_dtype)],
        scratch_shapes=[pltpu.VMEM((tm, d), F32),
                        pltpu.VMEM((WEIGHT_RING_SLOTS, d, tf), wu.dtype),
                        pltpu.VMEM((WEIGHT_RING_SLOTS, tf, d), wd.dtype),
                        pltpu.SemaphoreType.DMA((2, WEIGHT_RING_SLOTS))],
        compiler_params=pltpu.CompilerParams(dimension_semantics=("arbitrary", "arbitrary"),
                                             vmem_limit_bytes=MLP_VMEM_LIMIT),
        name="mlp_relu2",
    )(h, x, wu, wd, norm_w.reshape(1, d))


def _lower_bounds(logits):
    p = jax.nn.softmax(logits.astype(F32), axis=0)
    return jnp.cumsum(p, axis=0) - p[0:1]


def kernel(x, w_in, hgrn_lb_fwd, hgrn_lb_bwd, hgrn_norm_w, rel_bias_table, w_branch_hgrn,
           w_branch_attn, w_out, norm_mix_w, norm_mlp_w, w_up, w_down, final_norm_w):
    batch, s, d = x.shape
    depth = w_in.shape[0]
    assert batch == 1
    x = x.reshape(s, d)
    lb = jnp.stack([_lower_bounds(hgrn_lb_fwd), _lower_bounds(hgrn_lb_bwd)], axis=1)
    lb = lb.reshape(-1, 2, HGRN_HEADS // 2, 1, 2 * LANES)
    bias = _attn_bias(rel_bias_table)
    n_h = 5 * HGRN_WIDTH
    n_a = 3 * ATTN_WIDTH
    cast_rows = 256
    h = _norm(x, norm_mix_w[0])
    for l in range(depth):
        proj_h, wd_b = _proj(h, w_in, l, 0, n_h, F32, 2 * LANES, "proj_hgrn",
                             (w_down, 0, w_down.shape[1] // cast_rows))
        gates, wu_b = _proj(h, w_in, l, n_h + n_a, 2 * d, BF16, 0, "proj_gates",
                            (w_up, 1, w_up.shape[2] // cast_rows))
        o_hgrn = _hgrn(proj_h, lb[l])
        attn = None
        cast = []
        for gi, wt in enumerate((w_out, w_branch_hgrn, w_branch_attn)):
            qkv, wt_b = _proj_attn(h, w_in, l, n_h, gi, (wt, 0, wt.shape[1] // cast_rows))
            attn = _attn_group(qkv, bias, gi, attn)
            cast.append(wt_b)
        wo_b, pa_b, pb_b = cast
        x, h2 = _merge(o_hgrn, proj_h, attn, gates, x, pa_b, pb_b, wo_b, hgrn_norm_w[l],
                       norm_mlp_w[l])
        last = l == depth - 1
        x, h = _mlp(h2, x, wu_b, wd_b, final_norm_w if last else norm_mix_w[l + 1],
                    F32 if last else BF16)
    return h.reshape(batch, s, d)
```

```python
import functools

import jax
import jax.numpy as jnp
import numpy as np
from jax import lax
from jax.experimental import pallas as pl
from jax.experimental.pallas import tpu as pltpu

F32 = jnp.float32
BF16 = jnp.bfloat16

LANES = 128
HGRN_HEADS = 8
HGRN_WIDTH = HGRN_HEADS * LANES
CHUNK = 128
HGRN_BLOCK = 4 * CHUNK
N_LEVELS = 7
MXU_LEVEL_BELOW = 8
N_MXU_LEVELS = 3
MIN_FORGET = 1e-30
ATTN_GROUPS = ((128, 1), (512, 4), (2048, 16))
HEADS_PER_GROUP = 4
N_ATTN_HEADS = HEADS_PER_GROUP * len(ATTN_GROUPS)
ATTN_WIDTH = N_ATTN_HEADS * LANES
ATTN_OUT_WIDTH = HEADS_PER_GROUP * LANES
ATTN_QBLK = 128
ATTN_SIDE = 64
ATTN_TILES = 32
ATTN_QK_BATCH = 8
ATTN_RESIDUES = 8
ATTN_FOLD_ROWS = 256
REL_BUCKETS = 32
REL_MAX_DISTANCE = 1024
NORM_EPS = 1e-6
NEG_INF = -1e30
MERGE_HEAD_GROUP = 4
REGROUP_ROWS = 256
REGROUP_MATMUL_MIN_DIL = 8
WEIGHT_RING_SLOTS = 3
PROJ_ROW_TILES = 4
VMEM_LIMIT = 56 * 1024 * 1024
MLP_VMEM_LIMIT = 60 * 1024 * 1024

_NT = (((1,), (1,)), ((), ()))
_TN = (((0,), (0,)), ((), ()))


def _params(*sem):
    return pltpu.CompilerParams(dimension_semantics=sem, vmem_limit_bytes=VMEM_LIMIT)


def _rms(x, w):
    return x * lax.rsqrt(jnp.mean(x * x, axis=-1, keepdims=True) + NORM_EPS) * w


def _sigmoid(x):
    return 0.5 * jnp.tanh(0.5 * x) + 0.5


def _norm_kernel(x_ref, w_ref, o_ref):
    o_ref[...] = _rms(x_ref[...], w_ref[...]).astype(o_ref.dtype)


def _norm(x, w, tm=512):
    s, d = x.shape
    return pl.pallas_call(
        _norm_kernel,
        grid=(s // tm,),
        in_specs=[pl.BlockSpec((tm, d), lambda i: (i, 0)),
                  pl.BlockSpec((1, d), lambda i: (0, 0))],
        out_specs=pl.BlockSpec((tm, d), lambda i: (i, 0)),
        out_shape=jax.ShapeDtypeStruct((s, d), BF16),
        compiler_params=_params("parallel"),
        name="rmsnorm_in",
    )(x, w.reshape(1, d))


def _cast_rider(cast_in, cast_out):
    cast_out[...] = cast_in[...].astype(cast_out.dtype)


def _rider_specs(rider, layer, n_j):
    w, axis, n_chunks = rider
    r, c = w.shape[1:]

    def chunk(i, j):
        return jnp.minimum(i * n_j + j, n_chunks - 1)

    if axis == 0:
        blk = (r // n_chunks, c)
        in_spec = pl.BlockSpec((None,) + blk, lambda i, j: (layer, chunk(i, j), 0))
        out_spec = pl.BlockSpec(blk, lambda i, j: (chunk(i, j), 0))
    else:
        blk = (r, c // n_chunks)
        in_spec = pl.BlockSpec((None,) + blk, lambda i, j: (layer, 0, chunk(i, j)))
        out_spec = pl.BlockSpec(blk, lambda i, j: (0, chunk(i, j)))
    return in_spec, out_spec, jax.ShapeDtypeStruct((r, c), BF16)


def _ring_fetch(tile_copy):
    n_j = pl.num_programs(1)
    step = pl.program_id(0) * n_j + pl.program_id(1)
    total = pl.num_programs(0) * n_j

    @pl.when(step == 0)
    def _():
        for k in range(WEIGHT_RING_SLOTS - 1):
            tile_copy(k).start()

    @pl.when(step + WEIGHT_RING_SLOTS - 1 < total)
    def _():
        tile_copy(step + WEIGHT_RING_SLOTS - 1).start()

    tile_copy(step).wait()
    return step % WEIGHT_RING_SLOTS


def _proj_kernel(x_ref, w_hbm, cast_in, o_ref, cast_out, w_ring, sem, *, slabs, layer, col0):
    _cast_rider(cast_in, cast_out)
    tn = w_ring.shape[2]

    def tile_copy(k):
        cols = pl.ds(pl.multiple_of(col0 + (k % pl.num_programs(1)) * tn, tn), tn)
        slot = k % WEIGHT_RING_SLOTS
        return pltpu.make_async_copy(w_hbm.at[layer, :, cols], w_ring.at[slot], sem.at[slot])

    slot = _ring_fetch(tile_copy)
    acc = jnp.dot(x_ref[...], w_ring[slot].astype(BF16), preferred_element_type=F32)
    if slabs:
        width = o_ref.shape[-1]
        for j in range(slabs):
            o_ref[j] = acc[:, j * width:(j + 1) * width].astype(o_ref.dtype)
    else:
        o_ref[...] = acc.astype(o_ref.dtype)


def _proj(h, w, layer, col0, n, out_dtype, slab_width, name, rider, tn=512):
    s, d = h.shape
    tm = s // PROJ_ROW_TILES
    n_j = n // tn
    assert rider[2] <= PROJ_ROW_TILES * n_j
    if slab_width:
        slabs = tn // slab_width
        out_shape = jax.ShapeDtypeStruct((n // slab_width, s, slab_width), out_dtype)
        out_spec = pl.BlockSpec((slabs, tm, slab_width), lambda i, j: (j, i, 0))
    else:
        slabs = 0
        out_shape = jax.ShapeDtypeStruct((s, n), out_dtype)
        out_spec = pl.BlockSpec((tm, tn), lambda i, j: (i, j))
    r_in, r_out, r_shape = _rider_specs(rider, layer, n_j)
    assert PROJ_ROW_TILES * n_j >= WEIGHT_RING_SLOTS
    return pl.pallas_call(
        functools.partial(_proj_kernel, slabs=slabs, layer=layer, col0=col0),
        grid=(s // tm, n_j),
        in_specs=[pl.BlockSpec((tm, d), lambda i, j: (i, 0)),
                  pl.BlockSpec(memory_space=pl.ANY), r_in],
        out_specs=[out_spec, r_out],
        out_shape=[out_shape, r_shape],
        scratch_shapes=[pltpu.VMEM((WEIGHT_RING_SLOTS, d, tn), w.dtype),
                        pltpu.SemaphoreType.DMA((WEIGHT_RING_SLOTS,))],
        compiler_params=_params("arbitrary", "arbitrary"),
        name=name,
    )(h, w, rider[0])


def _regroup_matrix(dil):
    r = REGROUP_ROWS // dil
    out_row = np.arange(REGROUP_ROWS)
    src = (out_row % r) * dil + out_row // r
    return jnp.asarray(src[:, None] == np.arange(REGROUP_ROWS)[None, :], BF16)


def _proj_attn_kernel(x_ref, w_hbm, cast_in, perm_ref, o_ref, cast_out, scr, w_ring, sem, *,
                      dil, layer, col0, col_stride):
    _cast_rider(cast_in, cast_out)
    tn = w_ring.shape[2]

    def tile_copy(k):
        cols = pl.ds(pl.multiple_of(col0 + (k % pl.num_programs(1)) * col_stride, tn), tn)
        slot = k % WEIGHT_RING_SLOTS
        return pltpu.make_async_copy(w_hbm.at[layer, :, cols], w_ring.at[slot], sem.at[slot])

    slot = _ring_fetch(tile_copy)
    acc = jnp.dot(x_ref[...], w_ring[slot].astype(BF16), preferred_element_type=F32)
    if dil == 1:
        for hh in range(HEADS_PER_GROUP):
            o_ref[hh, 0] = acc[:, hh * LANES:(hh + 1) * LANES].astype(o_ref.dtype)
    elif dil < REGROUP_MATMUL_MIN_DIL:
        rows = scr.shape[1] // dil
        for hh in range(HEADS_PER_GROUP):
            scr[hh] = acc[:, hh * LANES:(hh + 1) * LANES]
        for hh in range(HEADS_PER_GROUP):
            for c in range(dil):
                o_ref[hh, c] = scr[hh, pl.ds(c, rows, stride=dil), :].astype(o_ref.dtype)
    else:
        r = REGROUP_ROWS // dil
        rounded = acc.astype(BF16)
        for b in range(acc.shape[0] // REGROUP_ROWS):
            blk = jnp.dot(perm_ref[...], rounded[b * REGROUP_ROWS:(b + 1) * REGROUP_ROWS],
                          preferred_element_type=F32).astype(o_ref.dtype)
            for hh in range(HEADS_PER_GROUP):
                for c in range(dil):
                    o_ref[hh, c, b * r:(b + 1) * r, :] = blk[c * r:(c + 1) * r,
                                                             hh * LANES:(hh + 1) * LANES]


def _proj_attn(h, w, layer, col0, gi, rider):
    s, d = h.shape
    tm = s // PROJ_ROW_TILES
    assert rider[2] <= PROJ_ROW_TILES * 3
    dil = ATTN_GROUPS[gi][1]
    tn = ATTN_OUT_WIDTH
    n_groups = len(ATTN_GROUPS)
    r_in, r_out, r_shape = _rider_specs(rider, layer, 3)
    perm = _regroup_matrix(dil)
    return pl.pallas_call(
        functools.partial(_proj_attn_kernel, dil=dil, layer=layer,
                          col0=col0 + gi * tn, col_stride=n_groups * tn),
        grid=(s // tm, 3),
        in_specs=[pl.BlockSpec((tm, d), lambda i, j: (i, 0)),
                  pl.BlockSpec(memory_space=pl.ANY), r_in,
                  pl.BlockSpec(perm.shape, lambda i, j: (0, 0))],
        out_specs=[pl.BlockSpec((HEADS_PER_GROUP, dil, tm // dil, LANES), lambda i, j: (j, 0, i, 0)),
                   r_out],
        out_shape=[jax.ShapeDtypeStruct((3 * HEADS_PER_GROUP, dil, s // dil, LANES), BF16), r_shape],
        scratch_shapes=[pltpu.VMEM((HEADS_PER_GROUP, tm, LANES), F32),
                        pltpu.VMEM((WEIGHT_RING_SLOTS, d, tn), w.dtype),
                        pltpu.SemaphoreType.DMA((WEIGHT_RING_SLOTS,))],
        compiler_params=_params("arbitrary", "arbitrary"),
        name=f"proj_attn_g{gi}",
    )(h, w, rider[0], perm)


def _hgrn_constants():
    c = CHUNK
    t = np.arange(c)[:, None]
    s = np.arange(c)[None, :]
    mats = [s <= t]
    masks = []
    for lvl in range(N_LEVELS):
        m = c >> (lvl + 1)
        p = (t // (2 * m)) * (2 * m) + m - 1
        if m < MXU_LEVEL_BELOW:
            mats.append(((t > p) & (s > p) & (s <= t)) | ((t <= p) & (s > t) & (s <= p)))
        masks.append((t // (2 * m) == s // (2 * m)) & (t % (2 * m) >= m) & (s % (2 * m) < m))
    w = np.stack(mats).astype(np.float32)
    k = np.stack(masks).astype(np.float32)
    w = np.stack([w, w[:, ::-1, ::-1]]).reshape(2, len(mats) * c, c)
    w = np.concatenate([w, w], axis=2)
    k = np.stack([k, k[:, ::-1, ::-1]])
    k = np.concatenate([k, k], axis=3)
    return jnp.asarray(w, BF16), jnp.asarray(k, BF16)


def _block_diag(x):
    zero = jnp.zeros((x.shape[0], LANES), x.dtype)
    return jnp.concatenate([jnp.concatenate([x[:, :LANES], zero], axis=1),
                            jnp.concatenate([zero, x[:, LANES:]], axis=1)], axis=0)


def _block_diag_t(x):
    zero = jnp.zeros((LANES, x.shape[1]), x.dtype)
    return jnp.concatenate([jnp.concatenate([x[:LANES], zero], axis=1),
                            jnp.concatenate([zero, x[LANES:]], axis=1)], axis=0)


def _hgrn_chunk(q, z, v, lb, wmat, m_ref, st, backward):
    half = 0.5 * (1.0 - lb)
    ct = half * jnp.tanh(0.5 * z)
    kk = half - ct
    logf = jnp.log2(jnp.maximum((lb + half) + ct, MIN_FORGET))
    hi = logf.astype(BF16)
    lo = (logf - hi.astype(F32)).astype(BF16)
    dec = jnp.dot(wmat, jnp.concatenate([hi, lo], axis=0), preferred_element_type=F32)
    b = dec[0:CHUNK]
    total = b[0:1] if backward else b[CHUNK - 1:CHUNK]
    q_in = (q * jnp.exp2(b)).astype(BF16)
    k_out = (kk * jnp.exp2(total - b)).astype(BF16)
    vb = v.astype(BF16)
    qb = q.astype(BF16)
    kb = kk.astype(BF16)
    o = jnp.dot(q_in, _block_diag_t(st.astype(BF16).T), preferred_element_type=F32)
    qk = q * kk
    diag = jnp.concatenate(
        [jnp.broadcast_to(jnp.sum(qk[:, :LANES], axis=-1, keepdims=True), (CHUNK, LANES)),
         jnp.broadcast_to(jnp.sum(qk[:, LANES:], axis=-1, keepdims=True), (CHUNK, LANES))], axis=1)
    o = o + diag * v
    a = None
    for lvl in range(N_LEVELS):
        m = CHUNK >> (lvl + 1)
        if m >= MXU_LEVEL_BELOW:
            parts = []
            for j in range(CHUNK // (2 * m)):
                lo_rows = b[2 * m * j:2 * m * j + m]
                hi_rows = b[2 * m * j + m:2 * m * (j + 1)]
                if backward:
                    piv = b[2 * m * j + m:2 * m * j + m + 1]
                    parts += [lo_rows - piv, piv - hi_rows]
                else:
                    piv = b[2 * m * j + m - 1:2 * m * j + m]
                    parts += [piv - lo_rows, hi_rows - piv]
            nl = jnp.concatenate(parts, axis=0)
        else:
            i = 1 + lvl - (N_LEVELS - N_MXU_LEVELS)
            nl = dec[i * CHUNK:(i + 1) * CHUNK]
        el = jnp.exp2(nl).astype(BF16)
        sc = jnp.dot(qb * el, _block_diag_t((kb * el).T), preferred_element_type=F32)
        sc = sc.astype(BF16) * m_ref[lvl]
        a = sc if a is None else a + sc
    o = o + jnp.dot(a, _block_diag(vb), preferred_element_type=F32)
    v_rows = jnp.concatenate([vb[:, :LANES], vb[:, LANES:]], axis=0)
    st_new = st * jnp.exp2(total) + lax.dot_general(v_rows, _block_diag(k_out), _TN,
                                                    preferred_element_type=F32)
    return o, st_new


def _hgrn_kernel(qf_ref, zf_ref, vf_ref, qb_ref, zb_ref, vb_ref, lb_ref, w_ref, m_ref,
                 of_ref, ob_ref, stf_ref, stb_ref):
    @pl.when(pl.program_id(0) == 0)
    def _():
        stf_ref[...] = jnp.zeros_like(stf_ref)
        stb_ref[...] = jnp.zeros_like(stb_ref)

    n_sub = qf_ref.shape[1] // CHUNK

    def sub_chunk(t, carry):
        rf = pl.ds(pl.multiple_of(t * CHUNK, CHUNK), CHUNK)
        rb = pl.ds(pl.multiple_of((n_sub - 1 - t) * CHUNK, CHUNK), CHUNK)
        for p in range(HGRN_HEADS // 2):
            o, st = _hgrn_chunk(qf_ref[p, rf, :], zf_ref[p, rf, :], vf_ref[p, rf, :], lb_ref[0, p],
                                w_ref[0], m_ref.at[0], stf_ref[p], False)
            of_ref[p, rf, :] = o
            stf_ref[p] = st
            o, st = _hgrn_chunk(qb_ref[p, rb, :], zb_ref[p, rb, :], vb_ref[p, rb, :], lb_ref[1, p],
                                w_ref[1], m_ref.at[1], stb_ref[p], True)
            ob_ref[p, rb, :] = o
            stb_ref[p] = st
        return carry

    lax.fori_loop(0, n_sub, sub_chunk, 0)


def _hgrn(proj_h, lb):
    s = proj_h.shape[1]
    n = s // HGRN_BLOCK
    pairs = HGRN_HEADS // 2
    wmat, masks = _hgrn_constants()
    hb = (pairs, HGRN_BLOCK, 2 * LANES)
    fwd = lambda slab: pl.BlockSpec(hb, lambda c: (slab, c, 0))
    bwd = lambda slab: pl.BlockSpec(hb, lambda c: (slab, n - 1 - c, 0))
    const = lambda a: pl.BlockSpec(a.shape, lambda c: (0,) * a.ndim)
    out_shape = jax.ShapeDtypeStruct((pairs, s, 2 * LANES), F32)
    state = pltpu.VMEM((pairs, LANES, 2 * LANES), F32)
    return pl.pallas_call(
        _hgrn_kernel,
        grid=(n,),
        in_specs=[fwd(0), fwd(1), fwd(3), bwd(0), bwd(2), bwd(3), const(lb), const(wmat), const(masks)],
        out_specs=[fwd(0), bwd(0)],
        out_shape=[out_shape, out_shape],
        scratch_shapes=[state, state],
        compiler_params=_params("arbitrary"),
        name="hgrn2_scan",
    )(proj_h, proj_h, proj_h, proj_h, proj_h, proj_h, lb, wmat, masks)


def _t5_bucket(rel):
    half = REL_BUCKETS // 2
    ret = (rel > 0).astype(np.int32) * half
    n = np.abs(rel)
    max_exact = half // 2
    large = max_exact + (np.log(np.maximum(n, 1) / max_exact)
                         / np.log(REL_MAX_DISTANCE / max_exact)
                         * (half - max_exact)).astype(np.int32)
    large = np.minimum(large, half - 1)
    return (ret + np.where(n < max_exact, n, large)).astype(np.int32)


def _attn_bias(rel_bias_table):
    span = 3 * ATTN_QBLK - 1
    rel = np.arange(span) - (ATTN_QBLK - 1) - ATTN_SIDE
    col = np.arange(2 * ATTN_QBLK)[None, :]
    band = np.abs(col - ATTN_SIDE - np.arange(ATTN_QBLK)[:, None]) <= ATTN_SIDE
    after_start, before_end = col >= ATTN_SIDE, col < 2 * ATTN_QBLK - ATTN_SIDE
    keep = [band, band & after_start, band & before_end, band & after_start & before_end]
    out = []
    for gi, (_, dil) in enumerate(ATTN_GROUPS):
        tab = rel_bias_table[:, gi * HEADS_PER_GROUP:(gi + 1) * HEADS_PER_GROUP].astype(F32)
        onehot = jnp.asarray(_t5_bucket(rel * dil)[:, None] == np.arange(REL_BUCKETS)[None, :], F32)
        vec = jnp.einsum("rb,bh->hr", onehot, tab, precision=lax.Precision.HIGHEST)
        period = span + 2
        vec = jnp.pad(vec, ((0, 0), (0, period - span)))
        flat = jnp.tile(vec, (1, ATTN_QBLK))[:, :ATTN_QBLK * (span + 1)]
        toep = flat.reshape(HEADS_PER_GROUP, ATTN_QBLK, span + 1)[:, :, ATTN_QBLK - 1:3 * ATTN_QBLK - 1]
        out.append(jnp.stack([jnp.where(jnp.asarray(k)[None], toep, NEG_INF) for k in keep], axis=1))
    return jnp.stack(out)


def _attn_kernel(q_ref, kp_ref, km_ref, kn_ref, vp_ref, vm_ref, vn_ref, bias_ref, *rest,
                 rows, dil, cpb, has_prev, final):
    rest = list(rest)
    op_ref, lp_ref = (rest.pop(0), rest.pop(0)) if has_prev else (None, None)
    o_ref = rest.pop(0)
    lse_ref = None if final else rest.pop(0)
    o_s, l_s = rest if rest else (o_ref, lse_ref)
    c = pl.program_id(2)
    i = pl.program_id(1)
    scale = LANES ** -0.5
    n_sub = rows // ATTN_QBLK

    def window(prev_ref, main_ref, next_ref, cc, j):
        lo, hi = j * ATTN_QBLK - ATTN_SIDE, (j + 1) * ATTN_QBLK + ATTN_SIDE
        parts = []
        if lo < 0:
            parts.append(prev_ref[cc])
        parts.append(main_ref[cc, max(lo, 0):min(hi, rows)])
        if hi > rows:
            parts.append(next_ref[cc])
        return parts[0] if len(parts) == 1 else jnp.concatenate(parts, axis=0)

    tiles = [(cc, j) for cc in range(cpb) for j in range(n_sub)]
    for t0 in range(0, len(tiles), ATTN_QK_BATCH):
        batch = tiles[t0:t0 + ATTN_QK_BATCH]
        scores = []
        for cc, j in batch:
            r0 = j * ATTN_QBLK
            variant = 0
            if j == 0:
                variant = variant + (i == 0).astype(jnp.int32)
            if j == n_sub - 1:
                variant = variant + 2 * (i == pl.num_programs(1) - 1).astype(jnp.int32)
            scores.append(lax.dot_general(q_ref[cc, r0:r0 + ATTN_QBLK],
                                          window(kp_ref, km_ref, kn_ref, cc, j), _NT,
                                          preferred_element_type=F32) * scale + bias_ref[variant])
        for (cc, j), s in zip(batch, scores):
            r0 = j * ATTN_QBLK
            m = jnp.max(s, axis=-1, keepdims=True)
            p = jnp.exp(s - m)
            l = jnp.sum(p, axis=-1, keepdims=True)
            o = jnp.dot(p.astype(BF16), window(vp_ref, vm_ref, vn_ref, cc, j),
                        preferred_element_type=F32) / l
            lse = jnp.broadcast_to(m + jnp.log(l), (ATTN_QBLK, LANES))
            if dil == 1:
                o_s[r0:r0 + ATTN_QBLK] = o
                l_s[r0:r0 + ATTN_QBLK] = lse
            else:
                start = r0 * dil + c * cpb + cc
                o_s[pl.ds(start, ATTN_QBLK, stride=dil), :] = o
                l_s[pl.ds(start, ATTN_QBLK, stride=dil), :] = lse

    if has_prev:
        @pl.when(c == dil // cpb - 1)
        def _():
            def fold(t, carry):
                r = pl.ds(pl.multiple_of(t * ATTN_FOLD_ROWS, ATTN_FOLD_ROWS), ATTN_FOLD_ROWS)
                lp, lc = lp_ref[r, :], l_s[r, :]
                m = jnp.maximum(lp, lc)
                wp, wc = jnp.exp(lp - m), jnp.exp(lc - m)
                den = wp + wc
                o = (wp * op_ref[r, :] + wc * o_s[r, :]) / den
                o_ref[r, :] = o.astype(o_ref.dtype)
                if not final:
                    lse_ref[r, :] = m + jnp.log(den)
                return carry

            lax.fori_loop(0, rows * dil // ATTN_FOLD_ROWS, fold, 0)


def _attn_group(qkv, bias, gi, prev):
    dil, length = qkv.shape[1], qkv.shape[2]
    s = dil * length
    cpb = min(ATTN_RESIDUES, dil)
    rows = min(ATTN_TILES * ATTN_QBLK // cpb, length)
    nblk = length // rows
    sub = rows // ATTN_SIDE
    last = length // ATTN_SIDE - 1
    hp = HEADS_PER_GROUP
    final = gi == len(ATTN_GROUPS) - 1
    has_prev = prev is not None
    assert has_prev or dil == 1

    def main(kind):
        return pl.BlockSpec((None, cpb, rows, LANES), lambda g, i, c: (kind * hp + g, c, i, 0))

    def prev_blk(kind):
        return pl.BlockSpec((None, cpb, ATTN_SIDE, LANES),
                            lambda g, i, c: (kind * hp + g, c, jnp.maximum(i * sub - 1, 0), 0))

    def next_blk(kind):
        return pl.BlockSpec((None, cpb, ATTN_SIDE, LANES),
                            lambda g, i, c: (kind * hp + g, c, jnp.minimum((i + 1) * sub, last), 0))

    run_spec = pl.BlockSpec((None, rows * dil, LANES), lambda g, i, c: (g, i, 0))
    run_shape = jax.ShapeDtypeStruct((hp, s, LANES), F32)
    in_specs = [main(0), prev_blk(1), main(1), next_blk(1), prev_blk(2), main(2), next_blk(2),
                pl.BlockSpec((None, None, 4, ATTN_QBLK, 2 * ATTN_QBLK), lambda g, i, c: (gi, g, 0, 0, 0))]
    args = [qkv] * 7 + [bias]
    if has_prev:
        in_specs += [run_spec, run_spec]
        args += list(prev)
    if final:
        out_specs = pl.BlockSpec((rows * dil, LANES), lambda g, i, c: (i, g))
        out_shape = jax.ShapeDtypeStruct((s, hp * LANES), BF16)
    else:
        out_specs = [run_spec, run_spec]
        out_shape = [run_shape, run_shape]
    scratch = []
    if has_prev:
        scratch = [pltpu.VMEM((rows * dil, LANES), F32), pltpu.VMEM((rows * dil, LANES), F32)]
    return pl.pallas_call(
        functools.partial(_attn_kernel, rows=rows, dil=dil, cpb=cpb, has_prev=has_prev, final=final),
        grid=(hp, nblk, dil // cpb),
        in_specs=in_specs,
        out_specs=out_specs,
        out_shape=out_shape,
        scratch_shapes=scratch,
        compiler_params=_params("parallel", "arbitrary", "arbitrary"),
        name=f"dilated_attn_g{gi}",
    )(*args)


def _merge_kernel(of_ref, ob_ref, g_ref, oa_ref, gate_ref, x_ref, pa_ref, pb_ref, wo_ref, hw_ref,
                  nw_ref, xo_ref, ho_ref):
    d = x_ref.shape[1]
    b = jnp.dot(oa_ref[...], pb_ref[...], preferred_element_type=F32)
    a = None
    for h0 in range(0, HGRN_HEADS, MERGE_HEAD_GROUP):
        cols = []
        for h in range(h0, h0 + MERGE_HEAD_GROUP):
            lanes = slice((h % 2) * LANES, (h % 2 + 1) * LANES)
            o = of_ref[h // 2, :, lanes] + ob_ref[h // 2, :, lanes]
            o = o * lax.rsqrt(jnp.mean(o * o, axis=-1, keepdims=True) + NORM_EPS)
            g = g_ref[h // 2, :, lanes]
            cols.append((o * hw_ref[h] * (g * _sigmoid(g))).astype(BF16))
        part = jnp.dot(jnp.concatenate(cols, axis=1), pa_ref[h0 * LANES:(h0 + MERGE_HEAD_GROUP) * LANES, :],
                       preferred_element_type=F32)
        a = part if a is None else a + part
    merged = _sigmoid(gate_ref[:, :d]) * a.astype(BF16) + _sigmoid(gate_ref[:, d:]) * b.astype(BF16)
    xn = x_ref[...] + jnp.dot(merged, wo_ref[...], preferred_element_type=F32)
    xo_ref[...] = xn
    ho_ref[...] = _rms(xn, nw_ref[...]).astype(ho_ref.dtype)


def _merge(o_hgrn, proj_h, o_attn, gates, x, pa, pb, wo, hgrn_norm_w, norm_w, tm=256):
    s, d = x.shape
    pairs = HGRN_HEADS // 2
    hspec = pl.BlockSpec((pairs, tm, 2 * LANES), lambda i: (0, i, 0))

    def wspec(rows):
        return pl.BlockSpec((rows, d), lambda i: (0, 0))

    return pl.pallas_call(
        _merge_kernel,
        grid=(s // tm,),
        in_specs=[hspec, hspec,
                  pl.BlockSpec((pairs, tm, 2 * LANES), lambda i: (4, i, 0)),
                  pl.BlockSpec((tm, ATTN_OUT_WIDTH), lambda i: (i, 0)),
                  pl.BlockSpec((tm, 2 * d), lambda i: (i, 0)),
                  pl.BlockSpec((tm, d), lambda i: (i, 0)),
                  wspec(HGRN_WIDTH), wspec(ATTN_OUT_WIDTH), wspec(d),
                  pl.BlockSpec((HGRN_HEADS, 1, LANES), lambda i: (0, 0, 0)),
                  pl.BlockSpec((1, d), lambda i: (0, 0))],
        out_specs=[pl.BlockSpec((tm, d), lambda i: (i, 0)),
                   pl.BlockSpec((tm, d), lambda i: (i, 0))],
        out_shape=[jax.ShapeDtypeStruct((s, d), F32), jax.ShapeDtypeStruct((s, d), BF16)],
        compiler_params=_params("parallel"),
        name="merge_out_proj",
    )(o_hgrn[0], o_hgrn[1], proj_h, o_attn, gates, x, pa, pb, wo,
      hgrn_norm_w.reshape(HGRN_HEADS, 1, LANES), norm_w.reshape(1, d))


def _mlp_kernel(h_ref, x_ref, wu_hbm, wd_hbm, nw_ref, xo_ref, ho_ref, acc, wu_ring, wd_ring, sem):
    f = pl.program_id(1)
    tf = wu_ring.shape[2]

    def up_copy(k):
        cols = pl.ds(pl.multiple_of((k % pl.num_programs(1)) * tf, tf), tf)
        slot = k % WEIGHT_RING_SLOTS
        return pltpu.make_async_copy(wu_hbm.at[:, cols], wu_ring.at[slot], sem.at[0, slot])

    def down_copy(k):
        rows = pl.ds(pl.multiple_of((k % pl.num_programs(1)) * tf, tf), tf)
        slot = k % WEIGHT_RING_SLOTS
        return pltpu.make_async_copy(wd_hbm.at[rows, :], wd_ring.at[slot], sem.at[1, slot])

    slot = _ring_fetch(up_copy)
    _ring_fetch(down_copy)

    @pl.when(f == 0)
    def _():
        acc[...] = x_ref[...]

    u = jnp.maximum(jnp.dot(h_ref[...], wu_ring[slot], preferred_element_type=F32), 0.0)
    acc[...] += jnp.dot((u * u).astype(BF16), wd_ring[slot], preferred_element_type=F32)

    @pl.when(f == pl.num_programs(1) - 1)
    def _():
        xn = acc[...]
        xo_ref[...] = xn
        ho_ref[...] = _rms(xn, nw_ref[...]).astype(ho_ref.dtype)


def _mlp(h, x, wu, wd, norm_w, h_dtype, tm=512, tf=1024):
    s, d = x.shape
    ff = wu.shape[1]
    return pl.pallas_call(
        _mlp_kernel,
        grid=(s // tm, ff // tf),
        in_specs=[pl.BlockSpec((tm, d), lambda i, f: (i, 0)),
                  pl.BlockSpec((tm, d), lambda i, f: (i, 0)),
                  pl.BlockSpec(memory_space=pl.ANY),
                  pl.BlockSpec(memory_space=pl.ANY),
                  pl.BlockSpec((1, d), lambda i, f: (0, 0))],
        out_specs=[pl.BlockSpec((tm, d), lambda i, f: (i, 0)),
                   pl.BlockSpec((tm, d), lambda i, f: (i, 0))],
        out_shape=[jax.ShapeDtypeStruct((s, d), F32), jax.ShapeDtypeStruct((s, d), h_dtype)],
        scratch_shapes=[pltpu.VMEM((tm, d), F32),
                        pltpu.VMEM((WEIGHT_RING_SLOTS, d, tf), wu.dtype),
                        pltpu.VMEM((WEIGHT_RING_SLOTS, tf, d), wd.dtype),
                        pltpu.SemaphoreType.DMA((2, WEIGHT_RING_SLOTS))],
        compiler_params=pltpu.CompilerParams(dimension_semantics=("arbitrary", "arbitrary"),
                                             vmem_limit_bytes=MLP_VMEM_LIMIT),
        name="mlp_relu2",
    )(h, x, wu, wd, norm_w.reshape(1, d))


def _lower_bounds(logits):
    p = jax.nn.softmax(logits.astype(F32), axis=0)
    return jnp.cumsum(p, axis=0) - p[0:1]


def kernel(x, w_in, hgrn_lb_fwd, hgrn_lb_bwd, hgrn_norm_w, rel_bias_table, w_branch_hgrn,
           w_branch_attn, w_out, norm_mix_w, norm_mlp_w, w_up, w_down, final_norm_w):
    batch, s, d = x.shape
    depth = w_in.shape[0]
    assert batch == 1
    x = x.reshape(s, d)
    lb = jnp.stack([_lower_bounds(hgrn_lb_fwd), _lower_bounds(hgrn_lb_bwd)], axis=1)
    lb = lb.reshape(-1, 2, HGRN_HEADS // 2, 1, 2 * LANES)
    bias = _attn_bias(rel_bias_table)
    n_h = 5 * HGRN_WIDTH
    n_a = 3 * ATTN_WIDTH
    cast_rows = 256
    h = _norm(x, norm_mix_w[0])
    for l in range(depth):
        proj_h, wd_b = _proj(h, w_in, l, 0, n_h, F32, 2 * LANES, "proj_hgrn",
                             (w_down, 0, w_down.shape[1] // cast_rows))
        gates, wu_b = _proj(h, w_in, l, n_h + n_a, 2 * d, BF16, 0, "proj_gates",
                            (w_up, 1, w_up.shape[2] // cast_rows))
        o_hgrn = _hgrn(proj_h, lb[l])
        attn = None
        cast = []
        for gi, wt in enumerate((w_out, w_branch_hgrn, w_branch_attn)):
            qkv, wt_b = _proj_attn(h, w_in, l, n_h, gi, (wt, 0, wt.shape[1] // cast_rows))
            attn = _attn_group(qkv, bias, gi, attn)
            cast.append(wt_b)
        wo_b, pa_b, pb_b = cast
        x, h2 = _merge(o_hgrn, proj_h, attn, gates, x, pa_b, pb_b, wo_b, hgrn_norm_w[l],
                       norm_mlp_w[l])
        last = l == depth - 1
        x, h = _mlp(h2, x, wu_b, wd_b, final_norm_w if last else norm_mix_w[l + 1],
                    F32 if last else BF16)
    return h.reshape(batch, s, d)
```

```python
import functools

import jax
import jax.numpy as jnp
import numpy as np
from jax import lax
from jax.experimental import pallas as pl
from jax.experimental.pallas import tpu as pltpu

F32 = jnp.float32
BF16 = jnp.bfloat16

LANES = 128
HGRN_HEADS = 8
HGRN_WIDTH = HGRN_HEADS * LANES
CHUNK = 128
HGRN_BLOCK = 4 * CHUNK
N_LEVELS = 7
MXU_LEVEL_BELOW = 8
N_MXU_LEVELS = 3
MIN_FORGET = 1e-30
ATTN_GROUPS = ((128, 1), (512, 4), (2048, 16))
HEADS_PER_GROUP = 4
N_ATTN_HEADS = HEADS_PER_GROUP * len(ATTN_GROUPS)
ATTN_WIDTH = N_ATTN_HEADS * LANES
ATTN_OUT_WIDTH = HEADS_PER_GROUP * LANES
ATTN_QBLK = 128
ATTN_SIDE = 64
ATTN_TILES = 32
ATTN_QK_BATCH = 8
ATTN_RESIDUES = 8
ATTN_FOLD_ROWS = 256
REL_BUCKETS = 32
REL_MAX_DISTANCE = 1024
NORM_EPS = 1e-6
NEG_INF = -1e30
MERGE_HEAD_GROUP = 4
REGROUP_ROWS = 256
REGROUP_MATMUL_MIN_DIL = 8
WEIGHT_RING_SLOTS = 3
PROJ_ROW_TILES = 4
VMEM_LIMIT = 56 * 1024 * 1024
MLP_VMEM_LIMIT = 60 * 1024 * 1024

_NT = (((1,), (1,)), ((), ()))
_TN = (((0,), (0,)), ((), ()))


def _params(*sem):
    return pltpu.CompilerParams(dimension_semantics=sem, vmem_limit_bytes=VMEM_LIMIT)


def _rms(x, w):
    return x * lax.rsqrt(jnp.mean(x * x, axis=-1, keepdims=True) + NORM_EPS) * w


def _sigmoid(x):
    return 0.5 * jnp.tanh(0.5 * x) + 0.5


def _norm_kernel(x_ref, w_ref, o_ref):
    o_ref[...] = _rms(x_ref[...], w_ref[...]).astype(o_ref.dtype)


def _norm(x, w, tm=512):
    s, d = x.shape
    return pl.pallas_call(
        _norm_kernel,
        grid=(s // tm,),
        in_specs=[pl.BlockSpec((tm, d), lambda i: (i, 0)),
                  pl.BlockSpec((1, d), lambda i: (0, 0))],
        out_specs=pl.BlockSpec((tm, d), lambda i: (i, 0)),
        out_shape=jax.ShapeDtypeStruct((s, d), BF16),
        compiler_params=_params("parallel"),
        name="rmsnorm_in",
    )(x, w.reshape(1, d))


def _cast_rider(cast_in, cast_out):
    cast_out[...] = cast_in[...].astype(cast_out.dtype)


def _rider_specs(rider, layer, n_j):
    w, axis, n_chunks = rider
    r, c = w.shape[1:]

    def chunk(i, j):
        return jnp.minimum(i * n_j + j, n_chunks - 1)

    if axis == 0:
        blk = (r // n_chunks, c)
        in_spec = pl.BlockSpec((None,) + blk, lambda i, j: (layer, chunk(i, j), 0))
        out_spec = pl.BlockSpec(blk, lambda i, j: (chunk(i, j), 0))
    else:
        blk = (r, c // n_chunks)
        in_spec = pl.BlockSpec((None,) + blk, lambda i, j: (layer, 0, chunk(i, j)))
        out_spec = pl.BlockSpec(blk, lambda i, j: (0, chunk(i, j)))
    return in_spec, out_spec, jax.ShapeDtypeStruct((r, c), BF16)


def _ring_fetch(tile_copy):
    n_j = pl.num_programs(1)
    step = pl.program_id(0) * n_j + pl.program_id(1)
    total = pl.num_programs(0) * n_j

    @pl.when(step == 0)
    def _():
        for k in range(WEIGHT_RING_SLOTS - 1):
            tile_copy(k).start()

    @pl.when(step + WEIGHT_RING_SLOTS - 1 < total)
    def _():
        tile_copy(step + WEIGHT_RING_SLOTS - 1).start()

    tile_copy(step).wait()
    return step % WEIGHT_RING_SLOTS


def _proj_kernel(x_ref, w_hbm, cast_in, o_ref, cast_out, w_ring, sem, *, slabs, layer, col0):
    _cast_rider(cast_in, cast_out)
    tn = w_ring.shape[2]

    def tile_copy(k):
        cols = pl.ds(pl.multiple_of(col0 + (k % pl.num_programs(1)) * tn, tn), tn)
        slot = k % WEIGHT_RING_SLOTS
        return pltpu.make_async_copy(w_hbm.at[layer, :, cols], w_ring.at[slot], sem.at[slot])

    slot = _ring_fetch(tile_copy)
    acc = jnp.dot(x_ref[...], w_ring[slot].astype(BF16), preferred_element_type=F32)
    if slabs:
        width = o_ref.shape[-1]
        for j in range(slabs):
            o_ref[j] = acc[:, j * width:(j + 1) * width].astype(o_ref.dtype)
    else:
        o_ref[...] = acc.astype(o_ref.dtype)


def _proj(h, w, layer, col0, n, out_dtype, slab_width, name, rider, tn=512):
    s, d = h.shape
    tm = s // PROJ_ROW_TILES
    n_j = n // tn
    assert rider[2] <= PROJ_ROW_TILES * n_j
    if slab_width:
        slabs = tn // slab_width
        out_shape = jax.ShapeDtypeStruct((n // slab_width, s, slab_width), out_dtype)
        out_spec = pl.BlockSpec((slabs, tm, slab_width), lambda i, j: (j, i, 0))
    else:
        slabs = 0
        out_shape = jax.ShapeDtypeStruct((s, n), out_dtype)
        out_spec = pl.BlockSpec((tm, tn), lambda i, j: (i, j))
    r_in, r_out, r_shape = _rider_specs(rider, layer, n_j)
    assert PROJ_ROW_TILES * n_j >= WEIGHT_RING_SLOTS
    return pl.pallas_call(
        functools.partial(_proj_kernel, slabs=slabs, layer=layer, col0=col0),
        grid=(s // tm, n_j),
        in_specs=[pl.BlockSpec((tm, d), lambda i, j: (i, 0)),
                  pl.BlockSpec(memory_space=pl.ANY), r_in],
        out_specs=[out_spec, r_out],
        out_shape=[out_shape, r_shape],
        scratch_shapes=[pltpu.VMEM((WEIGHT_RING_SLOTS, d, tn), w.dtype),
                        pltpu.SemaphoreType.DMA((WEIGHT_RING_SLOTS,))],
        compiler_params=_params("arbitrary", "arbitrary"),
        name=name,
    )(h, w, rider[0])


def _gates_attn_kernel(x_ref, w_hbm, cast_in, q_ref, kp_ref, km_ref, kn_ref, vp_ref, vm_ref, vn_ref,
                       bias_ref, o_ref, cast_out, ao_ref, al_ref, w_ring, sem, *, layer, col0, rows, n_blk):
    _cast_rider(cast_in, cast_out)
    tn = w_ring.shape[2]

    def tile_copy(k):
        cols = pl.ds(pl.multiple_of(col0 + (k % pl.num_programs(1)) * tn, tn), tn)
        slot = k % WEIGHT_RING_SLOTS
        return pltpu.make_async_copy(w_hbm.at[layer, :, cols], w_ring.at[slot], sem.at[slot])

    slot = _ring_fetch(tile_copy)
    o_ref[...] = jnp.dot(x_ref[...], w_ring[slot].astype(BF16),
                         preferred_element_type=F32).astype(o_ref.dtype)
    blk = (pl.program_id(0) * pl.num_programs(1) + pl.program_id(1)) % n_blk
    _attn_tiles(q_ref, kp_ref, km_ref, kn_ref, vp_ref, vm_ref, vn_ref, bias_ref, ao_ref, al_ref,
                rows=rows, dil=1, cpb=1, first_blk=blk == 0, last_blk=blk == n_blk - 1, c=0)


def _proj_gates_attn0(h, w, layer, col0, n, rider, qkv, bias, tn=512):
    s, d = h.shape
    tm = s // PROJ_ROW_TILES
    n_j = n // tn
    hp = HEADS_PER_GROUP
    n_blk = PROJ_ROW_TILES * n_j // hp
    rows = s // n_blk
    sub = rows // ATTN_SIDE
    last = s // ATTN_SIDE - 1
    assert rider[2] <= PROJ_ROW_TILES * n_j and qkv.shape[1] == 1 and rows % ATTN_QBLK == 0
    r_in, r_out, r_shape = _rider_specs(rider, layer, n_j)

    def head(i, j):
        return (i * n_j + j) // n_blk

    def blk(i, j):
        return (i * n_j + j) % n_blk

    def main(kind):
        return pl.BlockSpec((None, 1, rows, LANES), lambda i, j: (kind * hp + head(i, j), 0, blk(i, j), 0))

    def prev_blk(kind):
        return pl.BlockSpec((None, 1, ATTN_SIDE, LANES),
                            lambda i, j: (kind * hp + head(i, j), 0, jnp.maximum(blk(i, j) * sub - 1, 0), 0))

    def next_blk(kind):
        return pl.BlockSpec((None, 1, ATTN_SIDE, LANES),
                            lambda i, j: (kind * hp + head(i, j), 0,
                                          jnp.minimum((blk(i, j) + 1) * sub, last), 0))

    run_spec = pl.BlockSpec((None, rows, LANES), lambda i, j: (head(i, j), blk(i, j), 0))
    run_shape = jax.ShapeDtypeStruct((hp, s, LANES), F32)
    gates, w_b, a0, l0 = pl.pallas_call(
        functools.partial(_gates_attn_kernel, layer=layer, col0=col0, rows=rows, n_blk=n_blk),
        grid=(s // tm, n_j),
        in_specs=[pl.BlockSpec((tm, d), lambda i, j: (i, 0)),
                  pl.BlockSpec(memory_space=pl.ANY), r_in,
                  main(0), prev_blk(1), main(1), next_blk(1), prev_blk(2), main(2), next_blk(2),
                  pl.BlockSpec((None, None, 4, ATTN_QBLK, 2 * ATTN_QBLK),
                               lambda i, j: (0, head(i, j), 0, 0, 0))],
        out_specs=[pl.BlockSpec((tm, tn), lambda i, j: (i, j)), r_out, run_spec, run_spec],
        out_shape=[jax.ShapeDtypeStruct((s, n), BF16), r_shape, run_shape, run_shape],
        scratch_shapes=[pltpu.VMEM((WEIGHT_RING_SLOTS, d, tn), w.dtype),
                        pltpu.SemaphoreType.DMA((WEIGHT_RING_SLOTS,))],
        compiler_params=_params("arbitrary", "arbitrary"),
        name="proj_gates_attn_g0",
    )(h, w, rider[0], qkv, qkv, qkv, qkv, qkv, qkv, qkv, bias)
    return gates, w_b, (a0, l0)


def _regroup_matrix(dil):
    r = REGROUP_ROWS // dil
    out_row = np.arange(REGROUP_ROWS)
    src = (out_row % r) * dil + out_row // r
    return jnp.asarray(src[:, None] == np.arange(REGROUP_ROWS)[None, :], BF16)


def _proj_attn_kernel(x_ref, w_hbm, cast_in, perm_ref, o_ref, cast_out, scr, w_ring, sem, *,
                      dil, layer, col0, col_stride):
    _cast_rider(cast_in, cast_out)
    tn = w_ring.shape[2]

    def tile_copy(k):
        cols = pl.ds(pl.multiple_of(col0 + (k % pl.num_programs(1)) * col_stride, tn), tn)
        slot = k % WEIGHT_RING_SLOTS
        return pltpu.make_async_copy(w_hbm.at[layer, :, cols], w_ring.at[slot], sem.at[slot])

    slot = _ring_fetch(tile_copy)
    acc = jnp.dot(x_ref[...], w_ring[slot].astype(BF16), preferred_element_type=F32)
    if dil == 1:
        for hh in range(HEADS_PER_GROUP):
            o_ref[hh, 0] = acc[:, hh * LANES:(hh + 1) * LANES].astype(o_ref.dtype)
    elif dil < REGROUP_MATMUL_MIN_DIL:
        rows = scr.shape[1] // dil
        for hh in range(HEADS_PER_GROUP):
            scr[hh] = acc[:, hh * LANES:(hh + 1) * LANES]
        for hh in range(HEADS_PER_GROUP):
            for c in range(dil):
                o_ref[hh, c] = scr[hh, pl.ds(c, rows, stride=dil), :].astype(o_ref.dtype)
    else:
        r = REGROUP_ROWS // dil
        rounded = acc.astype(BF16)
        for b in range(acc.shape[0] // REGROUP_ROWS):
            blk = jnp.dot(perm_ref[...], rounded[b * REGROUP_ROWS:(b + 1) * REGROUP_ROWS],
                          preferred_element_type=F32).astype(o_ref.dtype)
            for hh in range(HEADS_PER_GROUP):
                for c in range(dil):
                    o_ref[hh, c, b * r:(b + 1) * r, :] = blk[c * r:(c + 1) * r,
                                                             hh * LANES:(hh + 1) * LANES]


def _proj_attn(h, w, layer, col0, gi, rider):
    s, d = h.shape
    tm = s // PROJ_ROW_TILES
    assert rider[2] <= PROJ_ROW_TILES * 3
    dil = ATTN_GROUPS[gi][1]
    tn = ATTN_OUT_WIDTH
    n_groups = len(ATTN_GROUPS)
    r_in, r_out, r_shape = _rider_specs(rider, layer, 3)
    perm = _regroup_matrix(dil)
    return pl.pallas_call(
        functools.partial(_proj_attn_kernel, dil=dil, layer=layer,
                          col0=col0 + gi * tn, col_stride=n_groups * tn),
        grid=(s // tm, 3),
        in_specs=[pl.BlockSpec((tm, d), lambda i, j: (i, 0)),
                  pl.BlockSpec(memory_space=pl.ANY), r_in,
                  pl.BlockSpec(perm.shape, lambda i, j: (0, 0))],
        out_specs=[pl.BlockSpec((HEADS_PER_GROUP, dil, tm // dil, LANES), lambda i, j: (j, 0, i, 0)),
                   r_out],
        out_shape=[jax.ShapeDtypeStruct((3 * HEADS_PER_GROUP, dil, s // dil, LANES), BF16), r_shape],
        scratch_shapes=[pltpu.VMEM((HEADS_PER_GROUP, tm, LANES), F32),
                        pltpu.VMEM((WEIGHT_RING_SLOTS, d, tn), w.dtype),
                        pltpu.SemaphoreType.DMA((WEIGHT_RING_SLOTS,))],
        compiler_params=_params("arbitrary", "arbitrary"),
        name=f"proj_attn_g{gi}",
    )(h, w, rider[0], perm)


def _hgrn_constants():
    c = CHUNK
    t = np.arange(c)[:, None]
    s = np.arange(c)[None, :]
    mats = [s <= t]
    masks = []
    for lvl in range(N_LEVELS):
        m = c >> (lvl + 1)
        p = (t // (2 * m)) * (2 * m) + m - 1
        if m < MXU_LEVEL_BELOW:
            mats.append(((t > p) & (s > p) & (s <= t)) | ((t <= p) & (s > t) & (s <= p)))
        masks.append((t // (2 * m) == s // (2 * m)) & (t % (2 * m) >= m) & (s % (2 * m) < m))
    w = np.stack(mats).astype(np.float32)
    k = np.stack(masks).astype(np.float32)
    w = np.stack([w, w[:, ::-1, ::-1]]).reshape(2, len(mats) * c, c)
    w = np.concatenate([w, w], axis=2)
    k = np.stack([k, k[:, ::-1, ::-1]])
    k = np.concatenate([k, k], axis=3)
    return jnp.asarray(w, BF16), jnp.asarray(k, BF16)


def _block_diag(x):
    zero = jnp.zeros((x.shape[0], LANES), x.dtype)
    return jnp.concatenate([jnp.concatenate([x[:, :LANES], zero], axis=1),
                            jnp.concatenate([zero, x[:, LANES:]], axis=1)], axis=0)


def _block_diag_t(x):
    zero = jnp.zeros((LANES, x.shape[1]), x.dtype)
    return jnp.concatenate([jnp.concatenate([x[:LANES], zero], axis=1),
                            jnp.concatenate([zero, x[LANES:]], axis=1)], axis=0)


def _hgrn_chunk(q, z, v, lb, wmat, m_ref, st, backward):
    half = 0.5 * (1.0 - lb)
    ct = half * jnp.tanh(0.5 * z)
    kk = half - ct
    logf = jnp.log2(jnp.maximum((lb + half) + ct, MIN_FORGET))
    hi = logf.astype(BF16)
    lo = (logf - hi.astype(F32)).astype(BF16)
    dec = jnp.dot(wmat, jnp.concatenate([hi, lo], axis=0), preferred_element_type=F32)
    b = dec[0:CHUNK]
    total = b[0:1] if backward else b[CHUNK - 1:CHUNK]
    q_in = (q * jnp.exp2(b)).astype(BF16)
    k_out = (kk * jnp.exp2(total - b)).astype(BF16)
    vb = v.astype(BF16)
    qb = q.astype(BF16)
    kb = kk.astype(BF16)
    o = jnp.dot(q_in, _block_diag_t(st.astype(BF16).T), preferred_element_type=F32)
    qk = q * kk
    diag = jnp.concatenate(
        [jnp.broadcast_to(jnp.sum(qk[:, :LANES], axis=-1, keepdims=True), (CHUNK, LANES)),
         jnp.broadcast_to(jnp.sum(qk[:, LANES:], axis=-1, keepdims=True), (CHUNK, LANES))], axis=1)
    o = o + diag * v
    a = None
    for lvl in range(N_LEVELS):
        m = CHUNK >> (lvl + 1)
        if m >= MXU_LEVEL_BELOW:
            parts = []
            for j in range(CHUNK // (2 * m)):
                lo_rows = b[2 * m * j:2 * m * j + m]
                hi_rows = b[2 * m * j + m:2 * m * (j + 1)]
                if backward:
                    piv = b[2 * m * j + m:2 * m * j + m + 1]
                    parts += [lo_rows - piv, piv - hi_rows]
                else:
                    piv = b[2 * m * j + m - 1:2 * m * j + m]
                    parts += [piv - lo_rows, hi_rows - piv]
            nl = jnp.concatenate(parts, axis=0)
        else:
            i = 1 + lvl - (N_LEVELS - N_MXU_LEVELS)
            nl = dec[i * CHUNK:(i + 1) * CHUNK]
        el = jnp.exp2(nl).astype(BF16)
        sc = jnp.dot(qb * el, _block_diag_t((kb * el).T), preferred_element_type=F32)
        sc = sc.astype(BF16) * m_ref[lvl]
        a = sc if a is None else a + sc
    o = o + jnp.dot(a, _block_diag(vb), preferred_element_type=F32)
    v_rows = jnp.concatenate([vb[:, :LANES], vb[:, LANES:]], axis=0)
    st_new = st * jnp.exp2(total) + lax.dot_general(v_rows, _block_diag(k_out), _TN,
                                                    preferred_element_type=F32)
    return o, st_new


def _hgrn_kernel(qf_ref, zf_ref, vf_ref, qb_ref, zb_ref, vb_ref, lb_ref, w_ref, m_ref,
                 of_ref, ob_ref, stf_ref, stb_ref):
    @pl.when(pl.program_id(0) == 0)
    def _():
        stf_ref[...] = jnp.zeros_like(stf_ref)
        stb_ref[...] = jnp.zeros_like(stb_ref)

    n_sub = qf_ref.shape[1] // CHUNK

    def sub_chunk(t, carry):
        rf = pl.ds(pl.multiple_of(t * CHUNK, CHUNK), CHUNK)
        rb = pl.ds(pl.multiple_of((n_sub - 1 - t) * CHUNK, CHUNK), CHUNK)
        for p in range(HGRN_HEADS // 2):
            o, st = _hgrn_chunk(qf_ref[p, rf, :], zf_ref[p, rf, :], vf_ref[p, rf, :], lb_ref[0, p],
                                w_ref[0], m_ref.at[0], stf_ref[p], False)
            of_ref[p, rf, :] = o
            stf_ref[p] = st
            o, st = _hgrn_chunk(qb_ref[p, rb, :], zb_ref[p, rb, :], vb_ref[p, rb, :], lb_ref[1, p],
                                w_ref[1], m_ref.at[1], stb_ref[p], True)
            ob_ref[p, rb, :] = o
            stb_ref[p] = st
        return carry

    lax.fori_loop(0, n_sub, sub_chunk, 0)


def _hgrn(proj_h, lb):
    s = proj_h.shape[1]
    n = s // HGRN_BLOCK
    pairs = HGRN_HEADS // 2
    wmat, masks = _hgrn_constants()
    hb = (pairs, HGRN_BLOCK, 2 * LANES)
    fwd = lambda slab: pl.BlockSpec(hb, lambda c: (slab, c, 0))
    bwd = lambda slab: pl.BlockSpec(hb, lambda c: (slab, n - 1 - c, 0))
    const = lambda a: pl.BlockSpec(a.shape, lambda c: (0,) * a.ndim)
    out_shape = jax.ShapeDtypeStruct((pairs, s, 2 * LANES), F32)
    state = pltpu.VMEM((pairs, LANES, 2 * LANES), F32)
    return pl.pallas_call(
        _hgrn_kernel,
        grid=(n,),
        in_specs=[fwd(0), fwd(1), fwd(3), bwd(0), bwd(2), bwd(3), const(lb), const(wmat), const(masks)],
        out_specs=[fwd(0), bwd(0)],
        out_shape=[out_shape, out_shape],
        scratch_shapes=[state, state],
        compiler_params=_params("arbitrary"),
        name="hgrn2_scan",
    )(proj_h, proj_h, proj_h, proj_h, proj_h, proj_h, lb, wmat, masks)


def _t5_bucket(rel):
    half = REL_BUCKETS // 2
    ret = (rel > 0).astype(np.int32) * half
    n = np.abs(rel)
    max_exact = half // 2
    large = max_exact + (np.log(np.maximum(n, 1) / max_exact)
                         / np.log(REL_MAX_DISTANCE / max_exact)
                         * (half - max_exact)).astype(np.int32)
    large = np.minimum(large, half - 1)
    return (ret + np.where(n < max_exact, n, large)).astype(np.int32)


def _attn_bias(rel_bias_table):
    span = 3 * ATTN_QBLK - 1
    rel = np.arange(span) - (ATTN_QBLK - 1) - ATTN_SIDE
    col = np.arange(2 * ATTN_QBLK)[None, :]
    band = np.abs(col - ATTN_SIDE - np.arange(ATTN_QBLK)[:, None]) <= ATTN_SIDE
    after_start, before_end = col >= ATTN_SIDE, col < 2 * ATTN_QBLK - ATTN_SIDE
    keep = [band, band & after_start, band & before_end, band & after_start & before_end]
    out = []
    for gi, (_, dil) in enumerate(ATTN_GROUPS):
        tab = rel_bias_table[:, gi * HEADS_PER_GROUP:(gi + 1) * HEADS_PER_GROUP].astype(F32)
        onehot = jnp.asarray(_t5_bucket(rel * dil)[:, None] == np.arange(REL_BUCKETS)[None, :], F32)
        vec = jnp.einsum("rb,bh->hr", onehot, tab, precision=lax.Precision.HIGHEST)
        period = span + 2
        vec = jnp.pad(vec, ((0, 0), (0, period - span)))
        flat = jnp.tile(vec, (1, ATTN_QBLK))[:, :ATTN_QBLK * (span + 1)]
        toep = flat.reshape(HEADS_PER_GROUP, ATTN_QBLK, span + 1)[:, :, ATTN_QBLK - 1:3 * ATTN_QBLK - 1]
        out.append(jnp.stack([jnp.where(jnp.asarray(k)[None], toep, NEG_INF) for k in keep], axis=1))
    return jnp.stack(out)


def _attn_tiles(q_ref, kp_ref, km_ref, kn_ref, vp_ref, vm_ref, vn_ref, bias_ref, o_s, l_s, *,
                rows, dil, cpb, first_blk, last_blk, c):
    scale = LANES ** -0.5
    n_sub = rows // ATTN_QBLK

    def window(prev_ref, main_ref, next_ref, cc, j):
        lo, hi = j * ATTN_QBLK - ATTN_SIDE, (j + 1) * ATTN_QBLK + ATTN_SIDE
        parts = []
        if lo < 0:
            parts.append(prev_ref[cc])
        parts.append(main_ref[cc, max(lo, 0):min(hi, rows)])
        if hi > rows:
            parts.append(next_ref[cc])
        return parts[0] if len(parts) == 1 else jnp.concatenate(parts, axis=0)

    tiles = [(cc, j) for cc in range(cpb) for j in range(n_sub)]
    for t0 in range(0, len(tiles), ATTN_QK_BATCH):
        batch = tiles[t0:t0 + ATTN_QK_BATCH]
        scores = []
        for cc, j in batch:
            r0 = j * ATTN_QBLK
            variant = 0
            if j == 0:
                variant = variant + first_blk.astype(jnp.int32)
            if j == n_sub - 1:
                variant = variant + 2 * last_blk.astype(jnp.int32)
            scores.append(lax.dot_general(q_ref[cc, r0:r0 + ATTN_QBLK],
                                          window(kp_ref, km_ref, kn_ref, cc, j), _NT,
                                          preferred_element_type=F32) * scale + bias_ref[variant])
        for (cc, j), s in zip(batch, scores):
            r0 = j * ATTN_QBLK
            m = jnp.max(s, axis=-1, keepdims=True)
            p = jnp.exp(s - m)
            l = jnp.sum(p, axis=-1, keepdims=True)
            o = jnp.dot(p.astype(BF16), window(vp_ref, vm_ref, vn_ref, cc, j),
                        preferred_element_type=F32) / l
            lse = jnp.broadcast_to(m + jnp.log(l), (ATTN_QBLK, LANES))
            if dil == 1:
                o_s[r0:r0 + ATTN_QBLK] = o
                l_s[r0:r0 + ATTN_QBLK] = lse
            else:
                start = r0 * dil + c * cpb + cc
                o_s[pl.ds(start, ATTN_QBLK, stride=dil), :] = o
                l_s[pl.ds(start, ATTN_QBLK, stride=dil), :] = lse


def _attn_kernel(q_ref, kp_ref, km_ref, kn_ref, vp_ref, vm_ref, vn_ref, bias_ref, *rest,
                 rows, dil, cpb, has_prev, final):
    rest = list(rest)
    op_ref, lp_ref = (rest.pop(0), rest.pop(0)) if has_prev else (None, None)
    o_ref = rest.pop(0)
    lse_ref = None if final else rest.pop(0)
    o_s, l_s = rest if rest else (o_ref, lse_ref)
    c = pl.program_id(2)
    i = pl.program_id(1)
    _attn_tiles(q_ref, kp_ref, km_ref, kn_ref, vp_ref, vm_ref, vn_ref, bias_ref, o_s, l_s,
                rows=rows, dil=dil, cpb=cpb, first_blk=i == 0, last_blk=i == pl.num_programs(1) - 1, c=c)

    if has_prev:
        @pl.when(c == dil // cpb - 1)
        def _():
            def fold(t, carry):
                r = pl.ds(pl.multiple_of(t * ATTN_FOLD_ROWS, ATTN_FOLD_ROWS), ATTN_FOLD_ROWS)
                lp, lc = lp_ref[r, :], l_s[r, :]
                m = jnp.maximum(lp, lc)
                wp, wc = jnp.exp(lp - m), jnp.exp(lc - m)
                den = wp + wc
                o = (wp * op_ref[r, :] + wc * o_s[r, :]) / den
                o_ref[r, :] = o.astype(o_ref.dtype)
                if not final:
                    lse_ref[r, :] = m + jnp.log(den)
                return carry

            lax.fori_loop(0, rows * dil // ATTN_FOLD_ROWS, fold, 0)


def _attn_group(qkv, bias, gi, prev):
    dil, length = qkv.shape[1], qkv.shape[2]
    s = dil * length
    cpb = min(ATTN_RESIDUES, dil)
    rows = min(ATTN_TILES * ATTN_QBLK // cpb, length)
    nblk = length // rows
    sub = rows // ATTN_SIDE
    last = length // ATTN_SIDE - 1
    hp = HEADS_PER_GROUP
    final = gi == len(ATTN_GROUPS) - 1
    has_prev = prev is not None
    assert has_prev or dil == 1

    def main(kind):
        return pl.BlockSpec((None, cpb, rows, LANES), lambda g, i, c: (kind * hp + g, c, i, 0))

    def prev_blk(kind):
        return pl.BlockSpec((None, cpb, ATTN_SIDE, LANES),
                            lambda g, i, c: (kind * hp + g, c, jnp.maximum(i * sub - 1, 0), 0))

    def next_blk(kind):
        return pl.BlockSpec((None, cpb, ATTN_SIDE, LANES),
                            lambda g, i, c: (kind * hp + g, c, jnp.minimum((i + 1) * sub, last), 0))

    run_spec = pl.BlockSpec((None, rows * dil, LANES), lambda g, i, c: (g, i, 0))
    run_shape = jax.ShapeDtypeStruct((hp, s, LANES), F32)
    in_specs = [main(0), prev_blk(1), main(1), next_blk(1), prev_blk(2), main(2), next_blk(2),
                pl.BlockSpec((None, None, 4, ATTN_QBLK, 2 * ATTN_QBLK), lambda g, i, c: (gi, g, 0, 0, 0))]
    args = [qkv] * 7 + [bias]
    if has_prev:
        in_specs += [run_spec, run_spec]
        args += list(prev)
    if final:
        out_specs = pl.BlockSpec((rows * dil, LANES), lambda g, i, c: (i, g))
        out_shape = jax.ShapeDtypeStruct((s, hp * LANES), BF16)
    else:
        out_specs = [run_spec, run_spec]
        out_shape = [run_shape, run_shape]
    scratch = []
    if has_prev:
        scratch = [pltpu.VMEM((rows * dil, LANES), F32), pltpu.VMEM((rows * dil, LANES), F32)]
    return pl.pallas_call(
        functools.partial(_attn_kernel, rows=rows, dil=dil, cpb=cpb, has_prev=has_prev, final=final),
        grid=(hp, nblk, dil // cpb),
        in_specs=in_specs,
        out_specs=out_specs,
        out_shape=out_shape,
        scratch_shapes=scratch,
        compiler_params=_params("parallel", "arbitrary", "arbitrary"),
        name=f"dilated_attn_g{gi}",
    )(*args)


def _merge_kernel(of_ref, ob_ref, g_ref, oa_ref, gate_ref, x_ref, pa_ref, pb_ref, wo_ref, hw_ref,
                  nw_ref, xo_ref, ho_ref):
    d = x_ref.shape[1]
    b = jnp.dot(oa_ref[...], pb_ref[...], preferred_element_type=F32)
    a = None
    for h0 in range(0, HGRN_HEADS, MERGE_HEAD_GROUP):
        cols = []
        for h in range(h0, h0 + MERGE_HEAD_GROUP):
            lanes = slice((h % 2) * LANES, (h % 2 + 1) * LANES)
            o = of_ref[h // 2, :, lanes] + ob_ref[h // 2, :, lanes]
            o = o * lax.rsqrt(jnp.mean(o * o, axis=-1, keepdims=True) + NORM_EPS)
            g = g_ref[h // 2, :, lanes]
            cols.append((o * hw_ref[h] * (g * _sigmoid(g))).astype(BF16))
        part = jnp.dot(jnp.concatenate(cols, axis=1), pa_ref[h0 * LANES:(h0 + MERGE_HEAD_GROUP) * LANES, :],
                       preferred_element_type=F32)
        a = part if a is None else a + part
    merged = _sigmoid(gate_ref[:, :d]) * a.astype(BF16) + _sigmoid(gate_ref[:, d:]) * b.astype(BF16)
    xn = x_ref[...] + jnp.dot(merged, wo_ref[...], preferred_element_type=F32)
    xo_ref[...] = xn
    ho_ref[...] = _rms(xn, nw_ref[...]).astype(ho_ref.dtype)


def _merge(o_hgrn, proj_h, o_attn, gates, x, pa, pb, wo, hgrn_norm_w, norm_w, tm=256):
    s, d = x.shape
    pairs = HGRN_HEADS // 2
    hspec = pl.BlockSpec((pairs, tm, 2 * LANES), lambda i: (0, i, 0))

    def wspec(rows):
        return pl.BlockSpec((rows, d), lambda i: (0, 0))

    return pl.pallas_call(
        _merge_kernel,
        grid=(s // tm,),
        in_specs=[hspec, hspec,
                  pl.BlockSpec((pairs, tm, 2 * LANES), lambda i: (4, i, 0)),
                  pl.BlockSpec((tm, ATTN_OUT_WIDTH), lambda i: (i, 0)),
                  pl.BlockSpec((tm, 2 * d), lambda i: (i, 0)),
                  pl.BlockSpec((tm, d), lambda i: (i, 0)),
                  wspec(HGRN_WIDTH), wspec(ATTN_OUT_WIDTH), wspec(d),
                  pl.BlockSpec((HGRN_HEADS, 1, LANES), lambda i: (0, 0, 0)),
                  pl.BlockSpec((1, d), lambda i: (0, 0))],
        out_specs=[pl.BlockSpec((tm, d), lambda i: (i, 0)),
                   pl.BlockSpec((tm, d), lambda i: (i, 0))],
        out_shape=[jax.ShapeDtypeStruct((s, d), F32), jax.ShapeDtypeStruct((s, d), BF16)],
        compiler_params=_params("parallel"),
        name="merge_out_proj",
    )(o_hgrn[0], o_hgrn[1], proj_h, o_attn, gates, x, pa, pb, wo,
      hgrn_norm_w.reshape(HGRN_HEADS, 1, LANES), norm_w.reshape(1, d))


def _mlp_kernel(h_ref, x_ref, wu_hbm, wd_hbm, nw_ref, xo_ref, ho_ref, acc, wu_ring, wd_ring, sem):
    f = pl.program_id(1)
    tf = wu_ring.shape[2]

    def up_copy(k):
        cols = pl.ds(pl.multiple_of((k % pl.num_programs(1)) * tf, tf), tf)
        slot = k % WEIGHT_RING_SLOTS
        return pltpu.make_async_copy(wu_hbm.at[:, cols], wu_ring.at[slot], sem.at[0, slot])

    def down_copy(k):
        rows = pl.ds(pl.multiple_of((k % pl.num_programs(1)) * tf, tf), tf)
        slot = k % WEIGHT_RING_SLOTS
        return pltpu.make_async_copy(wd_hbm.at[rows, :], wd_ring.at[slot], sem.at[1, slot])

    slot = _ring_fetch(up_copy)
    _ring_fetch(down_copy)

    @pl.when(f == 0)
    def _():
        acc[...] = x_ref[...]

    u = jnp.maximum(jnp.dot(h_ref[...], wu_ring[slot], preferred_element_type=F32), 0.0)
    acc[...] += jnp.dot((u * u).astype(BF16), wd_ring[slot], preferred_element_type=F32)

    @pl.when(f == pl.num_programs(1) - 1)
    def _():
        xn = acc[...]
        xo_ref[...] = xn
        ho_ref[...] = _rms(xn, nw_ref[...]).astype(ho_ref.dtype)


def _mlp(h, x, wu, wd, norm_w, h_dtype, tm=512, tf=1024):
    s, d = x.shape
    ff = wu.shape[1]
    return pl.pallas_call(
        _mlp_kernel,
        grid=(s // tm, ff // tf),
        in_specs=[pl.BlockSpec((tm, d), lambda i, f: (i, 0)),
                  pl.BlockSpec((tm, d), lambda i, f: (i, 0)),
                  pl.BlockSpec(memory_space=pl.ANY),
                  pl.BlockSpec(memory_space=pl.ANY),
                  pl.BlockSpec((1, d), lambda i, f: (0, 0))],
        out_specs=[pl.BlockSpec((tm, d), lambda i, f: (i, 0)),
                   pl.BlockSpec((tm, d), lambda i, f: (i, 0))],
        out_shape=[jax.ShapeDtypeStruct((s, d), F32), jax.ShapeDtypeStruct((s, d), h_dtype)],
        scratch_shapes=[pltpu.VMEM((tm, d), F32),
                        pltpu.VMEM((WEIGHT_RING_SLOTS, d, tf), wu.dtype),
                        pltpu.VMEM((WEIGHT_RING_SLOTS, tf, d), wd.dtype),
                        pltpu.SemaphoreType.DMA((2, WEIGHT_RING_SLOTS))],
        compiler_params=pltpu.CompilerParams(dimension_semantics=("arbitrary", "arbitrary"),
                                             vmem_limit_bytes=MLP_VMEM_LIMIT),
        name="mlp_relu2",
    )(h, x, wu, wd, norm_w.reshape(1, d))


def _lower_bounds(logits):
    p = jax.nn.softmax(logits.astype(F32), axis=0)
    return jnp.cumsum(p, axis=0) - p[0:1]


def kernel(x, w_in, hgrn_lb_fwd, hgrn_lb_bwd, hgrn_norm_w, rel_bias_table, w_branch_hgrn,
           w_branch_attn, w_out, norm_mix_w, norm_mlp_w, w_up, w_down, final_norm_w):
    batch, s, d = x.shape
    depth = w_in.shape[0]
    assert batch == 1
    x = x.reshape(s, d)
    lb = jnp.stack([_lower_bounds(hgrn_lb_fwd), _lower_bounds(hgrn_lb_bwd)], axis=1)
    lb = lb.reshape(-1, 2, HGRN_HEADS // 2, 1, 2 * LANES)
    bias = _attn_bias(rel_bias_table)
    n_h = 5 * HGRN_WIDTH
    n_a = 3 * ATTN_WIDTH
    cast_rows = 256
    h = _norm(x, norm_mix_w[0])
    for l in range(depth):
        proj_h, wd_b = _proj(h, w_in, l, 0, n_h, F32, 2 * LANES, "proj_hgrn",
                             (w_down, 0, w_down.shape[1] // cast_rows))
        qkv, wo_b = _proj_attn(h, w_in, l, n_h, 0, (w_out, 0, w_out.shape[1] // cast_rows))
        gates, wu_b, attn = _proj_gates_attn0(h, w_in, l, n_h + n_a, 2 * d,
                                              (w_up, 1, w_up.shape[2] // cast_rows), qkv, bias)
        o_hgrn = _hgrn(proj_h, lb[l])
        cast = []
        for gi, wt in ((1, w_branch_hgrn), (2, w_branch_attn)):
            qkv, wt_b = _proj_attn(h, w_in, l, n_h, gi, (wt, 0, wt.shape[1] // cast_rows))
            attn = _attn_group(qkv, bias, gi, attn)
            cast.append(wt_b)
        pa_b, pb_b = cast
        x, h2 = _merge(o_hgrn, proj_h, attn, gates, x, pa_b, pb_b, wo_b, hgrn_norm_w[l],
                       norm_mlp_w[l])
        last = l == depth - 1
        x, h = _mlp(h2, x, wu_b, wd_b, final_norm_w if last else norm_mix_w[l + 1],
                    F32 if last else BF16)
    return h.reshape(batch, s, d)
```
